```python
import jax, jax.numpy as jnp
from jax import lax
import numpy as np

D_MODEL = 1024
BATCH = 4
SEQ = 4096
DEPTH = 1

CHUNK = 64
Q_BLOCK = 128
SB_HEADS = 8
SB_HEAD_DIM = 64
SB_WIDTH = SB_HEADS * SB_HEAD_DIM
CA_HEADS = 8
CA_HEAD_DIM = 64
CA_WIDTH = CA_HEADS * CA_HEAD_DIM
CA_LEFT_CHUNKS = 8
CA_BAND = (CA_LEFT_CHUNKS + 1) * CHUNK
REL_CLIP = 256
N_BRANCH = 2
PROJ_WIDTH = 3 * SB_WIDTH + 3 * CA_WIDTH
N_EXPERTS = 32
TOP_K = 4
D_EXPERT = D_MODEL
SWIGLU_ALPHA = 1.702
SWIGLU_LIMIT = 7.0
MOE_BLOCK = 128
RMS_EPS = 1e-6
N_MOD = 6

kernel_name = 'hybrid_stickbreak_chunkattn_moe_block'


def rms_norm(x, g):
    xf = x.astype(jnp.float32)
    y = xf * lax.rsqrt(jnp.mean(xf * xf, axis=-1, keepdims=True) + RMS_EPS)
    return (y * g.astype(jnp.float32)).astype(x.dtype)


def to_heads(t, n_heads):
    b, s, w = t.shape
    return t.reshape(b, s, n_heads, w // n_heads).transpose(0, 2, 1, 3)


def from_heads(t):
    b, h, s, dh = t.shape
    return t.transpose(0, 2, 1, 3).reshape(b, s, h * dh)


def stick_breaking_attention(q, k, v):
    b, h, s, d = q.shape
    n_blk = s // Q_BLOCK
    scale = d ** -0.5
    q_blocks = q.reshape(b, h, n_blk, Q_BLOCK, d).transpose(2, 0, 1, 3, 4)
    key_pos = jnp.arange(s)

    def one_block(args):
        q_blk, blk = args
        z = jnp.einsum('bhqd,bhkd->bhqk', q_blk, k).astype(jnp.float32) * scale
        q_pos = blk * Q_BLOCK + jnp.arange(Q_BLOCK)
        causal = key_pos[None, :] < q_pos[:, None]
        sp = jnp.where(causal, jax.nn.softplus(z), 0.0)
        between = lax.cumsum(sp, axis=3, reverse=True) - sp
        a = jnp.where(causal, jnp.exp(jax.nn.log_sigmoid(z) - between), 0.0)
        return jnp.einsum('bhqk,bhkd->bhqd', a.astype(v.dtype), v)

    o = lax.map(one_block, (q_blocks, jnp.arange(n_blk)))
    return o.transpose(1, 2, 0, 3, 4).reshape(b, h, s, d)


def chunked_band_attention(q, k, v, rel_bias):
    b, h, s, d = q.shape
    n_chunk = s // CHUNK
    pad = CA_LEFT_CHUNKS * CHUNK
    scale = d ** -0.5
    kp = jnp.pad(k, ((0, 0), (0, 0), (pad, 0), (0, 0)))
    vp = jnp.pad(v, ((0, 0), (0, 0), (pad, 0), (0, 0)))
    band_idx = jnp.arange(n_chunk)[:, None] * CHUNK + jnp.arange(CA_BAND)[None, :]
    k_band = kp[:, :, band_idx]
    v_band = vp[:, :, band_idx]
    q_chunk = q.reshape(b, h, n_chunk, CHUNK, d)
    z = jnp.einsum('bhcqd,bhckd->bhcqk', q_chunk, k_band).astype(jnp.float32) * scale
    rel = pad + jnp.arange(CHUNK)[:, None] - jnp.arange(CA_BAND)[None, :]
    bias = rel_bias[:, jnp.clip(rel, -REL_CLIP, REL_CLIP) + REL_CLIP]
    z = z + bias[None, :, None, :, :].astype(jnp.float32)
    valid = band_idx >= pad
    z = jnp.where(valid[None, None, :, None, :], z, -jnp.inf)
    p = jax.nn.softmax(z, axis=-1)
    o = jnp.einsum('bhcqk,bhckd->bhcqd', p.astype(v.dtype), v_band)
    return o.reshape(b, h, s, d)


def clamped_swiglu(hh):
    gate, up = jnp.split(hh, 2, axis=-1)
    gate = jnp.minimum(gate, SWIGLU_LIMIT)
    up = jnp.clip(up, -SWIGLU_LIMIT, SWIGLU_LIMIT)
    return gate * jax.nn.sigmoid(SWIGLU_ALPHA * gate) * (up + 1.0)


def moe_ffn(h, w_router, b_router, w1, b1, w2, b2):
    n, d = h.shape
    logits = (h @ w_router + b_router).astype(jnp.float32)
    top_logit, top_e = lax.top_k(logits, TOP_K)
    top_w = jax.nn.softmax(top_logit, axis=-1)
    nk = n * TOP_K
    flat_e = top_e.reshape(nk)
    flat_tok = jnp.arange(nk, dtype=jnp.int32) // TOP_K
    flat_w = top_w.reshape(nk)
    order = jnp.argsort(flat_e)
    sorted_e = flat_e[order]
    counts = jnp.bincount(flat_e, length=N_EXPERTS)
    padded = (counts + MOE_BLOCK - 1) // MOE_BLOCK * MOE_BLOCK
    pad_end = jnp.cumsum(padded)
    pad_start = pad_end - padded
    grp_start = jnp.cumsum(counts) - counts
    dest = pad_start[sorted_e] + jnp.arange(nk) - grp_start[sorted_e]
    n_blocks = -(-nk // MOE_BLOCK) + N_EXPERTS
    rows = n_blocks * MOE_BLOCK
    row_tok = jnp.zeros((rows,), jnp.int32).at[dest].set(flat_tok[order])
    row_w = jnp.zeros((rows,), jnp.float32).at[dest].set(flat_w[order])
    block_e = jnp.minimum(
        jnp.searchsorted(pad_end, jnp.arange(n_blocks) * MOE_BLOCK, side='right'), N_EXPERTS - 1)
    xs = h[row_tok].reshape(n_blocks, MOE_BLOCK, d)

    def expert_block(args):
        xb, e = args
        hid = clamped_swiglu(xb @ w1[e] + b1[e])
        return hid @ w2[e] + b2[e]

    ys = lax.map(expert_block, (xs, block_e)).reshape(rows, d)
    out = jax.ops.segment_sum(ys.astype(jnp.float32) * row_w[:, None], row_tok, num_segments=n)
    return out.astype(h.dtype)


def setup_inputs(seed: int = 0) -> dict:
    key = jax.random.key(seed)
    ks = jax.random.split(key, 24)
    L, D, E, I = DEPTH, D_MODEL, N_EXPERTS, D_EXPERT
    nrm = lambda k, shape, s: jax.random.normal(k, shape, jnp.float32) * s
    return {
        'x': nrm(ks[0], (BATCH, SEQ, D), 1.0),
        'c': nrm(ks[1], (BATCH, D), 1.0),
        'w_ada': nrm(ks[2], (L, D, N_MOD * D), 0.25 * D ** -0.5),
        'b_ada': nrm(ks[3], (L, N_MOD * D), 0.02),
        'g_pre_mix': 1.0 + nrm(ks[4], (L, D), 0.05),
        'g_post_mix': 1.0 + nrm(ks[5], (L, D), 0.05),
        'w_in': nrm(ks[6], (L, D, PROJ_WIDTH), D ** -0.5),
        'rel_bias': nrm(ks[7], (L, CA_HEADS, 2 * REL_CLIP + 1), 0.5),
        'w_branch_sb': nrm(ks[8], (L, SB_WIDTH, D), SB_WIDTH ** -0.5),
        'w_branch_ca': nrm(ks[9], (L, CA_WIDTH, D), CA_WIDTH ** -0.5),
        'w_gate': nrm(ks[10], (L, D, N_BRANCH * D), D ** -0.5),
        'b_gate': nrm(ks[11], (L, N_BRANCH * D), 0.02),
        'w_out': nrm(ks[12], (L, D, D), D ** -0.5),
        'g_pre_ffn': 1.0 + nrm(ks[13], (L, D), 0.05),
        'g_post_ffn': 1.0 + nrm(ks[14], (L, D), 0.05),
        'w_router': nrm(ks[15], (L, D, E), D ** -0.5),
        'b_router': nrm(ks[16], (L, E), 0.01),
        'w_e1': nrm(ks[17], (L, E, D, 2 * I), D ** -0.5),
        'b_e1': nrm(ks[18], (L, E, 2 * I), 0.02),
        'w_e2': nrm(ks[19], (L, E, I, D), I ** -0.5),
        'b_e2': nrm(ks[20], (L, E, D), 0.02),
    }


def reference(x, c, w_ada, b_ada, g_pre_mix, g_post_mix, w_in, rel_bias, w_branch_sb, w_branch_ca,
              w_gate, b_gate, w_out, g_pre_ffn, g_post_ffn, w_router, b_router, w_e1, b_e1, w_e2, b_e2):
    b, s, d = x.shape
    for l in range(DEPTH):
        mod = (c @ w_ada[l] + b_ada[l])[:, None, :]
        sh_m, sc_m, gt_m, sh_f, sc_f, gt_f = jnp.split(mod, N_MOD, axis=-1)

        h = rms_norm(x, g_pre_mix[l]) * (1.0 + sc_m) + sh_m
        proj = h @ w_in[l]
        q_sb, k_sb, v_sb, q_ca, k_ca, v_ca = jnp.split(proj, 6, axis=-1)
        o_sb = stick_breaking_attention(to_heads(q_sb, SB_HEADS), to_heads(k_sb, SB_HEADS),
                                        to_heads(v_sb, SB_HEADS))
        o_ca = chunked_band_attention(to_heads(q_ca, CA_HEADS), to_heads(k_ca, CA_HEADS),
                                      to_heads(v_ca, CA_HEADS), rel_bias[l])
        y_sb = from_heads(o_sb) @ w_branch_sb[l]
        y_ca = from_heads(o_ca) @ w_branch_ca[l]
        g_sb, g_ca = jnp.split(jax.nn.sigmoid(h @ w_gate[l] + b_gate[l]), N_BRANCH, axis=-1)
        mixed = (g_sb * y_sb + g_ca * y_ca) @ w_out[l]
        x = x + gt_m * rms_norm(mixed, g_post_mix[l])

        h2 = rms_norm(x, g_pre_ffn[l]) * (1.0 + sc_f) + sh_f
        ff = moe_ffn(h2.reshape(b * s, d), w_router[l], b_router[l], w_e1[l], b_e1[l],
                     w_e2[l], b_e2[l]).reshape(b, s, d)
        x = x + gt_f * rms_norm(ff, g_post_ffn[l])
    return x
```

```python
import functools

import jax
import jax.numpy as jnp
from jax import lax
from jax.experimental import pallas as pl
from jax.experimental.pallas import tpu as pltpu

F32 = jnp.float32
BF16 = jnp.bfloat16
I32 = jnp.int32

RMS_EPS = 1e-6
HEAD_DIM = 64
LANES = 128
CHUNK = 64
CA_LEFT_CHUNKS = 8
REL_CLIP = 256
N_EXPERTS = 32
TOP_K = 4
SWIGLU_ALPHA = 1.702
SWIGLU_LIMIT = 7.0
NEG = -1e30

SB_TILE = 256
CA_TILE = 2 * CHUNK
CA_KEY_TILES = CA_LEFT_CHUNKS * CHUNK // CA_TILE + 1
ROW_TILE = 512
MOE_BLOCK = 256
VMEM_LIMIT = 56 * 1024 * 1024


def _cparams(sem):
    return pltpu.CompilerParams(dimension_semantics=sem, vmem_limit_bytes=VMEM_LIMIT)


def _dot(a, b):
    return jnp.dot(a, b, preferred_element_type=F32)


def _dot_nt(a, b):
    return lax.dot_general(a, b, (((1,), (1,)), ((), ())), preferred_element_type=F32)


def _rms(x, g):
    return x * lax.rsqrt(jnp.mean(x * x, axis=-1, keepdims=True) + RMS_EPS) * g


def _mod_kernel(c_ref, w_ref, b_ref, o_ref):
    o_ref[...] = jnp.dot(c_ref[...], w_ref[...], preferred_element_type=F32,
                         precision=lax.Precision.HIGHEST) + b_ref[...]


def _modulation(c, w_ada, b_ada):
    b, d = c.shape
    n_out = w_ada.shape[1]
    rows = 8
    c_pad = jnp.zeros((rows, d), F32).at[:b].set(c)
    out = pl.pallas_call(
        _mod_kernel,
        grid=(n_out // d,),
        in_specs=[pl.BlockSpec((rows, d), lambda j: (0, 0)),
                  pl.BlockSpec((d, d), lambda j: (0, j)),
                  pl.BlockSpec((1, d), lambda j: (0, j))],
        out_specs=pl.BlockSpec((rows, d), lambda j: (0, j)),
        out_shape=jax.ShapeDtypeStruct((rows, n_out), F32),
        compiler_params=_cparams(("arbitrary",)),
        name="adaln_mod",
    )(c_pad, w_ada, b_ada.reshape(1, n_out))
    return out[:b].reshape(b, n_out // d, d)


def _premix_kernel(x_ref, mod_ref, g_ref, win_ref, wg_ref, bg_ref, proj_ref, gate_ref, *, col):
    h = _rms(x_ref[...], g_ref[...]) * (1.0 + mod_ref[0, 1:2, :]) + mod_ref[0, 0:1, :]
    hb = h.astype(BF16)
    for c0 in range(0, proj_ref.shape[1], col):
        proj_ref[:, c0:c0 + col] = _dot(hb, win_ref[:, c0:c0 + col]).astype(BF16)
    for c0 in range(0, gate_ref.shape[1], col):
        g = jax.nn.sigmoid(_dot(hb, wg_ref[:, c0:c0 + col]) + bg_ref[:, c0:c0 + col])
        gate_ref[:, c0:c0 + col] = g.astype(BF16)


def _premix(x2, mod, g_pre, w_in_b, w_gate_b, b_gate, seq):
    n, d = x2.shape
    pw, gw = w_in_b.shape[1], w_gate_b.shape[1]
    tm = min(ROW_TILE, seq)
    per_b = seq // tm
    return pl.pallas_call(
        functools.partial(_premix_kernel, col=512),
        grid=(n // tm,),
        in_specs=[pl.BlockSpec((tm, d), lambda i: (i, 0)),
                  pl.BlockSpec((1,) + mod.shape[1:], lambda i: (i // per_b, 0, 0)),
                  pl.BlockSpec((1, d), lambda i: (0, 0)),
                  pl.BlockSpec((d, pw), lambda i: (0, 0)),
                  pl.BlockSpec((d, gw), lambda i: (0, 0)),
                  pl.BlockSpec((1, gw), lambda i: (0, 0))],
        out_specs=[pl.BlockSpec((tm, pw), lambda i: (i, 0)),
                   pl.BlockSpec((tm, gw), lambda i: (i, 0))],
        out_shape=[jax.ShapeDtypeStruct((n, pw), BF16), jax.ShapeDtypeStruct((n, gw), BF16)],
        compiler_params=_cparams(("arbitrary",)),
        name="premix",
    )(x2, mod, g_pre.reshape(1, d), w_in_b, w_gate_b, b_gate.reshape(1, gw))


def _sb_kernel(q_ref, k_ref, v_ref, u_ref, o_ref):
    i = pl.program_id(2)
    t = q_ref.shape[0]
    q2 = q_ref[...]
    lane = lax.broadcasted_iota(I32, (t, LANES), 1)
    row = lax.broadcasted_iota(I32, (t, t), 0)
    colk = lax.broadcasted_iota(I32, (t, t), 1)
    causal = colk < row
    u = u_ref[...]

    def visit(qh, j, carry, acc, diagonal):
        start = pl.multiple_of(j * t, t)
        kt = k_ref[pl.ds(start, t), :]
        vt = v_ref[pl.ds(start, t), :]
        z = _dot_nt(qh, kt)
        sp = jnp.maximum(z, 0.0) + jnp.log1p(jnp.exp(-jnp.abs(z)))
        if diagonal:
            sp = jnp.where(causal, sp, 0.0)
        hi = sp.astype(BF16)
        lo = (sp - hi.astype(F32)).astype(BF16)
        cum = _dot(hi, u) + _dot(lo, u)
        a = jnp.exp(z - cum - carry)
        if diagonal:
            a = jnp.where(causal, a, 0.0)
        acc = acc + _dot(a.astype(BF16), vt)
        return carry + cum[:, 0:1], acc

    outs = []
    for hh in range(2):
        in_head = (lane < HEAD_DIM) if hh == 0 else (lane >= HEAD_DIM)
        qh = jnp.where(in_head, q2, jnp.zeros_like(q2))
        carry, acc = visit(qh, i, jnp.zeros((t, 1), F32), jnp.zeros((t, LANES), F32), True)

        def body(n, ca, qh=qh):
            return visit(qh, i - 1 - n, ca[0], ca[1], False)

        carry, acc = lax.fori_loop(0, i, body, (carry, acc))
        outs.append(acc)
    o_ref[...] = jnp.where(lane < HEAD_DIM, outs[0], outs[1]).astype(o_ref.dtype)


def _sb_attention(proj, batch, seq, width, q_col, k_col, v_col):
    n = proj.shape[0]
    t = min(SB_TILE, seq)
    nq = seq // t
    pairs = width // LANES
    tri = (jnp.arange(t)[:, None] >= jnp.arange(t)[None, :]).astype(BF16)
    return pl.pallas_call(
        _sb_kernel,
        grid=(batch, pairs, nq),
        in_specs=[pl.BlockSpec((t, LANES), lambda b, p, i: (b * nq + i, q_col + p)),
                  pl.BlockSpec((seq, LANES), lambda b, p, i: (b, k_col + p)),
                  pl.BlockSpec((seq, LANES), lambda b, p, i: (b, v_col + p)),
                  pl.BlockSpec((t, t), lambda b, p, i: (0, 0))],
        out_specs=pl.BlockSpec((t, LANES), lambda b, p, i: (b * nq + i, p)),
        out_shape=jax.ShapeDtypeStruct((n, width), BF16),
        compiler_params=_cparams(("arbitrary", "arbitrary", "arbitrary")),
        name="stickbreak_attn",
    )(proj, proj, proj, tri)


def _ca_kernel(q_ref, k_ref, v_ref, t_ref, o_ref):
    r = pl.program_id(2)
    t = q_ref.shape[0]
    q2 = q_ref[...]
    lane = lax.broadcasted_iota(I32, (t, LANES), 1)
    outs = []
    for hh in range(2):
        in_head = (lane < HEAD_DIM) if hh == 0 else (lane >= HEAD_DIM)
        qh = jnp.where(in_head, q2, jnp.zeros_like(q2))
        zs, vts = [], []
        for tb in range(CA_KEY_TILES):
            kt_idx = r - (CA_KEY_TILES - 1) + tb
            start = pl.multiple_of(jnp.maximum(kt_idx, 0) * t, t)
            z = _dot_nt(qh, k_ref[pl.ds(start, t), :]) + t_ref[hh, :, tb * t:(tb + 1) * t]
            zs.append(jnp.where(kt_idx >= 0, z, NEG))
            vts.append(v_ref[pl.ds(start, t), :])
        m = functools.reduce(jnp.maximum, [jnp.max(z, axis=-1, keepdims=True) for z in zs])
        ps = [jnp.exp(z - m) for z in zs]
        denom = functools.reduce(jnp.add, [jnp.sum(p, axis=-1, keepdims=True) for p in ps])
        acc = functools.reduce(jnp.add, [_dot(p.astype(BF16), vt) for p, vt in zip(ps, vts)])
        outs.append(acc / denom)
    o_ref[...] = jnp.where(lane < HEAD_DIM, outs[0], outs[1]).astype(o_ref.dtype)


def _ca_table(rel_bias):
    i = jnp.arange(CA_TILE)[:, None]
    j = jnp.arange(CA_KEY_TILES * CA_TILE)[None, :]
    rel = CA_LEFT_CHUNKS * CHUNK + i - j
    bias = rel_bias[:, jnp.clip(rel, -REL_CLIP, REL_CLIP) + REL_CLIP].astype(F32)
    band = (CA_LEFT_CHUNKS + 1) * CHUNK
    first = (i // CHUNK) * CHUNK
    visible = (j >= first) & (j < first + band)
    return jnp.where(visible[None], bias, NEG)


def _ca_attention(proj, table, batch, seq, width, q_col, k_col, v_col):
    n = proj.shape[0]
    t = CA_TILE
    nq = seq // t
    pairs = width // LANES
    return pl.pallas_call(
        _ca_kernel,
        grid=(batch, pairs, nq),
        in_specs=[pl.BlockSpec((t, LANES), lambda b, p, i: (b * nq + i, q_col + p)),
                  pl.BlockSpec((seq, LANES), lambda b, p, i: (b, k_col + p)),
                  pl.BlockSpec((seq, LANES), lambda b, p, i: (b, v_col + p)),
                  pl.BlockSpec((2, t, CA_KEY_TILES * t), lambda b, p, i: (p, 0, 0))],
        out_specs=pl.BlockSpec((t, LANES), lambda b, p, i: (b * nq + i, p)),
        out_shape=jax.ShapeDtypeStruct((n, width), BF16),
        compiler_params=_cparams(("arbitrary", "arbitrary", "arbitrary")),
        name="chunkband_attn",
    )(proj, proj, proj, table)


def _postmix_kernel(osb_ref, oca_ref, gate_ref, x_ref, mod_ref, gpost_ref, gffn_ref, wbs_ref, wbc_ref,
                    wout_ref, wr_ref, br_ref, x1_ref, h2_ref, tope_ref, topw_ref):
    d = x_ref.shape[1]
    y_sb = _dot(osb_ref[...], wbs_ref[...])
    y_ca = _dot(oca_ref[...], wbc_ref[...])
    mixed_in = gate_ref[:, :d].astype(F32) * y_sb + gate_ref[:, d:].astype(F32) * y_ca
    mixed = _dot(mixed_in.astype(BF16), wout_ref[...])
    x1 = x_ref[...] + mod_ref[0, 2:3, :] * _rms(mixed, gpost_ref[...])
    x1_ref[...] = x1
    h2 = _rms(x1, gffn_ref[...]) * (1.0 + mod_ref[0, 4:5, :]) + mod_ref[0, 3:4, :]
    h2_ref[...] = h2
    logits = jnp.dot(h2, wr_ref[...], preferred_element_type=F32,
                     precision=lax.Precision.HIGHEST) + br_ref[...]
    lane = lax.broadcasted_iota(I32, logits.shape, 1)
    vals, idxs = [], []
    for _ in range(TOP_K):
        m = jnp.max(logits, axis=-1, keepdims=True)
        idx = jnp.min(jnp.where(logits == m, lane, LANES), axis=-1, keepdims=True)
        vals.append(m)
        idxs.append(idx)
        logits = jnp.where(lane == idx, 2 * NEG, logits)
    ex = [jnp.exp(v - vals[0]) for v in vals]
    denom = functools.reduce(jnp.add, ex)
    tope = jnp.zeros(logits.shape, I32)
    topw = jnp.zeros(logits.shape, F32)
    for k in range(TOP_K):
        tope = jnp.where(lane == k, idxs[k], tope)
        topw = jnp.where(lane == k, ex[k] / denom, topw)
    tope_ref[...] = tope
    topw_ref[...] = topw


def _postmix(o_sb, o_ca, gate, x2, mod, g_post, g_ffn, wbs_b, wbc_b, wout_b, wr_pad, br_pad, seq):
    n, d = x2.shape
    w = o_sb.shape[1]
    tm = min(ROW_TILE, seq)
    per_b = seq // tm
    row = lambda i: (i, 0)
    fixed = lambda i: (0, 0)
    return pl.pallas_call(
        _postmix_kernel,
        grid=(n // tm,),
        in_specs=[pl.BlockSpec((tm, w), row), pl.BlockSpec((tm, w), row),
                  pl.BlockSpec((tm, 2 * d), row), pl.BlockSpec((tm, d), row),
                  pl.BlockSpec((1,) + mod.shape[1:], lambda i: (i // per_b, 0, 0)),
                  pl.BlockSpec((1, d), fixed), pl.BlockSpec((1, d), fixed),
                  pl.BlockSpec((w, d), fixed), pl.BlockSpec((w, d), fixed),
                  pl.BlockSpec((d, d), fixed), pl.BlockSpec((d, LANES), fixed),
                  pl.BlockSpec((1, LANES), fixed)],
        out_specs=[pl.BlockSpec((tm, d), row), pl.BlockSpec((tm, d), row),
                   pl.BlockSpec((tm, LANES), row), pl.BlockSpec((tm, LANES), row)],
        out_shape=[jax.ShapeDtypeStruct((n, d), F32), jax.ShapeDtypeStruct((n, d), F32),
                   jax.ShapeDtypeStruct((n, LANES), I32), jax.ShapeDtypeStruct((n, LANES), F32)],
        compiler_params=_cparams(("arbitrary",)),
        name="postmix_router",
    )(o_sb, o_ca, gate, x2, mod, g_post.reshape(1, d), g_ffn.reshape(1, d), wbs_b, wbc_b, wout_b,
      wr_pad, br_pad)


def _rank_kernel(e_ref, l_ref, rank_ref, count_ref, run_ref):
    @pl.when(pl.program_id(0) == 0)
    def _():
        run_ref[...] = jnp.zeros_like(run_ref)

    e = e_ref[...]
    lane = lax.broadcasted_iota(I32, (e.shape[0], LANES), 1)
    onehot = lane == e
    before = _dot(l_ref[...], onehot.astype(BF16)) + run_ref[0:1, :]
    rank_ref[...] = jnp.sum(jnp.where(onehot, before, 0.0), axis=-1, keepdims=True).astype(I32)
    run_ref[0:1, :] = run_ref[0:1, :] + jnp.sum(onehot.astype(F32), axis=0, keepdims=True)
    count_ref[...] = run_ref[...].astype(I32)


def _routing_ranks(flat_e):
    nk = flat_e.shape[0]
    t = min(512, nk)
    strict_lower = (jnp.arange(t)[:, None] > jnp.arange(t)[None, :]).astype(BF16)
    rank, count = pl.pallas_call(
        _rank_kernel,
        grid=(nk // t,),
        in_specs=[pl.BlockSpec((t, 1), lambda i: (i, 0)),
                  pl.BlockSpec((t, t), lambda i: (0, 0))],
        out_specs=[pl.BlockSpec((t, 1), lambda i: (i, 0)),
                   pl.BlockSpec((8, LANES), lambda i: (0, 0))],
        out_shape=[jax.ShapeDtypeStruct((nk, 1), I32), jax.ShapeDtypeStruct((8, LANES), I32)],
        scratch_shapes=[pltpu.VMEM((8, LANES), F32)],
        compiler_params=_cparams(("arbitrary",)),
        name="routing_ranks",
    )(flat_e.reshape(nk, 1), strict_lower)
    return rank[:, 0], count[0, :N_EXPERTS]


def _moe_kernel(be_ref, cnt_ref, nvalid_ref, rowflat_ref, h2_hbm, w1_ref, b1_ref, w2_ref, b2_ref, y4_hbm,
                xbuf, ybuf, sem_in, sem_out, *, n_tokens):
    i = pl.program_id(0)
    blk = xbuf.shape[0]
    di = w2_ref.shape[1]

    @pl.when(i < nvalid_ref[0])
    def _():
        base = i * blk
        cnt = cnt_ref[i]

        def gather(r):
            tok = jnp.minimum(rowflat_ref[base + r] >> 2, n_tokens - 1)
            return pltpu.make_async_copy(h2_hbm.at[pl.ds(tok, 1), :], xbuf.at[pl.ds(r, 1), :], sem_in)

        def scatter(r):
            return pltpu.make_async_copy(ybuf.at[pl.ds(r, 1), :],
                                         y4_hbm.at[pl.ds(rowflat_ref[base + r], 1), :], sem_out)

        def start_gather(r, c):
            gather(r).start()
            return c

        def wait_gather(r, c):
            gather(r).wait()
            return c

        lax.fori_loop(0, blk, start_gather, 0)
        lax.fori_loop(0, blk, wait_gather, 0)

        xb = xbuf[...].astype(BF16)
        gate = _dot(xb, w1_ref[0, :, :di].astype(BF16)) + b1_ref[0, :, :di]
        up = _dot(xb, w1_ref[0, :, di:].astype(BF16)) + b1_ref[0, :, di:]
        gate = jnp.minimum(gate, SWIGLU_LIMIT)
        up = jnp.clip(up, -SWIGLU_LIMIT, SWIGLU_LIMIT)
        hid = gate * jax.nn.sigmoid(SWIGLU_ALPHA * gate) * (up + 1.0)
        ybuf[...] = _dot(hid.astype(BF16), w2_ref[0].astype(BF16)) + b2_ref[0]

        def start_scatter(r, c):
            scatter(r).start()
            return c

        def wait_scatter(r, c):
            scatter(r).wait()
            return c

        lax.fori_loop(0, cnt, start_scatter, 0)
        lax.fori_loop(0, cnt, wait_scatter, 0)


def _expert_ffn(h2, row_flat, block_e, block_cnt, n_valid, w1, b1, w2, b2):
    n, d = h2.shape
    e, _, two_i = w1.shape
    n_blocks = block_e.shape[0]
    grid_spec = pltpu.PrefetchScalarGridSpec(
        num_scalar_prefetch=4,
        grid=(n_blocks,),
        in_specs=[pl.BlockSpec(memory_space=pl.ANY),
                  pl.BlockSpec((1, d, two_i), lambda i, be, cnt, nv, rf: (be[i], 0, 0)),
                  pl.BlockSpec((1, 1, two_i), lambda i, be, cnt, nv, rf: (be[i], 0, 0)),
                  pl.BlockSpec((1, two_i // 2, d), lambda i, be, cnt, nv, rf: (be[i], 0, 0)),
                  pl.BlockSpec((1, 1, d), lambda i, be, cnt, nv, rf: (be[i], 0, 0))],
        out_specs=pl.BlockSpec(memory_space=pl.ANY),
        scratch_shapes=[pltpu.VMEM((MOE_BLOCK, d), F32), pltpu.VMEM((MOE_BLOCK, d), F32),
                        pltpu.SemaphoreType.DMA(()), pltpu.SemaphoreType.DMA(())],
    )
    return pl.pallas_call(
        functools.partial(_moe_kernel, n_tokens=n),
        grid_spec=grid_spec,
        out_shape=jax.ShapeDtypeStruct((n * TOP_K, d), F32),
        compiler_params=_cparams(("arbitrary",)),
        name="expert_ffn",
    )(block_e, block_cnt, n_valid, row_flat, h2, w1, b1.reshape(e, 1, two_i), w2, b2.reshape(e, 1, d))


def _combine_kernel(y4_ref, w_ref, x1_ref, mod_ref, g_ref, o_ref):
    d = x1_ref.shape[1]
    ff = functools.reduce(
        jnp.add, [y4_ref[:, k * d:(k + 1) * d] * w_ref[:, k:k + 1] for k in range(TOP_K)])
    o_ref[...] = x1_ref[...] + mod_ref[0, 5:6, :] * _rms(ff, g_ref[...])


def _combine(y4, top_w, x1, mod, g_post, seq):
    n, d = x1.shape
    tm = min(ROW_TILE, seq)
    per_b = seq // tm
    return pl.pallas_call(
        _combine_kernel,
        grid=(n // tm,),
        in_specs=[pl.BlockSpec((tm, TOP_K * d), lambda i: (i, 0)),
                  pl.BlockSpec((tm, LANES), lambda i: (i, 0)),
                  pl.BlockSpec((tm, d), lambda i: (i, 0)),
                  pl.BlockSpec((1,) + mod.shape[1:], lambda i: (i // per_b, 0, 0)),
                  pl.BlockSpec((1, d), lambda i: (0, 0))],
        out_specs=pl.BlockSpec((tm, d), lambda i: (i, 0)),
        out_shape=jax.ShapeDtypeStruct((n, d), F32),
        compiler_params=_cparams(("arbitrary",)),
        name="combine",
    )(y4.reshape(n, TOP_K * d), top_w, x1, mod, g_post.reshape(1, d))


def _layer(x, c, w_ada, b_ada, g_pre_mix, g_post_mix, w_in, rel_bias, w_branch_sb, w_branch_ca, w_gate,
           b_gate, w_out, g_pre_ffn, g_post_ffn, w_router, b_router, w_e1, b_e1, w_e2, b_e2):
    batch, seq, d = x.shape
    n = batch * seq
    sb_w = w_branch_sb.shape[0]
    ca_w = w_branch_ca.shape[0]
    n_exp = w_router.shape[1]
    x2 = x.reshape(n, d)

    mod = _modulation(c, w_ada, b_ada)

    scale = HEAD_DIM ** -0.5
    col_scale = jnp.ones((w_in.shape[1],), F32)
    col_scale = col_scale.at[:sb_w].set(scale).at[3 * sb_w:3 * sb_w + ca_w].set(scale)
    w_in_b = (w_in * col_scale[None, :]).astype(BF16)
    proj, gate = _premix(x2, mod, g_pre_mix, w_in_b, w_gate.astype(BF16), b_gate, seq)

    sbl, cal = sb_w // LANES, ca_w // LANES
    o_sb = _sb_attention(proj, batch, seq, sb_w, 0, sbl, 2 * sbl)
    o_ca = _ca_attention(proj, _ca_table(rel_bias), batch, seq, ca_w, 3 * sbl, 3 * sbl + cal, 3 * sbl + 2 * cal)

    wr_pad = jnp.zeros((d, LANES), F32).at[:, :n_exp].set(w_router)
    br_pad = jnp.full((1, LANES), NEG, F32).at[0, :n_exp].set(b_router)
    x1, h2, top_e, top_w = _postmix(o_sb, o_ca, gate, x2, mod, g_post_mix, g_pre_ffn,
                                    w_branch_sb.astype(BF16), w_branch_ca.astype(BF16), w_out.astype(BF16),
                                    wr_pad, br_pad, seq)

    nk = n * TOP_K
    flat_e = top_e[:, :TOP_K].reshape(nk)
    rank, counts = _routing_ranks(flat_e)
    padded = (counts + MOE_BLOCK - 1) // MOE_BLOCK * MOE_BLOCK
    pad_end = jnp.cumsum(padded)
    pad_start = pad_end - padded
    dest = pad_start[flat_e] + rank
    n_blocks = nk // MOE_BLOCK + n_exp
    rows = n_blocks * MOE_BLOCK
    row_flat = jnp.full((rows,), nk, I32).at[dest].set(jnp.arange(nk, dtype=I32))
    blk_start = jnp.arange(n_blocks, dtype=I32) * MOE_BLOCK
    block_e = jnp.minimum(jnp.searchsorted(pad_end, blk_start, side='right'), n_exp - 1).astype(I32)
    block_cnt = jnp.clip(pad_start[block_e] + counts[block_e] - blk_start, 0, MOE_BLOCK).astype(I32)
    n_valid = (pad_end[-1] // MOE_BLOCK).astype(I32).reshape(1)

    y4 = _expert_ffn(h2, row_flat, block_e, block_cnt, n_valid, w_e1, b_e1, w_e2, b_e2)
    out = _combine(y4, top_w, x1, mod, g_post_ffn, seq)
    return out.reshape(batch, seq, d)


def kernel(x, c, w_ada, b_ada, g_pre_mix, g_post_mix, w_in, rel_bias, w_branch_sb, w_branch_ca, w_gate, b_gate,
           w_out, g_pre_ffn, g_post_ffn, w_router, b_router, w_e1, b_e1, w_e2, b_e2):
    for l in range(w_ada.shape[0]):
        x = _layer(x, c, w_ada[l], b_ada[l], g_pre_mix[l], g_post_mix[l], w_in[l], rel_bias[l],
                   w_branch_sb[l], w_branch_ca[l], w_gate[l], b_gate[l], w_out[l], g_pre_ffn[l],
                   g_post_ffn[l], w_router[l], b_router[l], w_e1[l], b_e1[l], w_e2[l], b_e2[l])
    return x
```

```python
import functools

import numpy as np
import jax
import jax.numpy as jnp
from jax import lax
from jax.experimental import pallas as pl
from jax.experimental.pallas import tpu as pltpu

F32 = jnp.float32
BF16 = jnp.bfloat16
I32 = jnp.int32

RMS_EPS = 1e-6
HEAD_DIM = 64
LANES = 128
SUBLANES = 8
CHUNK = 64
CA_LEFT_CHUNKS = 8
REL_CLIP = 256
TOP_K = 4
SWIGLU_ALPHA = 1.702
SWIGLU_LIMIT = 7.0
NEG = -1e30
LOG2_E = 1.4426950408889634

SB_TILE = 256
SB_ROW_SPLIT = 2
CA_TILE = 2 * CHUNK
CA_KEY_TILES = CA_LEFT_CHUNKS * CHUNK // CA_TILE + 1
CA_WINDOW = CA_KEY_TILES * CA_TILE
ROW_TILE = 512
MOE_BLOCK = 256
MOE_COLS = 256
VMEM_LIMIT = 56 * 1024 * 1024


def _cparams(sem):
    return pltpu.CompilerParams(dimension_semantics=sem, vmem_limit_bytes=VMEM_LIMIT)


def _dot(a, b):
    return jnp.dot(a, b, preferred_element_type=F32)


def _dot_nt(a, b):
    return lax.dot_general(a, b, (((1,), (1,)), ((), ())), preferred_element_type=F32)


def _rms(x, g):
    return x * lax.rsqrt(jnp.mean(x * x, axis=-1, keepdims=True) + RMS_EPS) * g


def _mod_kernel(c_ref, w_ref, b_ref, o_ref):
    o_ref[...] = jnp.dot(c_ref[...], w_ref[...], preferred_element_type=F32,
                         precision=lax.Precision.HIGHEST) + b_ref[...]


def _modulation(c, w_ada, b_ada):
    b, d = c.shape
    n_out = w_ada.shape[1]
    c_pad = jnp.zeros((SUBLANES, d), F32).at[:b].set(c)
    out = pl.pallas_call(
        _mod_kernel,
        grid=(n_out // d,),
        in_specs=[pl.BlockSpec((SUBLANES, d), lambda j: (0, 0)),
                  pl.BlockSpec((d, d), lambda j: (0, j)),
                  pl.BlockSpec((1, d), lambda j: (0, j))],
        out_specs=pl.BlockSpec((SUBLANES, d), lambda j: (0, j)),
        out_shape=jax.ShapeDtypeStruct((SUBLANES, n_out), F32),
        compiler_params=_cparams(("arbitrary",)),
        name="adaln_mod",
    )(c_pad, w_ada, b_ada.reshape(1, n_out))
    return out[:b].reshape(b, n_out // d, d)


def _premix_kernel(x_ref, mod_ref, g_ref, win_ref, wg_ref, bg_ref, proj_ref, gate_ref, *, col):
    h = _rms(x_ref[...], g_ref[...]) * (1.0 + mod_ref[0, 1:2, :]) + mod_ref[0, 0:1, :]
    hb = h.astype(BF16)
    for c0 in range(0, proj_ref.shape[1], col):
        proj_ref[:, c0:c0 + col] = _dot(hb, win_ref[:, c0:c0 + col]).astype(BF16)
    for c0 in range(0, gate_ref.shape[1], col):
        g = jax.nn.sigmoid(_dot(hb, wg_ref[:, c0:c0 + col]) + bg_ref[:, c0:c0 + col])
        gate_ref[:, c0:c0 + col] = g.astype(BF16)


def _premix(x2, mod, g_pre, w_in_b, w_gate_b, b_gate, seq):
    n, d = x2.shape
    pw, gw = w_in_b.shape[1], w_gate_b.shape[1]
    tm = min(ROW_TILE, seq)
    per_b = seq // tm
    return pl.pallas_call(
        functools.partial(_premix_kernel, col=512),
        grid=(n // tm,),
        in_specs=[pl.BlockSpec((tm, d), lambda i: (i, 0)),
                  pl.BlockSpec((1,) + mod.shape[1:], lambda i: (i // per_b, 0, 0)),
                  pl.BlockSpec((1, d), lambda i: (0, 0)),
                  pl.BlockSpec((d, pw), lambda i: (0, 0)),
                  pl.BlockSpec((d, gw), lambda i: (0, 0)),
                  pl.BlockSpec((1, gw), lambda i: (0, 0))],
        out_specs=[pl.BlockSpec((tm, pw), lambda i: (i, 0)),
                   pl.BlockSpec((tm, gw), lambda i: (i, 0))],
        out_shape=[jax.ShapeDtypeStruct((n, pw), BF16), jax.ShapeDtypeStruct((n, gw), BF16)],
        compiler_params=_cparams(("arbitrary",)),
        name="premix",
    )(x2, mod, g_pre.reshape(1, d), w_in_b, w_gate_b, b_gate.reshape(1, gw))


def _sb_kernel(q_ref, k_ref, v_ref, u_ref, o_ref):
    i = pl.program_id(2)
    t = q_ref.shape[0]
    rows = t // SB_ROW_SPLIT
    q2 = q_ref[...]
    lane = lax.broadcasted_iota(I32, (t, LANES), 1)
    u = u_ref[...]
    chains = [(hh, r0) for hh in range(2) for r0 in range(0, t, rows)]
    qs = []
    for hh, r0 in chains:
        in_head = (lane < HEAD_DIM) if hh == 0 else (lane >= HEAD_DIM)
        qs.append(jnp.where(in_head, q2, jnp.zeros_like(q2))[r0:r0 + rows])

    def visit(j, state, diagonal):
        start = pl.multiple_of(j * t, t)
        kt = k_ref[pl.ds(start, t), :]
        vt = v_ref[pl.ds(start, t), :]
        zs = [_dot_nt(qh, kt) for qh in qs]
        sps = [jnp.maximum(z, 0.0) + jnp.log(1.0 + jnp.exp2(-jnp.abs(z))) * LOG2_E for z in zs]
        if diagonal:
            causal = [lax.broadcasted_iota(I32, (rows, t), 1) < lax.broadcasted_iota(I32, (rows, t), 0) + r0
                      for _, r0 in chains]
            sps = [jnp.where(c, sp, 0.0) for c, sp in zip(causal, sps)]
        his = [sp.astype(BF16) for sp in sps]
        los = [(sp - hi.astype(F32)).astype(BF16) for sp, hi in zip(sps, his)]
        cums = [_dot(hi, u) + _dot(lo, u) for hi, lo in zip(his, los)]
        ws = [jnp.exp2(z - cum - carry) for z, cum, (carry, _) in zip(zs, cums, state)]
        if diagonal:
            ws = [jnp.where(c, w, 0.0) for c, w in zip(causal, ws)]
        return tuple((carry + cum[:, 0:1], acc + _dot(w.astype(BF16), vt))
                     for cum, w, (carry, acc) in zip(cums, ws, state))

    state = tuple((jnp.zeros((rows, 1), F32), jnp.zeros((rows, LANES), F32)) for _ in chains)
    state = visit(i, state, True)
    state = lax.fori_loop(0, i, lambda n, st: visit(i - 1 - n, st, False), state)
    accs = [jnp.concatenate([acc for (h2, _), (_, acc) in zip(chains, state) if h2 == hh], axis=0)
            for hh in range(2)]
    o_ref[...] = jnp.where(lane < HEAD_DIM, accs[0], accs[1]).astype(o_ref.dtype)


def _sb_attention(proj, batch, seq, width, q_col, k_col, v_col):
    n = proj.shape[0]
    t = min(SB_TILE, seq)
    nq = seq // t
    pairs = width // LANES
    tri = (jnp.arange(t)[:, None] >= jnp.arange(t)[None, :]).astype(BF16)
    return pl.pallas_call(
        _sb_kernel,
        grid=(batch, pairs, nq),
        in_specs=[pl.BlockSpec((t, LANES), lambda b, p, i: (b * nq + i, q_col + p)),
                  pl.BlockSpec((seq, LANES), lambda b, p, i: (b, k_col + p)),
                  pl.BlockSpec((seq, LANES), lambda b, p, i: (b, v_col + p)),
                  pl.BlockSpec((t, t), lambda b, p, i: (0, 0))],
        out_specs=pl.BlockSpec((t, LANES), lambda b, p, i: (b * nq + i, p)),
        out_shape=jax.ShapeDtypeStruct((n, width), BF16),
        compiler_params=_cparams(("arbitrary", "arbitrary", "arbitrary")),
        name="stickbreak_attn",
    )(proj, proj, proj, tri)


def _ca_kernel(q_ref, k_ref, v_ref, t_ref, o_ref):
    r = pl.program_id(2)
    t = q_ref.shape[0]
    q2 = q_ref[...]
    lane = lax.broadcasted_iota(I32, (t, LANES), 1)
    outs = []
    for hh in range(2):
        in_head = (lane < HEAD_DIM) if hh == 0 else (lane >= HEAD_DIM)
        qh = jnp.where(in_head, q2, jnp.zeros_like(q2))
        zs, vts = [], []
        for tb in range(CA_KEY_TILES):
            kt_idx = r - (CA_KEY_TILES - 1) + tb
            start = pl.multiple_of(jnp.maximum(kt_idx, 0) * t, t)
            z = _dot_nt(qh, k_ref[pl.ds(start, t), :]) + t_ref[hh, :, tb * t:(tb + 1) * t]
            zs.append(jnp.where(kt_idx >= 0, z, NEG))
            vts.append(v_ref[pl.ds(start, t), :])
        m = functools.reduce(jnp.maximum, [jnp.max(z, axis=-1, keepdims=True) for z in zs])
        ps = [jnp.exp(z - m) for z in zs]
        denom = functools.reduce(jnp.add, [jnp.sum(p, axis=-1, keepdims=True) for p in ps])
        acc = functools.reduce(jnp.add, [_dot(p.astype(BF16), vt) for p, vt in zip(ps, vts)])
        outs.append(acc / denom)
    o_ref[...] = jnp.where(lane < HEAD_DIM, outs[0], outs[1]).astype(o_ref.dtype)


def _ca_table(rel_bias):
    h = rel_bias.shape[0]
    left = CA_LEFT_CHUNKS * CHUNK
    period = CA_WINDOW + CA_TILE
    u = np.arange(period)
    rel = np.where(u < CA_WINDOW, left - u, left + (period - u))
    v = rel_bias[:, np.clip(rel, -REL_CLIP, REL_CLIP) + REL_CLIP].astype(F32)
    flat = jnp.tile(v, (1, CA_TILE))[:, :CA_TILE * (period - 1)]
    bias = flat.reshape(h, CA_TILE, period - 1)[:, :, :CA_WINDOW]
    i = np.arange(CA_TILE)[:, None]
    j = np.arange(CA_WINDOW)[None, :]
    first = (i // CHUNK) * CHUNK
    visible = (j >= first) & (j < first + (CA_LEFT_CHUNKS + 1) * CHUNK)
    return jnp.where(visible[None], bias, NEG)


def _ca_attention(proj, table, batch, seq, width, q_col, k_col, v_col):
    n = proj.shape[0]
    t = CA_TILE
    nq = seq // t
    pairs = width // LANES
    return pl.pallas_call(
        _ca_kernel,
        grid=(batch, pairs, nq),
        in_specs=[pl.BlockSpec((t, LANES), lambda b, p, i: (b * nq + i, q_col + p)),
                  pl.BlockSpec((seq, LANES), lambda b, p, i: (b, k_col + p)),
                  pl.BlockSpec((seq, LANES), lambda b, p, i: (b, v_col + p)),
                  pl.BlockSpec((2, t, CA_WINDOW), lambda b, p, i: (p, 0, 0))],
        out_specs=pl.BlockSpec((t, LANES), lambda b, p, i: (b * nq + i, p)),
        out_shape=jax.ShapeDtypeStruct((n, width), BF16),
        compiler_params=_cparams(("arbitrary", "arbitrary", "arbitrary")),
        name="chunkband_attn",
    )(proj, proj, proj, table)


def _postmix_kernel(osb_ref, oca_ref, gate_ref, x_ref, mod_ref, gpost_ref, gffn_ref, wbs_ref, wbc_ref,
                    wout_ref, wr_ref, br_ref, x1_ref, h2_ref, tope_ref, topw_ref, cnt_ref):
    d = x_ref.shape[1]
    y_sb = _dot(osb_ref[...], wbs_ref[...])
    y_ca = _dot(oca_ref[...], wbc_ref[...])
    mixed_in = gate_ref[:, :d].astype(F32) * y_sb + gate_ref[:, d:].astype(F32) * y_ca
    mixed = _dot(mixed_in.astype(BF16), wout_ref[...])
    x1 = x_ref[...] + mod_ref[0, 2:3, :] * _rms(mixed, gpost_ref[...])
    x1_ref[...] = x1
    h2 = _rms(x1, gffn_ref[...]) * (1.0 + mod_ref[0, 4:5, :]) + mod_ref[0, 3:4, :]
    h2_ref[...] = h2
    logits = jnp.dot(h2, wr_ref[...], preferred_element_type=F32,
                     precision=lax.Precision.HIGHEST) + br_ref[...]
    lane = lax.broadcasted_iota(I32, logits.shape, 1)
    vals, idxs = [], []
    for _ in range(TOP_K):
        m = jnp.max(logits, axis=-1, keepdims=True)
        idx = jnp.min(jnp.where(logits == m, lane, LANES), axis=-1, keepdims=True)
        vals.append(m)
        idxs.append(idx)
        logits = jnp.where(lane == idx, 2 * NEG, logits)
    ex = [jnp.exp(v - vals[0]) for v in vals]
    denom = functools.reduce(jnp.add, ex)
    tope = jnp.zeros(logits.shape, I32)
    topw = jnp.zeros(logits.shape, F32)
    for k in range(TOP_K):
        tope = jnp.where(lane == k, idxs[k], tope)
        topw = jnp.where(lane == k, ex[k] / denom, topw)
    tope_ref[...] = tope
    topw_ref[...] = topw

    @pl.when(pl.program_id(0) == 0)
    def _():
        cnt_ref[...] = jnp.zeros_like(cnt_ref)

    chosen = functools.reduce(jnp.logical_or, [lane == idx for idx in idxs])
    cnt_ref[0:1, :] += jnp.sum(chosen.astype(F32), axis=0, keepdims=True)


def _postmix(o_sb, o_ca, gate, x2, mod, g_post, g_ffn, wbs_b, wbc_b, wout_b, wr_pad, br_pad, seq):
    n, d = x2.shape
    w = o_sb.shape[1]
    tm = min(ROW_TILE, seq)
    per_b = seq // tm
    row = lambda i: (i, 0)
    fixed = lambda i: (0, 0)
    return pl.pallas_call(
        _postmix_kernel,
        grid=(n // tm,),
        in_specs=[pl.BlockSpec((tm, w), row), pl.BlockSpec((tm, w), row),
                  pl.BlockSpec((tm, 2 * d), row), pl.BlockSpec((tm, d), row),
                  pl.BlockSpec((1,) + mod.shape[1:], lambda i: (i // per_b, 0, 0)),
                  pl.BlockSpec((1, d), fixed), pl.BlockSpec((1, d), fixed),
                  pl.BlockSpec((w, d), fixed), pl.BlockSpec((w, d), fixed),
                  pl.BlockSpec((d, d), fixed), pl.BlockSpec((d, LANES), fixed),
                  pl.BlockSpec((1, LANES), fixed)],
        out_specs=[pl.BlockSpec((tm, d), row), pl.BlockSpec((tm, d), row),
                   pl.BlockSpec((tm, LANES), row), pl.BlockSpec((tm, LANES), row),
                   pl.BlockSpec((SUBLANES, LANES), fixed)],
        out_shape=[jax.ShapeDtypeStruct((n, d), F32), jax.ShapeDtypeStruct((n, d), F32),
                   jax.ShapeDtypeStruct((n, LANES), I32), jax.ShapeDtypeStruct((n, LANES), F32),
                   jax.ShapeDtypeStruct((SUBLANES, LANES), F32)],
        compiler_params=_cparams(("arbitrary",)),
        name="postmix_router",
    )(o_sb, o_ca, gate, x2, mod, g_post.reshape(1, d), g_ffn.reshape(1, d), wbs_b, wbc_b, wout_b,
      wr_pad, br_pad)


def _rank_kernel(e_ref, l_ref, start_ref, dest_ref, run_ref):
    @pl.when(pl.program_id(0) == 0)
    def _():
        run_ref[...] = start_ref[...]

    e = e_ref[...]
    lane = lax.broadcasted_iota(I32, e.shape, 1)
    hots = [lane == e[:, k:k + 1] for k in range(TOP_K)]
    chosen = functools.reduce(jnp.logical_or, hots)
    before = _dot(l_ref[...], chosen.astype(BF16)) + run_ref[0:1, :]
    dest = jnp.zeros(e.shape, I32)
    for k in range(TOP_K):
        dk = jnp.sum(jnp.where(hots[k], before, 0.0), axis=-1, keepdims=True).astype(I32)
        dest = jnp.where(lane == k, dk, dest)
    dest_ref[...] = dest
    run_ref[0:1, :] += jnp.sum(chosen.astype(F32), axis=0, keepdims=True)


def _routing_ranks(top_e, group_start):
    n = top_e.shape[0]
    t = min(ROW_TILE, n)
    strict_lower = (jnp.arange(t)[:, None] > jnp.arange(t)[None, :]).astype(BF16)
    return pl.pallas_call(
        _rank_kernel,
        grid=(n // t,),
        in_specs=[pl.BlockSpec((t, LANES), lambda i: (i, 0)),
                  pl.BlockSpec((t, t), lambda i: (0, 0)),
                  pl.BlockSpec((SUBLANES, LANES), lambda i: (0, 0))],
        out_specs=pl.BlockSpec((t, LANES), lambda i: (i, 0)),
        out_shape=jax.ShapeDtypeStruct((n, LANES), I32),
        scratch_shapes=[pltpu.VMEM((SUBLANES, LANES), F32)],
        compiler_params=_cparams(("arbitrary",)),
        name="routing_ranks",
    )(top_e, strict_lower, group_start)


def _moe_kernel(be_ref, nvalid_ref, rowflat_ref, h2_hbm, w1_ref, b1_ref, w2_ref, b2_ref, y4_hbm,
                xbuf, ybuf, w1b, w2b, hid, sem_in, sem_out, *, n_tokens):
    i = pl.program_id(0)
    n_blocks = pl.num_programs(0)
    nv = nvalid_ref[0]
    blk, d = xbuf.shape[1], xbuf.shape[2]
    di = w2b.shape[0]

    def gather_copy(flat, r, slot):
        tok = jnp.minimum(flat >> 2, n_tokens - 1)
        return pltpu.make_async_copy(h2_hbm.at[pl.ds(tok, 1), :], xbuf.at[slot, pl.ds(r, 1), :],
                                     sem_in.at[slot])

    def scatter_copy(flat, r, slot):
        col = pl.multiple_of((flat & (TOP_K - 1)) * d, d)
        return pltpu.make_async_copy(ybuf.at[slot, pl.ds(r, 1), :],
                                     y4_hbm.at[pl.ds(flat >> 2, 1), pl.ds(col, d)], sem_out.at[slot])

    def wait_gather(slot):
        pltpu.make_async_copy(h2_hbm.at[pl.ds(0, blk), :], xbuf.at[slot], sem_in.at[slot]).wait()

    def wait_scatter(slot):
        pltpu.make_async_copy(ybuf.at[slot], y4_hbm.at[pl.ds(0, blk), pl.ds(0, d)], sem_out.at[slot]).wait()

    def step(slot):
        other = 1 - slot

        @pl.when(i == 0)
        def _():
            ybuf[...] = jnp.zeros_like(ybuf)
            spare = [pltpu.make_async_copy(ybuf.at[s], y4_hbm.at[pl.ds(n_tokens + s * blk, blk), pl.ds(k * d, d)],
                                           sem_out.at[s])
                     for s in range(2) for k in range(TOP_K)]
            for cp in spare:
                cp.start()
            for cp in spare:
                cp.wait()

            def first(r, c):
                gather_copy(rowflat_ref[r], r, slot).start()
                return c
            lax.fori_loop(0, blk, first, 0)

        @pl.when(i >= 1)
        def _():
            wait_scatter(slot)

        wait_gather(slot)

        @pl.when(jnp.logical_or(i == 0, be_ref[i] != be_ref[jnp.maximum(i - 1, 0)]))
        def _():
            w1b[...] = w1_ref[0].astype(BF16)
            w2b[...] = w2_ref[0].astype(BF16)

        nxt = jnp.minimum(i + 1, n_blocks - 1) * blk
        prev = jnp.where(i == 0, n_blocks - 1, i - 1) * blk
        n_hid, n_out = di // MOE_COLS, d // MOE_COLS
        per_stage = blk // (n_hid + n_out)

        def issue(stage):
            last = blk if stage == n_hid + n_out - 1 else (stage + 1) * per_stage
            for r in range(stage * per_stage, last):
                gather_copy(rowflat_ref[nxt + r], r, other).start()
                scatter_copy(rowflat_ref[prev + r], r, other).start()

        xb = xbuf[slot].astype(BF16)
        for c in range(n_hid):
            issue(c)
            c0 = c * MOE_COLS
            gate = _dot(xb, w1b[:, c0:c0 + MOE_COLS]) + b1_ref[0, :, c0:c0 + MOE_COLS]
            up = _dot(xb, w1b[:, di + c0:di + c0 + MOE_COLS]) + b1_ref[0, :, di + c0:di + c0 + MOE_COLS]
            gate = jnp.minimum(gate, SWIGLU_LIMIT)
            up = jnp.clip(up, -SWIGLU_LIMIT, SWIGLU_LIMIT)
            hid[:, c0:c0 + MOE_COLS] = (gate * jax.nn.sigmoid(SWIGLU_ALPHA * gate) * (up + 1.0)).astype(BF16)
        hb = hid[...]
        for c in range(n_out):
            issue(n_hid + c)
            c0 = c * MOE_COLS
            ybuf[slot, :, c0:c0 + MOE_COLS] = _dot(hb, w2b[:, c0:c0 + MOE_COLS]) + b2_ref[0, :, c0:c0 + MOE_COLS]

        @pl.when(i == nv - 1)
        def _():
            def last(r, c):
                scatter_copy(rowflat_ref[i * blk + r], r, slot).start()
                return c
            lax.fori_loop(0, blk, last, 0)
            wait_scatter(other)
            wait_scatter(slot)
            wait_gather(other)

    @pl.when(jnp.logical_and(i < nv, i % 2 == 0))
    def _():
        step(0)

    @pl.when(jnp.logical_and(i < nv, i % 2 == 1))
    def _():
        step(1)


def _expert_ffn(h2, row_flat, block_e, n_valid, w1, b1, w2, b2):
    n, d = h2.shape
    e, _, two_i = w1.shape
    n_blocks = block_e.shape[0]
    by_expert = lambda i, be, nv, rf: (be[i], 0, 0)
    grid_spec = pltpu.PrefetchScalarGridSpec(
        num_scalar_prefetch=3,
        grid=(n_blocks,),
        in_specs=[pl.BlockSpec(memory_space=pl.ANY),
                  pl.BlockSpec((1, d, two_i), by_expert),
                  pl.BlockSpec((1, 1, two_i), by_expert),
                  pl.BlockSpec((1, two_i // 2, d), by_expert),
                  pl.BlockSpec((1, 1, d), by_expert)],
        out_specs=pl.BlockSpec(memory_space=pl.ANY),
        scratch_shapes=[pltpu.VMEM((2, MOE_BLOCK, d), F32), pltpu.VMEM((2, MOE_BLOCK, d), F32),
                        pltpu.VMEM((d, two_i), BF16), pltpu.VMEM((two_i // 2, d), BF16),
                        pltpu.VMEM((MOE_BLOCK, two_i // 2), BF16),
                        pltpu.SemaphoreType.DMA((2,)), pltpu.SemaphoreType.DMA((2,))],
    )
    return pl.pallas_call(
        functools.partial(_moe_kernel, n_tokens=n),
        grid_spec=grid_spec,
        out_shape=jax.ShapeDtypeStruct((n + 2 * MOE_BLOCK, TOP_K * d), F32),
        compiler_params=_cparams(("arbitrary",)),
        name="expert_ffn",
    )(block_e, n_valid, row_flat, h2, w1, b1.reshape(e, 1, two_i), w2, b2.reshape(e, 1, d))


def _combine_kernel(y4_ref, w_ref, x1_ref, mod_ref, g_ref, o_ref):
    d = x1_ref.shape[1]
    ff = functools.reduce(
        jnp.add, [y4_ref[:, k * d:(k + 1) * d] * w_ref[:, k:k + 1] for k in range(TOP_K)])
    o_ref[...] = x1_ref[...] + mod_ref[0, 5:6, :] * _rms(ff, g_ref[...])


def _combine(y4, top_w, x1, mod, g_post, seq):
    n, d = x1.shape
    tm = min(ROW_TILE, seq)
    per_b = seq // tm
    return pl.pallas_call(
        _combine_kernel,
        grid=(n // tm,),
        in_specs=[pl.BlockSpec((tm, TOP_K * d), lambda i: (i, 0)),
                  pl.BlockSpec((tm, LANES), lambda i: (i, 0)),
                  pl.BlockSpec((tm, d), lambda i: (i, 0)),
                  pl.BlockSpec((1,) + mod.shape[1:], lambda i: (i // per_b, 0, 0)),
                  pl.BlockSpec((1, d), lambda i: (0, 0))],
        out_specs=pl.BlockSpec((tm, d), lambda i: (i, 0)),
        out_shape=jax.ShapeDtypeStruct((n, d), F32),
        compiler_params=_cparams(("arbitrary",)),
        name="combine",
    )(y4, top_w, x1, mod, g_post.reshape(1, d))


def _layer(x, c, w_ada, b_ada, g_pre_mix, g_post_mix, w_in, rel_bias, w_branch_sb, w_branch_ca, w_gate,
           b_gate, w_out, g_pre_ffn, g_post_ffn, w_router, b_router, w_e1, b_e1, w_e2, b_e2):
    batch, seq, d = x.shape
    n = batch * seq
    sb_w = w_branch_sb.shape[0]
    ca_w = w_branch_ca.shape[0]
    n_exp = w_router.shape[1]
    x2 = x.reshape(n, d)

    mod = _modulation(c, w_ada, b_ada)

    scale = HEAD_DIM ** -0.5
    col_scale = np.ones((w_in.shape[1],), np.float32)
    col_scale[:sb_w] = scale * LOG2_E
    col_scale[3 * sb_w:3 * sb_w + ca_w] = scale
    w_in_b = (w_in * col_scale[None, :]).astype(BF16)
    proj, gate = _premix(x2, mod, g_pre_mix, w_in_b, w_gate.astype(BF16), b_gate, seq)

    sbl, cal = sb_w // LANES, ca_w // LANES
    o_sb = _sb_attention(proj, batch, seq, sb_w, 0, sbl, 2 * sbl)
    o_ca = _ca_attention(proj, _ca_table(rel_bias), batch, seq, ca_w, 3 * sbl, 3 * sbl + cal, 3 * sbl + 2 * cal)

    wr_pad = jnp.zeros((d, LANES), F32).at[:, :n_exp].set(w_router)
    br_pad = jnp.full((1, LANES), NEG, F32).at[0, :n_exp].set(b_router)
    x1, h2, top_e, top_w, cnt = _postmix(o_sb, o_ca, gate, x2, mod, g_post_mix, g_pre_ffn,
                                         w_branch_sb.astype(BF16), w_branch_ca.astype(BF16),
                                         w_out.astype(BF16), wr_pad, br_pad, seq)

    nk = n * TOP_K
    counts = cnt[0, :n_exp].astype(I32)
    padded = (counts + MOE_BLOCK - 1) // MOE_BLOCK * MOE_BLOCK
    pad_end = jnp.cumsum(padded)
    pad_start = pad_end - padded
    group_start = jnp.zeros((SUBLANES, LANES), F32).at[0, :n_exp].set(pad_start.astype(F32))
    dest = _routing_ranks(top_e, group_start)[:, :TOP_K].reshape(nk)
    n_blocks = nk // MOE_BLOCK + n_exp
    rows = n_blocks * MOE_BLOCK
    spare = (n + np.arange(rows, dtype=np.int32) % (2 * MOE_BLOCK)) * TOP_K
    row_flat = jnp.asarray(spare).at[dest].set(jnp.arange(nk, dtype=I32))
    blk_start = jnp.arange(n_blocks, dtype=I32) * MOE_BLOCK
    block_e = jnp.minimum(jnp.sum((pad_end[None, :] <= blk_start[:, None]).astype(I32), axis=1), n_exp - 1)
    n_valid = (pad_end[-1] // MOE_BLOCK).astype(I32).reshape(1)

    y4 = _expert_ffn(h2, row_flat, block_e, n_valid, w_e1, b_e1, w_e2, b_e2)
    out = _combine(y4, top_w, x1, mod, g_post_ffn, seq)
    return out.reshape(batch, seq, d)


def kernel(x, c, w_ada, b_ada, g_pre_mix, g_post_mix, w_in, rel_bias, w_branch_sb, w_branch_ca, w_gate, b_gate,
           w_out, g_pre_ffn, g_post_ffn, w_router, b_router, w_e1, b_e1, w_e2, b_e2):
    for l in range(w_ada.shape[0]):
        x = _layer(x, c, w_ada[l], b_ada[l], g_pre_mix[l], g_post_mix[l], w_in[l], rel_bias[l],
                   w_branch_sb[l], w_branch_ca[l], w_gate[l], b_gate[l], w_out[l], g_pre_ffn[l],
                   g_post_ffn[l], w_router[l], b_router[l], w_e1[l], b_e1[l], w_e2[l], b_e2[l])
    return x
```

```python
import functools

import numpy as np
import jax
import jax.numpy as jnp
from jax import lax
from jax.experimental import pallas as pl
from jax.experimental.pallas import tpu as pltpu

F32 = jnp.float32
BF16 = jnp.bfloat16
I32 = jnp.int32

RMS_EPS = 1e-6
HEAD_DIM = 64
LANES = 128
SUBLANES = 8
CHUNK = 64
CA_LEFT_CHUNKS = 8
REL_CLIP = 256
TOP_K = 4
SWIGLU_ALPHA = 1.702
SWIGLU_LIMIT = 7.0
NEG = -1e30
LOG2_E = 1.4426950408889634

SB_TILE = 512
SB_KEY_TILE = 256
SB_ROWS = 128
CA_TILE = 2 * CHUNK
CA_KEY_TILES = CA_LEFT_CHUNKS * CHUNK // CA_TILE + 1
CA_WINDOW = CA_KEY_TILES * CA_TILE
ROW_TILE = 512
MOE_BLOCK = 256
MOE_COLS = 256
VMEM_LIMIT = 56 * 1024 * 1024


def _cparams(sem):
    return pltpu.CompilerParams(dimension_semantics=sem, vmem_limit_bytes=VMEM_LIMIT)


def _dot(a, b):
    return jnp.dot(a, b, preferred_element_type=F32)


def _dot_nt(a, b):
    return lax.dot_general(a, b, (((1,), (1,)), ((), ())), preferred_element_type=F32)


def _rms(x, g):
    return x * lax.rsqrt(jnp.mean(x * x, axis=-1, keepdims=True) + RMS_EPS) * g


def _mod_kernel(c_ref, w_ref, b_ref, o_ref):
    o_ref[...] = jnp.dot(c_ref[...], w_ref[...], preferred_element_type=F32,
                         precision=lax.Precision.HIGHEST) + b_ref[...]


def _modulation(c, w_ada, b_ada):
    b, d = c.shape
    n_out = w_ada.shape[1]
    c_pad = jnp.zeros((SUBLANES, d), F32).at[:b].set(c)
    out = pl.pallas_call(
        _mod_kernel,
        grid=(n_out // d,),
        in_specs=[pl.BlockSpec((SUBLANES, d), lambda j: (0, 0)),
                  pl.BlockSpec((d, d), lambda j: (0, j)),
                  pl.BlockSpec((1, d), lambda j: (0, j))],
        out_specs=pl.BlockSpec((SUBLANES, d), lambda j: (0, j)),
        out_shape=jax.ShapeDtypeStruct((SUBLANES, n_out), F32),
        compiler_params=_cparams(("arbitrary",)),
        name="adaln_mod",
    )(c_pad, w_ada, b_ada.reshape(1, n_out))
    return out[:b].reshape(b, n_out // d, d)


def _premix_kernel(x_ref, mod_ref, g_ref, win_ref, wg_ref, bg_ref, proj_ref, gate_ref, *, col):
    h = _rms(x_ref[...], g_ref[...]) * (1.0 + mod_ref[0, 1:2, :]) + mod_ref[0, 0:1, :]
    hb = h.astype(BF16)
    for c0 in range(0, proj_ref.shape[1], col):
        proj_ref[:, c0:c0 + col] = _dot(hb, win_ref[:, c0:c0 + col]).astype(BF16)
    for c0 in range(0, gate_ref.shape[1], col):
        g = jax.nn.sigmoid(_dot(hb, wg_ref[:, c0:c0 + col]) + bg_ref[:, c0:c0 + col])
        gate_ref[:, c0:c0 + col] = g.astype(BF16)


def _premix(x2, mod, g_pre, w_in_b, w_gate_b, b_gate, seq):
    n, d = x2.shape
    pw, gw = w_in_b.shape[1], w_gate_b.shape[1]
    tm = min(ROW_TILE, seq)
    per_b = seq // tm
    return pl.pallas_call(
        functools.partial(_premix_kernel, col=512),
        grid=(n // tm,),
        in_specs=[pl.BlockSpec((tm, d), lambda i: (i, 0)),
                  pl.BlockSpec((1,) + mod.shape[1:], lambda i: (i // per_b, 0, 0)),
                  pl.BlockSpec((1, d), lambda i: (0, 0)),
                  pl.BlockSpec((d, pw), lambda i: (0, 0)),
                  pl.BlockSpec((d, gw), lambda i: (0, 0)),
                  pl.BlockSpec((1, gw), lambda i: (0, 0))],
        out_specs=[pl.BlockSpec((tm, pw), lambda i: (i, 0)),
                   pl.BlockSpec((tm, gw), lambda i: (i, 0))],
        out_shape=[jax.ShapeDtypeStruct((n, pw), BF16), jax.ShapeDtypeStruct((n, gw), BF16)],
        compiler_params=_cparams(("arbitrary",)),
        name="premix",
    )(x2, mod, g_pre.reshape(1, d), w_in_b, w_gate_b, b_gate.reshape(1, gw))


def _sb_kernel(q_ref, k_ref, v_ref, u_ref, o_ref, z_scr, cum_scr, carry_scr, acc_scr):
    i = pl.program_id(2)
    t = q_ref.shape[0]
    tk = u_ref.shape[1]
    per_q = t // tk
    rows = SB_ROWS
    q2 = q_ref[...]
    lane = lax.broadcasted_iota(I32, (t, LANES), 1)
    u2 = u_ref[...]
    chains = [(hh, r0) for hh in range(2) for r0 in range(0, t, rows)]
    qs = []
    for hh, r0 in chains:
        in_head = (lane < HEAD_DIM) if hh == 0 else (lane >= HEAD_DIM)
        qs.append(jnp.where(in_head, q2, jnp.zeros_like(q2))[r0:r0 + rows])

    def causal(j_local):
        return [lax.broadcasted_iota(I32, (rows, tk), 1) + j_local * tk
                < lax.broadcasted_iota(I32, (rows, tk), 0) + r0 for _, r0 in chains]

    n_ch = len(chains)

    def logits(j, slot):
        start = pl.multiple_of(j * tk, tk)
        kt = k_ref[pl.ds(start, tk), :]
        for c, qh in enumerate(qs):
            z_scr[slot, c] = _dot_nt(qh, kt)

    def sums(slot, mask):
        for c in range(n_ch):
            z = z_scr[slot, c].astype(BF16)
            sp = jnp.maximum(z, 0) + jnp.log(1 + jnp.exp(-jnp.abs(z)))
            if mask is not None:
                sp = jnp.where(mask[c], sp, jnp.zeros_like(sp))
            cum_scr[slot, c] = _dot(sp, u2)

    def finish(j, slot, mask):
        start = pl.multiple_of(j * tk, tk)
        vt = v_ref[pl.ds(start, tk), :]
        for c in range(n_ch):
            cum = cum_scr[slot, c]
            carry = carry_scr[c]
            w = jnp.exp(z_scr[slot, c] - cum - jnp.concatenate([carry] * (tk // LANES), axis=1))
            if mask is not None:
                w = jnp.where(mask[c], w, 0.0)
            acc_scr[c] += _dot(w.astype(BF16), vt)
            carry_scr[c] = carry + jnp.broadcast_to(cum[:, 0:1], carry.shape)

    def sweep_tile(j, slot, mask, next_mask):
        logits(jnp.maximum(j - 1, 0), 1 - slot)
        finish(j, slot, mask)
        sums(1 - slot, next_mask)

    carry_scr[...] = jnp.zeros_like(carry_scr)
    acc_scr[...] = jnp.zeros_like(acc_scr)
    first = i * per_q
    logits(first + per_q - 1, (per_q - 1) % 2)
    sums((per_q - 1) % 2, causal(per_q - 1))
    for jl in range(per_q - 1, -1, -1):
        sweep_tile(first + jl, jl % 2, causal(jl), causal(jl - 1) if jl > 0 else None)

    def body(n, carry):
        j = first - 1 - 2 * n
        sweep_tile(j, 1, None, None)
        sweep_tile(j - 1, 0, None, None)
        return carry

    lax.fori_loop(0, first // 2, body, 0)
    accs = [jnp.concatenate([acc_scr[c] for c, (h2, _) in enumerate(chains) if h2 == hh], axis=0)
            for hh in range(2)]
    o_ref[...] = jnp.where(lane < HEAD_DIM, accs[0], accs[1]).astype(o_ref.dtype)


def _sb_attention(proj, batch, seq, width, q_col, k_col, v_col):
    n = proj.shape[0]
    t = min(SB_TILE, seq)
    tk = min(SB_KEY_TILE, t)
    assert (t // tk) % 2 == 0, "the sweep alternates two staging slots per key tile"
    nq = seq // t
    pairs = width // LANES
    n_chains = 2 * (t // SB_ROWS)
    tri = jnp.asarray((np.arange(tk)[:, None] >= np.arange(tk)[None, :]).astype(np.float32)).astype(BF16)
    return pl.pallas_call(
        _sb_kernel,
        grid=(batch, pairs, nq),
        in_specs=[pl.BlockSpec((t, LANES), lambda b, p, i: (b * nq + i, q_col + p)),
                  pl.BlockSpec((seq, LANES), lambda b, p, i: (b, k_col + p)),
                  pl.BlockSpec((seq, LANES), lambda b, p, i: (b, v_col + p)),
                  pl.BlockSpec((tk, tk), lambda b, p, i: (0, 0))],
        out_specs=pl.BlockSpec((t, LANES), lambda b, p, i: (b * nq + i, p)),
        out_shape=jax.ShapeDtypeStruct((n, width), BF16),
        scratch_shapes=[pltpu.VMEM((2, n_chains, SB_ROWS, tk), F32), pltpu.VMEM((2, n_chains, SB_ROWS, tk), F32),
                        pltpu.VMEM((n_chains, SB_ROWS, LANES), F32), pltpu.VMEM((n_chains, SB_ROWS, LANES), F32)],
        compiler_params=_cparams(("arbitrary", "arbitrary", "arbitrary")),
        name="stickbreak_attn",
    )(proj, proj, proj, tri)


def _ca_kernel(q_ref, k_ref, v_ref, t_ref, o_ref):
    r = pl.program_id(2)
    t = q_ref.shape[0]
    window = t_ref.shape[-1]
    start = pl.multiple_of(jnp.maximum(r - (CA_KEY_TILES - 1), 0) * t, t)
    kw = k_ref[pl.ds(start, window), :]
    vw = v_ref[pl.ds(start, window), :]
    q2 = q_ref[...]
    lane = lax.broadcasted_iota(I32, (t, LANES), 1)
    chains = [(hh, r0) for hh in range(2) for r0 in range(0, t, CHUNK)]
    zs = []
    for hh, r0 in chains:
        in_head = (lane < HEAD_DIM) if hh == 0 else (lane >= HEAD_DIM)
        qh = jnp.where(in_head, q2, jnp.zeros_like(q2))[r0:r0 + CHUNK]
        zs.append(_dot_nt(qh, kw) + t_ref[0, hh, r0:r0 + CHUNK, :])
    ps = [jnp.exp2(z - jnp.max(z, axis=-1, keepdims=True)) for z in zs]
    outs = [_dot(p.astype(BF16), vw) / jnp.sum(p, axis=-1, keepdims=True) for p in ps]
    heads = [jnp.concatenate([o for (h2, _), o in zip(chains, outs) if h2 == hh], axis=0) for hh in range(2)]
    o_ref[...] = jnp.where(lane < HEAD_DIM, heads[0], heads[1]).astype(o_ref.dtype)


def _ca_tables(rel_bias):
    h = rel_bias.shape[0]
    left = CA_LEFT_CHUNKS * CHUNK
    period = CA_WINDOW + CA_TILE
    u = np.arange(period)
    rel = np.where(u < CA_WINDOW, left - u, left + (period - u))
    v = rel_bias[:, np.clip(rel, -REL_CLIP, REL_CLIP) + REL_CLIP].astype(F32) * LOG2_E
    flat = jnp.tile(v, (1, CA_TILE))[:, :CA_TILE * (period - 1)]
    bias = flat.reshape(h, CA_TILE, period - 1)[:, :, :CA_WINDOW]
    i = np.arange(CA_TILE)[:, None]
    j = np.arange(CA_WINDOW)[None, :]
    first = (i // CHUNK) * CHUNK
    visible = (j >= first) & (j < first + (CA_LEFT_CHUNKS + 1) * CHUNK)
    full = jnp.where(visible[None], bias, NEG)
    tables = []
    for r in range(CA_KEY_TILES):
        shift = (CA_KEY_TILES - 1 - r) * CA_TILE
        tables.append(jnp.pad(full[:, :, shift:], ((0, 0), (0, 0), (0, shift)), constant_values=NEG))
    return jnp.stack(tables)


def _ca_attention(proj, tables, batch, seq, width, q_col, k_col, v_col):
    n = proj.shape[0]
    t = CA_TILE
    nq = seq // t
    pairs = width // LANES
    assert seq >= CA_WINDOW
    return pl.pallas_call(
        _ca_kernel,
        grid=(batch, pairs, nq),
        in_specs=[pl.BlockSpec((t, LANES), lambda b, p, i: (b * nq + i, q_col + p)),
                  pl.BlockSpec((seq, LANES), lambda b, p, i: (b, k_col + p)),
                  pl.BlockSpec((seq, LANES), lambda b, p, i: (b, v_col + p)),
                  pl.BlockSpec((1, 2, t, CA_WINDOW),
                               lambda b, p, i: (jnp.minimum(i, CA_KEY_TILES - 1), p, 0, 0))],
        out_specs=pl.BlockSpec((t, LANES), lambda b, p, i: (b * nq + i, p)),
        out_shape=jax.ShapeDtypeStruct((n, width), BF16),
        compiler_params=_cparams(("arbitrary", "arbitrary", "arbitrary")),
        name="chunkband_attn",
    )(proj, proj, proj, tables)


def _postmix_kernel(osb_ref, oca_ref, gate_ref, x_ref, mod_ref, gpost_ref, gffn_ref, wbs_ref, wbc_ref,
                    wout_ref, wr_ref, br_ref, x1_ref, h2_ref, tope_ref, topw_ref, cnt_ref):
    d = x_ref.shape[1]
    y_sb = _dot(osb_ref[...], wbs_ref[...])
    y_ca = _dot(oca_ref[...], wbc_ref[...])
    mixed_in = gate_ref[:, :d].astype(F32) * y_sb + gate_ref[:, d:].astype(F32) * y_ca
    mixed = _dot(mixed_in.astype(BF16), wout_ref[...])
    x1 = x_ref[...] + mod_ref[0, 2:3, :] * _rms(mixed, gpost_ref[...])
    x1_ref[...] = x1
    h2 = _rms(x1, gffn_ref[...]) * (1.0 + mod_ref[0, 4:5, :]) + mod_ref[0, 3:4, :]
    h2_ref[...] = h2
    logits = jnp.dot(h2, wr_ref[...], preferred_element_type=F32,
                     precision=lax.Precision.HIGHEST) + br_ref[...]
    lane = lax.broadcasted_iota(I32, logits.shape, 1)
    vals, idxs = [], []
    for _ in range(TOP_K):
        m = jnp.max(logits, axis=-1, keepdims=True)
        idx = jnp.min(jnp.where(logits == m, lane, LANES), axis=-1, keepdims=True)
        vals.append(m)
        idxs.append(idx)
        logits = jnp.where(lane == idx, 2 * NEG, logits)
    ex = [jnp.exp(v - vals[0]) for v in vals]
    denom = functools.reduce(jnp.add, ex)
    tope = jnp.zeros(logits.shape, I32)
    topw = jnp.zeros(logits.shape, F32)
    for k in range(TOP_K):
        tope = jnp.where(lane == k, idxs[k], tope)
        topw = jnp.where(lane == k, ex[k] / denom, topw)
    tope_ref[...] = tope
    topw_ref[...] = topw

    @pl.when(pl.program_id(0) == 0)
    def _():
        cnt_ref[...] = jnp.zeros_like(cnt_ref)

    chosen = functools.reduce(jnp.logical_or, [lane == idx for idx in idxs])
    cnt_ref[0:1, :] += jnp.sum(chosen.astype(F32), axis=0, keepdims=True)


def _postmix(o_sb, o_ca, gate, x2, mod, g_post, g_ffn, wbs_b, wbc_b, wout_b, wr_pad, br_pad, seq):
    n, d = x2.shape
    w = o_sb.shape[1]
    tm = min(ROW_TILE, seq)
    per_b = seq // tm
    row = lambda i: (i, 0)
    fixed = lambda i: (0, 0)
    return pl.pallas_call(
        _postmix_kernel,
        grid=(n // tm,),
        in_specs=[pl.BlockSpec((tm, w), row), pl.BlockSpec((tm, w), row),
                  pl.BlockSpec((tm, 2 * d), row), pl.BlockSpec((tm, d), row),
                  pl.BlockSpec((1,) + mod.shape[1:], lambda i: (i // per_b, 0, 0)),
                  pl.BlockSpec((1, d), fixed), pl.BlockSpec((1, d), fixed),
                  pl.BlockSpec((w, d), fixed), pl.BlockSpec((w, d), fixed),
                  pl.BlockSpec((d, d), fixed), pl.BlockSpec((d, LANES), fixed),
                  pl.BlockSpec((1, LANES), fixed)],
        out_specs=[pl.BlockSpec((tm, d), row), pl.BlockSpec((tm, d), row),
                   pl.BlockSpec((tm, LANES), row), pl.BlockSpec((tm, LANES), row),
                   pl.BlockSpec((SUBLANES, LANES), fixed)],
        out_shape=[jax.ShapeDtypeStruct((n, d), F32), jax.ShapeDtypeStruct((n, d), F32),
                   jax.ShapeDtypeStruct((n, LANES), I32), jax.ShapeDtypeStruct((n, LANES), F32),
                   jax.ShapeDtypeStruct((SUBLANES, LANES), F32)],
        compiler_params=_cparams(("arbitrary",)),
        name="postmix_router",
    )(o_sb, o_ca, gate, x2, mod, g_post.reshape(1, d), g_ffn.reshape(1, d), wbs_b, wbc_b, wout_b,
      wr_pad, br_pad)


def _rank_kernel(e_ref, l_ref, start_ref, dest_ref, run_ref):
    @pl.when(pl.program_id(0) == 0)
    def _():
        run_ref[...] = start_ref[...]

    e = e_ref[...]
    lane = lax.broadcasted_iota(I32, e.shape, 1)
    hots = [lane == e[:, k:k + 1] for k in range(TOP_K)]
    chosen = functools.reduce(jnp.logical_or, hots)
    before = _dot(l_ref[...], chosen.astype(BF16)) + run_ref[0:1, :]
    dest = jnp.zeros(e.shape, I32)
    for k in range(TOP_K):
        dk = jnp.sum(jnp.where(hots[k], before, 0.0), axis=-1, keepdims=True).astype(I32)
        dest = jnp.where(lane == k, dk, dest)
    dest_ref[...] = dest
    run_ref[0:1, :] += jnp.sum(chosen.astype(F32), axis=0, keepdims=True)


def _routing_ranks(top_e, group_start):
    n = top_e.shape[0]
    t = min(ROW_TILE, n)
    strict_lower = (jnp.arange(t)[:, None] > jnp.arange(t)[None, :]).astype(BF16)
    return pl.pallas_call(
        _rank_kernel,
        grid=(n // t,),
        in_specs=[pl.BlockSpec((t, LANES), lambda i: (i, 0)),
                  pl.BlockSpec((t, t), lambda i: (0, 0)),
                  pl.BlockSpec((SUBLANES, LANES), lambda i: (0, 0))],
        out_specs=pl.BlockSpec((t, LANES), lambda i: (i, 0)),
        out_shape=jax.ShapeDtypeStruct((n, LANES), I32),
        scratch_shapes=[pltpu.VMEM((SUBLANES, LANES), F32)],
        compiler_params=_cparams(("arbitrary",)),
        name="routing_ranks",
    )(top_e, strict_lower, group_start)


def _moe_kernel(be_ref, nvalid_ref, rowflat_ref, h2_hbm, w1_ref, b1_ref, w2_ref, b2_ref, y4_hbm,
                xbuf, ybuf, w1b, w2b, hid, sem_in, sem_out, *, n_tokens):
    i = pl.program_id(0)
    n_blocks = pl.num_programs(0)
    nv = nvalid_ref[0]
    blk, d = xbuf.shape[1], xbuf.shape[2]
    di = w2b.shape[0]

    def gather_copy(flat, r, slot):
        tok = jnp.minimum(flat >> 2, n_tokens - 1)
        return pltpu.make_async_copy(h2_hbm.at[pl.ds(tok, 1), :], xbuf.at[slot, pl.ds(r, 1), :],
                                     sem_in.at[slot])

    def scatter_copy(flat, r, slot):
        col = pl.multiple_of((flat & (TOP_K - 1)) * d, d)
        return pltpu.make_async_copy(ybuf.at[slot, pl.ds(r, 1), :],
                                     y4_hbm.at[pl.ds(flat >> 2, 1), pl.ds(col, d)], sem_out.at[slot])

    def wait_gather(slot):
        pltpu.make_async_copy(h2_hbm.at[pl.ds(0, blk), :], xbuf.at[slot], sem_in.at[slot]).wait()

    def wait_scatter(slot):
        pltpu.make_async_copy(ybuf.at[slot], y4_hbm.at[pl.ds(0, blk), pl.ds(0, d)], sem_out.at[slot]).wait()

    def step(slot):
        other = 1 - slot

        @pl.when(i == 0)
        def _():
            ybuf[...] = jnp.zeros_like(ybuf)
            spare = [pltpu.make_async_copy(ybuf.at[s], y4_hbm.at[pl.ds(n_tokens + s * blk, blk), pl.ds(k * d, d)],
                                           sem_out.at[s])
                     for s in range(2) for k in range(TOP_K)]
            for cp in spare:
                cp.start()
            for cp in spare:
                cp.wait()

            def first(r, c):
                gather_copy(rowflat_ref[r], r, slot).start()
                return c
            lax.fori_loop(0, blk, first, 0)

        @pl.when(i >= 1)
        def _():
            wait_scatter(slot)

        wait_gather(slot)

        @pl.when(jnp.logical_or(i == 0, be_ref[i] != be_ref[jnp.maximum(i - 1, 0)]))
        def _():
            w1b[...] = w1_ref[0].astype(BF16)
            w2b[...] = w2_ref[0].astype(BF16)

        nxt = jnp.minimum(i + 1, n_blocks - 1) * blk
        prev = jnp.where(i == 0, n_blocks - 1, i - 1) * blk
        n_hid, n_out = di // MOE_COLS, d // MOE_COLS
        per_stage = blk // (n_hid + n_out)

        def issue(stage):
            last = blk if stage == n_hid + n_out - 1 else (stage + 1) * per_stage
            for r in range(stage * per_stage, last):
                gather_copy(rowflat_ref[nxt + r], r, other).start()
                scatter_copy(rowflat_ref[prev + r], r, other).start()

        xb = xbuf[slot].astype(BF16)
        for c in range(n_hid):
            issue(c)
            c0 = c * MOE_COLS
            gate = _dot(xb, w1b[:, c0:c0 + MOE_COLS]) + b1_ref[0, :, c0:c0 + MOE_COLS]
            up = _dot(xb, w1b[:, di + c0:di + c0 + MOE_COLS]) + b1_ref[0, :, di + c0:di + c0 + MOE_COLS]
            gate = jnp.minimum(gate, SWIGLU_LIMIT)
            up = jnp.clip(up, -SWIGLU_LIMIT, SWIGLU_LIMIT)
            hid[:, c0:c0 + MOE_COLS] = (gate * jax.nn.sigmoid(SWIGLU_ALPHA * gate) * (up + 1.0)).astype(BF16)
        hb = hid[...]
        for c in range(n_out):
            issue(n_hid + c)
            c0 = c * MOE_COLS
            ybuf[slot, :, c0:c0 + MOE_COLS] = _dot(hb, w2b[:, c0:c0 + MOE_COLS]) + b2_ref[0, :, c0:c0 + MOE_COLS]

        @pl.when(i == nv - 1)
        def _():
            def last(r, c):
                scatter_copy(rowflat_ref[i * blk + r], r, slot).start()
                return c
            lax.fori_loop(0, blk, last, 0)
            wait_scatter(other)
            wait_scatter(slot)
            wait_gather(other)

    @pl.when(jnp.logical_and(i < nv, i % 2 == 0))
    def _():
        step(0)

    @pl.when(jnp.logical_and(i < nv, i % 2 == 1))
    def _():
        step(1)


def _expert_ffn(h2, row_flat, block_e, n_valid, w1, b1, w2, b2):
    n, d = h2.shape
    e, _, two_i = w1.shape
    n_blocks = block_e.shape[0]
    by_expert = lambda i, be, nv, rf: (be[i], 0, 0)
    grid_spec = pltpu.PrefetchScalarGridSpec(
        num_scalar_prefetch=3,
        grid=(n_blocks,),
        in_specs=[pl.BlockSpec(memory_space=pl.ANY),
                  pl.BlockSpec((1, d, two_i), by_expert),
                  pl.BlockSpec((1, 1, two_i), by_expert),
                  pl.BlockSpec((1, two_i // 2, d), by_expert),
                  pl.BlockSpec((1, 1, d), by_expert)],
        out_specs=pl.BlockSpec(memory_space=pl.ANY),
        scratch_shapes=[pltpu.VMEM((2, MOE_BLOCK, d), F32), pltpu.VMEM((2, MOE_BLOCK, d), F32),
                        pltpu.VMEM((d, two_i), BF16), pltpu.VMEM((two_i // 2, d), BF16),
                        pltpu.VMEM((MOE_BLOCK, two_i // 2), BF16),
                        pltpu.SemaphoreType.DMA((2,)), pltpu.SemaphoreType.DMA((2,))],
    )
    return pl.pallas_call(
        functools.partial(_moe_kernel, n_tokens=n),
        grid_spec=grid_spec,
        out_shape=jax.ShapeDtypeStruct((n + 2 * MOE_BLOCK, TOP_K * d), F32),
        compiler_params=_cparams(("arbitrary",)),
        name="expert_ffn",
    )(block_e, n_valid, row_flat, h2, w1, b1.reshape(e, 1, two_i), w2, b2.reshape(e, 1, d))


def _combine_kernel(y4_ref, w_ref, x1_ref, mod_ref, g_ref, o_ref):
    d = x1_ref.shape[1]
    ff = functools.reduce(
        jnp.add, [y4_ref[:, k * d:(k + 1) * d] * w_ref[:, k:k + 1] for k in range(TOP_K)])
    o_ref[...] = x1_ref[...] + mod_ref[0, 5:6, :] * _rms(ff, g_ref[...])


def _combine(y4, top_w, x1, mod, g_post, seq):
    n, d = x1.shape
    tm = min(ROW_TILE, seq)
    per_b = seq // tm
    return pl.pallas_call(
        _combine_kernel,
        grid=(n // tm,),
        in_specs=[pl.BlockSpec((tm, TOP_K * d), lambda i: (i, 0)),
                  pl.BlockSpec((tm, LANES), lambda i: (i, 0)),
                  pl.BlockSpec((tm, d), lambda i: (i, 0)),
                  pl.BlockSpec((1,) + mod.shape[1:], lambda i: (i // per_b, 0, 0)),
                  pl.BlockSpec((1, d), lambda i: (0, 0))],
        out_specs=pl.BlockSpec((tm, d), lambda i: (i, 0)),
        out_shape=jax.ShapeDtypeStruct((n, d), F32),
        compiler_params=_cparams(("arbitrary",)),
        name="combine",
    )(y4, top_w, x1, mod, g_post.reshape(1, d))


def _layer(x, c, w_ada, b_ada, g_pre_mix, g_post_mix, w_in, rel_bias, w_branch_sb, w_branch_ca, w_gate,
           b_gate, w_out, g_pre_ffn, g_post_ffn, w_router, b_router, w_e1, b_e1, w_e2, b_e2):
    batch, seq, d = x.shape
    n = batch * seq
    sb_w = w_branch_sb.shape[0]
    ca_w = w_branch_ca.shape[0]
    n_exp = w_router.shape[1]
    x2 = x.reshape(n, d)

    mod = _modulation(c, w_ada, b_ada)

    scale = HEAD_DIM ** -0.5
    col_scale = np.ones((w_in.shape[1],), np.float32)
    col_scale[:sb_w] = scale
    col_scale[3 * sb_w:3 * sb_w + ca_w] = scale * LOG2_E
    w_in_b = (w_in * col_scale[None, :]).astype(BF16)
    proj, gate = _premix(x2, mod, g_pre_mix, w_in_b, w_gate.astype(BF16), b_gate, seq)

    sbl, cal = sb_w // LANES, ca_w // LANES
    o_sb = _sb_attention(proj, batch, seq, sb_w, 0, sbl, 2 * sbl)
    o_ca = _ca_attention(proj, _ca_tables(rel_bias), batch, seq, ca_w, 3 * sbl, 3 * sbl + cal, 3 * sbl + 2 * cal)

    wr_pad = jnp.zeros((d, LANES), F32).at[:, :n_exp].set(w_router)
    br_pad = jnp.full((1, LANES), NEG, F32).at[0, :n_exp].set(b_router)
    x1, h2, top_e, top_w, cnt = _postmix(o_sb, o_ca, gate, x2, mod, g_post_mix, g_pre_ffn,
                                         w_branch_sb.astype(BF16), w_branch_ca.astype(BF16),
                                         w_out.astype(BF16), wr_pad, br_pad, seq)

    nk = n * TOP_K
    counts = cnt[0, :n_exp].astype(I32)
    padded = (counts + MOE_BLOCK - 1) // MOE_BLOCK * MOE_BLOCK
    pad_end = jnp.cumsum(padded)
    pad_start = pad_end - padded
    group_start = jnp.zeros((SUBLANES, LANES), F32).at[0, :n_exp].set(pad_start.astype(F32))
    dest = _routing_ranks(top_e, group_start)[:, :TOP_K].reshape(nk)
    n_blocks = nk // MOE_BLOCK + n_exp
    rows = n_blocks * MOE_BLOCK
    spare = (n + np.arange(rows, dtype=np.int32) % (2 * MOE_BLOCK)) * TOP_K
    row_flat = jnp.asarray(spare).at[dest].set(jnp.arange(nk, dtype=I32))
    blk_start = jnp.arange(n_blocks, dtype=I32) * MOE_BLOCK
    block_e = jnp.minimum(jnp.sum((pad_end[None, :] <= blk_start[:, None]).astype(I32), axis=1), n_exp - 1)
    n_valid = (pad_end[-1] // MOE_BLOCK).astype(I32).reshape(1)

    y4 = _expert_ffn(h2, row_flat, block_e, n_valid, w_e1, b_e1, w_e2, b_e2)
    out = _combine(y4, top_w, x1, mod, g_post_ffn, seq)
    return out.reshape(batch, seq, d)


def kernel(x, c, w_ada, b_ada, g_pre_mix, g_post_mix, w_in, rel_bias, w_branch_sb, w_branch_ca, w_gate, b_gate,
           w_out, g_pre_ffn, g_post_ffn, w_router, b_router, w_e1, b_e1, w_e2, b_e2):
    for l in range(w_ada.shape[0]):
        x = _layer(x, c, w_ada[l], b_ada[l], g_pre_mix[l], g_post_mix[l], w_in[l], rel_bias[l],
                   w_branch_sb[l], w_branch_ca[l], w_gate[l], b_gate[l], w_out[l], g_pre_ffn[l],
                   g_post_ffn[l], w_router[l], b_router[l], w_e1[l], b_e1[l], w_e2[l], b_e2[l])
    return x
```

```python
import functools

import numpy as np
import jax
import jax.numpy as jnp
from jax import lax
from jax.experimental import pallas as pl
from jax.experimental.pallas import tpu as pltpu

F32 = jnp.float32
BF16 = jnp.bfloat16
I32 = jnp.int32

RMS_EPS = 1e-6
HEAD_DIM = 64
LANES = 128
SUBLANES = 8
CHUNK = 64
CA_LEFT_CHUNKS = 8
REL_CLIP = 256
TOP_K = 4
SWIGLU_ALPHA = 1.702
SWIGLU_LIMIT = 7.0
NEG = -1e30
LOG2_E = 1.4426950408889634

SB_TILE = 512
SB_KEY_TILE = 256
SB_ROWS = 128
CA_TILE = 2 * CHUNK
CA_KEY_TILES = CA_LEFT_CHUNKS * CHUNK // CA_TILE + 1
CA_WINDOW = CA_KEY_TILES * CA_TILE
ROW_TILE = 512
MOE_BLOCK = 256
MOE_COLS = 256
VMEM_LIMIT = 56 * 1024 * 1024


def _cparams(sem):
    return pltpu.CompilerParams(dimension_semantics=sem, vmem_limit_bytes=VMEM_LIMIT)


def _dot(a, b):
    return jnp.dot(a, b, preferred_element_type=F32)


def _dot_nt(a, b):
    return lax.dot_general(a, b, (((1,), (1,)), ((), ())), preferred_element_type=F32)


def _rms(x, g):
    return x * lax.rsqrt(jnp.mean(x * x, axis=-1, keepdims=True) + RMS_EPS) * g


def _mod_kernel(c_ref, w_ref, b_ref, o_ref):
    o_ref[...] = jnp.dot(c_ref[...], w_ref[...], preferred_element_type=F32,
                         precision=lax.Precision.HIGHEST) + b_ref[...]


def _modulation(c, w_ada, b_ada):
    b, d = c.shape
    n_out = w_ada.shape[1]
    c_pad = jnp.zeros((SUBLANES, d), F32).at[:b].set(c)
    out = pl.pallas_call(
        _mod_kernel,
        grid=(n_out // d,),
        in_specs=[pl.BlockSpec((SUBLANES, d), lambda j: (0, 0)),
                  pl.BlockSpec((d, d), lambda j: (0, j)),
                  pl.BlockSpec((1, d), lambda j: (0, j))],
        out_specs=pl.BlockSpec((SUBLANES, d), lambda j: (0, j)),
        out_shape=jax.ShapeDtypeStruct((SUBLANES, n_out), F32),
        compiler_params=_cparams(("arbitrary",)),
        name="adaln_mod",
    )(c_pad, w_ada, b_ada.reshape(1, n_out))
    return out[:b].reshape(b, n_out // d, d)


def _premix_kernel(x_ref, mod_ref, g_ref, win_ref, wg_ref, bg_ref, proj_ref, gate_ref, *, col):
    h = _rms(x_ref[...], g_ref[...]) * (1.0 + mod_ref[0, 1:2, :]) + mod_ref[0, 0:1, :]
    hb = h.astype(BF16)
    for c0 in range(0, proj_ref.shape[1], col):
        proj_ref[:, c0:c0 + col] = _dot(hb, win_ref[:, c0:c0 + col]).astype(BF16)
    for c0 in range(0, gate_ref.shape[1], col):
        g = jax.nn.sigmoid(_dot(hb, wg_ref[:, c0:c0 + col]) + bg_ref[:, c0:c0 + col])
        gate_ref[:, c0:c0 + col] = g.astype(BF16)


def _premix(x2, mod, g_pre, w_in_b, w_gate_b, b_gate, seq):
    n, d = x2.shape
    pw, gw = w_in_b.shape[1], w_gate_b.shape[1]
    tm = min(ROW_TILE, seq)
    per_b = seq // tm
    return pl.pallas_call(
        functools.partial(_premix_kernel, col=512),
        grid=(n // tm,),
        in_specs=[pl.BlockSpec((tm, d), lambda i: (i, 0)),
                  pl.BlockSpec((1,) + mod.shape[1:], lambda i: (i // per_b, 0, 0)),
                  pl.BlockSpec((1, d), lambda i: (0, 0)),
                  pl.BlockSpec((d, pw), lambda i: (0, 0)),
                  pl.BlockSpec((d, gw), lambda i: (0, 0)),
                  pl.BlockSpec((1, gw), lambda i: (0, 0))],
        out_specs=[pl.BlockSpec((tm, pw), lambda i: (i, 0)),
                   pl.BlockSpec((tm, gw), lambda i: (i, 0))],
        out_shape=[jax.ShapeDtypeStruct((n, pw), BF16), jax.ShapeDtypeStruct((n, gw), BF16)],
        compiler_params=_cparams(("arbitrary",)),
        name="premix",
    )(x2, mod, g_pre.reshape(1, d), w_in_b, w_gate_b, b_gate.reshape(1, gw))


def _sb_kernel(q_ref, k_ref, v_ref, u_ref, o_ref, z_scr, cum_scr, carry_scr, acc_scr):
    i = pl.program_id(2)
    t = q_ref.shape[0]
    tk = u_ref.shape[1]
    per_q = t // tk
    rows = SB_ROWS
    q2 = q_ref[...]
    lane = lax.broadcasted_iota(I32, (t, LANES), 1)
    u2 = u_ref[...]
    chains = [(hh, r0) for hh in range(2) for r0 in range(0, t, rows)]
    qs = []
    for hh, r0 in chains:
        in_head = (lane < HEAD_DIM) if hh == 0 else (lane >= HEAD_DIM)
        qs.append(jnp.where(in_head, q2, jnp.zeros_like(q2))[r0:r0 + rows])

    def causal(j_local):
        return [lax.broadcasted_iota(I32, (rows, tk), 1) + j_local * tk
                < lax.broadcasted_iota(I32, (rows, tk), 0) + r0 for _, r0 in chains]

    n_ch = len(chains)

    def logits(j, slot):
        start = pl.multiple_of(j * tk, tk)
        kt = k_ref[pl.ds(start, tk), :]
        for c, qh in enumerate(qs):
            z_scr[slot, c] = _dot_nt(qh, kt)

    def sums(slot, mask):
        for c in range(n_ch):
            z = z_scr[slot, c].astype(BF16)
            sp = jnp.maximum(z, 0) + jnp.log(1 + jnp.exp(-jnp.abs(z)))
            if mask is not None:
                sp = jnp.where(mask[c], sp, jnp.zeros_like(sp))
            cum_scr[slot, c] = _dot(sp, u2)

    def finish(j, slot, mask):
        start = pl.multiple_of(j * tk, tk)
        vt = v_ref[pl.ds(start, tk), :]
        for c in range(n_ch):
            cum = cum_scr[slot, c]
            carry = carry_scr[c]
            w = jnp.exp(z_scr[slot, c] - cum - jnp.concatenate([carry] * (tk // LANES), axis=1))
            if mask is not None:
                w = jnp.where(mask[c], w, 0.0)
            acc_scr[c] += _dot(w.astype(BF16), vt)
            carry_scr[c] = carry + jnp.broadcast_to(cum[:, 0:1], carry.shape)

    def sweep_tile(j, slot, mask, next_mask):
        logits(jnp.maximum(j - 1, 0), 1 - slot)
        finish(j, slot, mask)
        sums(1 - slot, next_mask)

    carry_scr[...] = jnp.zeros_like(carry_scr)
    acc_scr[...] = jnp.zeros_like(acc_scr)
    first = i * per_q
    logits(first + per_q - 1, (per_q - 1) % 2)
    sums((per_q - 1) % 2, causal(per_q - 1))
    for jl in range(per_q - 1, -1, -1):
        sweep_tile(first + jl, jl % 2, causal(jl), causal(jl - 1) if jl > 0 else None)

    def body(n, carry):
        j = first - 1 - 2 * n
        sweep_tile(j, 1, None, None)
        sweep_tile(j - 1, 0, None, None)
        return carry

    lax.fori_loop(0, first // 2, body, 0)
    accs = [jnp.concatenate([acc_scr[c] for c, (h2, _) in enumerate(chains) if h2 == hh], axis=0)
            for hh in range(2)]
    o_ref[...] = jnp.where(lane < HEAD_DIM, accs[0], accs[1]).astype(o_ref.dtype)


def _sb_attention(proj, batch, seq, width, q_col, k_col, v_col):
    n = proj.shape[0]
    t = min(SB_TILE, seq)
    tk = min(SB_KEY_TILE, t)
    assert (t // tk) % 2 == 0, "the sweep alternates two staging slots per key tile"
    nq = seq // t
    pairs = width // LANES
    n_chains = 2 * (t // SB_ROWS)
    tri = jnp.asarray((np.arange(tk)[:, None] >= np.arange(tk)[None, :]).astype(np.float32)).astype(BF16)
    return pl.pallas_call(
        _sb_kernel,
        grid=(batch, pairs, nq),
        in_specs=[pl.BlockSpec((t, LANES), lambda b, p, i: (b * nq + i, q_col + p)),
                  pl.BlockSpec((seq, LANES), lambda b, p, i: (b, k_col + p)),
                  pl.BlockSpec((seq, LANES), lambda b, p, i: (b, v_col + p)),
                  pl.BlockSpec((tk, tk), lambda b, p, i: (0, 0))],
        out_specs=pl.BlockSpec((t, LANES), lambda b, p, i: (b * nq + i, p)),
        out_shape=jax.ShapeDtypeStruct((n, width), BF16),
        scratch_shapes=[pltpu.VMEM((2, n_chains, SB_ROWS, tk), F32), pltpu.VMEM((2, n_chains, SB_ROWS, tk), F32),
                        pltpu.VMEM((n_chains, SB_ROWS, LANES), F32), pltpu.VMEM((n_chains, SB_ROWS, LANES), F32)],
        compiler_params=_cparams(("arbitrary", "arbitrary", "arbitrary")),
        name="stickbreak_attn",
    )(proj, proj, proj, tri)


def _ca_kernel(q_ref, k_ref, v_ref, t_ref, o_ref):
    r = pl.program_id(2)
    t = q_ref.shape[0]
    window = t_ref.shape[-1]
    start = pl.multiple_of(jnp.maximum(r - (CA_KEY_TILES - 1), 0) * t, t)
    kw = k_ref[pl.ds(start, window), :]
    vw = v_ref[pl.ds(start, window), :]
    q2 = q_ref[...]
    lane = lax.broadcasted_iota(I32, (t, LANES), 1)
    chains = [(hh, r0) for hh in range(2) for r0 in range(0, t, CHUNK)]
    zs = []
    for hh, r0 in chains:
        in_head = (lane < HEAD_DIM) if hh == 0 else (lane >= HEAD_DIM)
        qh = jnp.where(in_head, q2, jnp.zeros_like(q2))[r0:r0 + CHUNK]
        zs.append(_dot_nt(qh, kw) + t_ref[0, hh, r0:r0 + CHUNK, :])
    ps = [jnp.exp2(z - jnp.max(z, axis=-1, keepdims=True)) for z in zs]
    outs = [_dot(p.astype(BF16), vw) / jnp.sum(p, axis=-1, keepdims=True) for p in ps]
    heads = [jnp.concatenate([o for (h2, _), o in zip(chains, outs) if h2 == hh], axis=0) for hh in range(2)]
    o_ref[...] = jnp.where(lane < HEAD_DIM, heads[0], heads[1]).astype(o_ref.dtype)


def _ca_tables(rel_bias):
    h = rel_bias.shape[0]
    left = CA_LEFT_CHUNKS * CHUNK
    period = CA_WINDOW + CA_TILE
    u = np.arange(period)
    rel = np.where(u < CA_WINDOW, left - u, left + (period - u))
    v = rel_bias[:, np.clip(rel, -REL_CLIP, REL_CLIP) + REL_CLIP].astype(F32) * LOG2_E
    flat = jnp.tile(v, (1, CA_TILE))[:, :CA_TILE * (period - 1)]
    bias = flat.reshape(h, CA_TILE, period - 1)[:, :, :CA_WINDOW]
    i = np.arange(CA_TILE)[:, None]
    j = np.arange(CA_WINDOW)[None, :]
    first = (i // CHUNK) * CHUNK
    visible = (j >= first) & (j < first + (CA_LEFT_CHUNKS + 1) * CHUNK)
    full = jnp.where(visible[None], bias, NEG)
    tables = []
    for r in range(CA_KEY_TILES):
        shift = (CA_KEY_TILES - 1 - r) * CA_TILE
        tables.append(jnp.pad(full[:, :, shift:], ((0, 0), (0, 0), (0, shift)), constant_values=NEG))
    return jnp.stack(tables)


def _ca_attention(proj, tables, batch, seq, width, q_col, k_col, v_col):
    n = proj.shape[0]
    t = CA_TILE
    nq = seq // t
    pairs = width // LANES
    assert seq >= CA_WINDOW
    return pl.pallas_call(
        _ca_kernel,
        grid=(batch, pairs, nq),
        in_specs=[pl.BlockSpec((t, LANES), lambda b, p, i: (b * nq + i, q_col + p)),
                  pl.BlockSpec((seq, LANES), lambda b, p, i: (b, k_col + p)),
                  pl.BlockSpec((seq, LANES), lambda b, p, i: (b, v_col + p)),
                  pl.BlockSpec((1, 2, t, CA_WINDOW),
                               lambda b, p, i: (jnp.minimum(i, CA_KEY_TILES - 1), p, 0, 0))],
        out_specs=pl.BlockSpec((t, LANES), lambda b, p, i: (b * nq + i, p)),
        out_shape=jax.ShapeDtypeStruct((n, width), BF16),
        compiler_params=_cparams(("arbitrary", "arbitrary", "arbitrary")),
        name="chunkband_attn",
    )(proj, proj, proj, tables)


def _postmix_kernel(osb_ref, oca_ref, gate_ref, x_ref, mod_ref, gpost_ref, gffn_ref, wbs_ref, wbc_ref,
                    wout_ref, wr_ref, br_ref, x1_ref, h2_ref, tope_ref, topw_ref, cnt_ref):
    d = x_ref.shape[1]
    y_sb = _dot(osb_ref[...], wbs_ref[...])
    y_ca = _dot(oca_ref[...], wbc_ref[...])
    mixed_in = gate_ref[:, :d].astype(F32) * y_sb + gate_ref[:, d:].astype(F32) * y_ca
    mixed = _dot(mixed_in.astype(BF16), wout_ref[...])
    x1 = x_ref[...] + mod_ref[0, 2:3, :] * _rms(mixed, gpost_ref[...])
    x1_ref[...] = x1
    h2 = _rms(x1, gffn_ref[...]) * (1.0 + mod_ref[0, 4:5, :]) + mod_ref[0, 3:4, :]
    n_sub = d // LANES
    for s in range(n_sub):
        h2_ref[pl.ds(s, h2.shape[0], stride=n_sub), :] = h2[:, s * LANES:(s + 1) * LANES]
    logits = jnp.dot(h2, wr_ref[...], preferred_element_type=F32,
                     precision=lax.Precision.HIGHEST) + br_ref[...]
    lane = lax.broadcasted_iota(I32, logits.shape, 1)
    vals, idxs = [], []
    for _ in range(TOP_K):
        m = jnp.max(logits, axis=-1, keepdims=True)
        idx = jnp.min(jnp.where(logits == m, lane, LANES), axis=-1, keepdims=True)
        vals.append(m)
        idxs.append(idx)
        logits = jnp.where(lane == idx, 2 * NEG, logits)
    ex = [jnp.exp(v - vals[0]) for v in vals]
    denom = functools.reduce(jnp.add, ex)
    tope = jnp.zeros(logits.shape, I32)
    topw = jnp.zeros(logits.shape, F32)
    for k in range(TOP_K):
        tope = jnp.where(lane == k, idxs[k], tope)
        topw = jnp.where(lane == k, ex[k] / denom, topw)
    tope_ref[...] = tope
    topw_ref[...] = topw

    @pl.when(pl.program_id(0) == 0)
    def _():
        cnt_ref[...] = jnp.zeros_like(cnt_ref)

    chosen = functools.reduce(jnp.logical_or, [lane == idx for idx in idxs])
    cnt_ref[0:1, :] += jnp.sum(chosen.astype(F32), axis=0, keepdims=True)


def _postmix(o_sb, o_ca, gate, x2, mod, g_post, g_ffn, wbs_b, wbc_b, wout_b, wr_pad, br_pad, seq):
    n, d = x2.shape
    w = o_sb.shape[1]
    tm = min(ROW_TILE, seq)
    per_b = seq // tm
    row = lambda i: (i, 0)
    fixed = lambda i: (0, 0)
    return pl.pallas_call(
        _postmix_kernel,
        grid=(n // tm,),
        in_specs=[pl.BlockSpec((tm, w), row), pl.BlockSpec((tm, w), row),
                  pl.BlockSpec((tm, 2 * d), row), pl.BlockSpec((tm, d), row),
                  pl.BlockSpec((1,) + mod.shape[1:], lambda i: (i // per_b, 0, 0)),
                  pl.BlockSpec((1, d), fixed), pl.BlockSpec((1, d), fixed),
                  pl.BlockSpec((w, d), fixed), pl.BlockSpec((w, d), fixed),
                  pl.BlockSpec((d, d), fixed), pl.BlockSpec((d, LANES), fixed),
                  pl.BlockSpec((1, LANES), fixed)],
        out_specs=[pl.BlockSpec((tm, d), row), pl.BlockSpec((tm * (d // LANES), LANES), row),
                   pl.BlockSpec((tm, LANES), row), pl.BlockSpec((tm, LANES), row),
                   pl.BlockSpec((SUBLANES, LANES), fixed)],
        out_shape=[jax.ShapeDtypeStruct((n, d), F32), jax.ShapeDtypeStruct((n * (d // LANES), LANES), F32),
                   jax.ShapeDtypeStruct((n, LANES), I32), jax.ShapeDtypeStruct((n, LANES), F32),
                   jax.ShapeDtypeStruct((SUBLANES, LANES), F32)],
        compiler_params=_cparams(("arbitrary",)),
        name="postmix_router",
    )(o_sb, o_ca, gate, x2, mod, g_post.reshape(1, d), g_ffn.reshape(1, d), wbs_b, wbc_b, wout_b,
      wr_pad, br_pad)


def _rank_kernel(e_ref, l_ref, start_ref, dest_ref, run_ref):
    @pl.when(pl.program_id(0) == 0)
    def _():
        run_ref[...] = start_ref[...]

    e = e_ref[...]
    lane = lax.broadcasted_iota(I32, e.shape, 1)
    hots = [lane == e[:, k:k + 1] for k in range(TOP_K)]
    chosen = functools.reduce(jnp.logical_or, hots)
    before = _dot(l_ref[...], chosen.astype(BF16)) + run_ref[0:1, :]
    dest = jnp.zeros(e.shape, I32)
    for k in range(TOP_K):
        dk = jnp.sum(jnp.where(hots[k], before, 0.0), axis=-1, keepdims=True).astype(I32)
        dest = jnp.where(lane == k, dk, dest)
    dest_ref[...] = dest
    run_ref[0:1, :] += jnp.sum(chosen.astype(F32), axis=0, keepdims=True)


def _routing_ranks(top_e, group_start):
    n = top_e.shape[0]
    t = min(ROW_TILE, n)
    strict_lower = (jnp.arange(t)[:, None] > jnp.arange(t)[None, :]).astype(BF16)
    return pl.pallas_call(
        _rank_kernel,
        grid=(n // t,),
        in_specs=[pl.BlockSpec((t, LANES), lambda i: (i, 0)),
                  pl.BlockSpec((t, t), lambda i: (0, 0)),
                  pl.BlockSpec((SUBLANES, LANES), lambda i: (0, 0))],
        out_specs=pl.BlockSpec((t, LANES), lambda i: (i, 0)),
        out_shape=jax.ShapeDtypeStruct((n, LANES), I32),
        scratch_shapes=[pltpu.VMEM((SUBLANES, LANES), F32)],
        compiler_params=_cparams(("arbitrary",)),
        name="routing_ranks",
    )(top_e, strict_lower, group_start)


def _moe_kernel(be_ref, nvalid_ref, rowflat_ref, h2_hbm, w1_ref, b1_ref, w2_ref, b2_ref, y4_hbm,
                xbuf, ybuf, w1b, w2b, hid, sem_in, sem_out, *, n_tokens):
    i = pl.program_id(0)
    n_blocks = pl.num_programs(0)
    nv = nvalid_ref[0]
    n_sub = h2_hbm.shape[1]
    blk = xbuf.shape[1] // n_sub
    d = n_sub * LANES
    di = w2b.shape[0]

    def tile(buf, slot, r):
        return buf.at[slot, pl.ds(r * n_sub, n_sub), :]

    def gather_copy(flat, r, slot):
        return pltpu.make_async_copy(h2_hbm.at[flat & (n_tokens - 1)], tile(xbuf, slot, r), sem_in.at[slot])

    def scatter_copy(flat, r, slot):
        return pltpu.make_async_copy(tile(ybuf, slot, r), y4_hbm.at[flat], sem_out.at[slot])

    def wait_gather(slot):
        pltpu.make_async_copy(xbuf.at[slot], xbuf.at[slot], sem_in.at[slot]).wait()

    def wait_scatter(slot):
        pltpu.make_async_copy(ybuf.at[slot], ybuf.at[slot], sem_out.at[slot]).wait()

    def step(slot):
        other = 1 - slot

        @pl.when(i == 0)
        def _():
            ybuf[...] = jnp.zeros_like(ybuf)

            def clear(r, c):
                for s in range(2):
                    scatter_copy(n_tokens * TOP_K + s * blk + r, r, s).start()
                return c
            lax.fori_loop(0, blk, clear, 0)
            wait_scatter(0)
            wait_scatter(1)

            def first(r, c):
                gather_copy(rowflat_ref[r], r, slot).start()
                return c
            lax.fori_loop(0, blk, first, 0)

        @pl.when(i >= 1)
        def _():
            wait_scatter(slot)

        wait_gather(slot)

        @pl.when(jnp.logical_or(i == 0, be_ref[i] != be_ref[jnp.maximum(i - 1, 0)]))
        def _():
            w1b[...] = w1_ref[0].astype(BF16)
            w2b[...] = w2_ref[0].astype(BF16)

        nxt = jnp.minimum(i + 1, n_blocks - 1) * blk
        prev = jnp.where(i == 0, n_blocks - 1, i - 1) * blk
        n_hid, n_out = di // MOE_COLS, d // MOE_COLS
        per_stage = blk // (n_hid + n_out)

        def issue(stage):
            last = blk if stage == n_hid + n_out - 1 else (stage + 1) * per_stage
            for r in range(stage * per_stage, last):
                gather_copy(rowflat_ref[nxt + r], r, other).start(priority=r % 2)
                scatter_copy(rowflat_ref[prev + r], r, other).start(priority=r % 2)

        xb = jnp.concatenate([xbuf[slot, pl.ds(s, blk, stride=n_sub), :] for s in range(n_sub)],
                             axis=1).astype(BF16)
        for c in range(n_hid):
            issue(c)
            c0 = c * MOE_COLS
            gate = _dot(xb, w1b[:, c0:c0 + MOE_COLS]) + b1_ref[0, :, c0:c0 + MOE_COLS]
            up = _dot(xb, w1b[:, di + c0:di + c0 + MOE_COLS]) + b1_ref[0, :, di + c0:di + c0 + MOE_COLS]
            gate = jnp.minimum(gate, SWIGLU_LIMIT)
            up = jnp.clip(up, -SWIGLU_LIMIT, SWIGLU_LIMIT)
            hid[:, c0:c0 + MOE_COLS] = (gate * jax.nn.sigmoid(SWIGLU_ALPHA * gate) * (up + 1.0)).astype(BF16)
        hb = hid[...]
        for c in range(n_out):
            issue(n_hid + c)
            c0 = c * MOE_COLS
            y = _dot(hb, w2b[:, c0:c0 + MOE_COLS]) + b2_ref[0, :, c0:c0 + MOE_COLS]
            for s in range(MOE_COLS // LANES):
                ybuf[slot, pl.ds(c0 // LANES + s, blk, stride=n_sub), :] = y[:, s * LANES:(s + 1) * LANES]

        @pl.when(i == nv - 1)
        def _():
            def last(r, c):
                scatter_copy(rowflat_ref[i * blk + r], r, slot).start()
                return c
            lax.fori_loop(0, blk, last, 0)
            wait_scatter(other)
            wait_scatter(slot)
            wait_gather(other)

    @pl.when(jnp.logical_and(i < nv, i % 2 == 0))
    def _():
        step(0)

    @pl.when(jnp.logical_and(i < nv, i % 2 == 1))
    def _():
        step(1)


def _expert_ffn(h2_tiles, row_flat, block_e, n_valid, w1, b1, w2, b2):
    e, d, two_i = w1.shape
    n_sub = d // LANES
    n = h2_tiles.shape[0] // n_sub
    h2 = h2_tiles.reshape(n, n_sub, LANES)
    n_blocks = block_e.shape[0]
    by_expert = lambda i, be, nv, rf: (be[i], 0, 0)
    grid_spec = pltpu.PrefetchScalarGridSpec(
        num_scalar_prefetch=3,
        grid=(n_blocks,),
        in_specs=[pl.BlockSpec(memory_space=pl.ANY),
                  pl.BlockSpec((1, d, two_i), by_expert),
                  pl.BlockSpec((1, 1, two_i), by_expert),
                  pl.BlockSpec((1, two_i // 2, d), by_expert),
                  pl.BlockSpec((1, 1, d), by_expert)],
        out_specs=pl.BlockSpec(memory_space=pl.ANY),
        scratch_shapes=[pltpu.VMEM((2, MOE_BLOCK * n_sub, LANES), F32), pltpu.VMEM((2, MOE_BLOCK * n_sub, LANES), F32),
                        pltpu.VMEM((d, two_i), BF16), pltpu.VMEM((two_i // 2, d), BF16),
                        pltpu.VMEM((MOE_BLOCK, two_i // 2), BF16),
                        pltpu.SemaphoreType.DMA((2,)), pltpu.SemaphoreType.DMA((2,))],
    )
    y4 = pl.pallas_call(
        functools.partial(_moe_kernel, n_tokens=n),
        grid_spec=grid_spec,
        out_shape=jax.ShapeDtypeStruct((n * TOP_K + 2 * MOE_BLOCK, n_sub, LANES), F32),
        compiler_params=_cparams(("arbitrary",)),
        name="expert_ffn",
    )(block_e, n_valid, row_flat, h2, w1, b1.reshape(e, 1, two_i), w2, b2.reshape(e, 1, d))
    return y4.reshape((n * TOP_K + 2 * MOE_BLOCK) * n_sub, LANES)


def _combine_kernel(*refs):
    y_refs, (w_ref, x1_ref, mod_ref, g_ref, o_ref) = refs[:TOP_K], refs[TOP_K:]
    tm, d = x1_ref.shape
    n_sub = d // LANES
    ff = jnp.concatenate(
        [functools.reduce(jnp.add, [y_refs[k][pl.ds(s, tm, stride=n_sub), :] * w_ref[:, k:k + 1]
                                    for k in range(TOP_K)])
         for s in range(n_sub)], axis=1)
    o_ref[...] = x1_ref[...] + mod_ref[0, 5:6, :] * _rms(ff, g_ref[...])


def _combine(y4, top_w, x1, mod, g_post, seq):
    n, d = x1.shape
    tm = min(ROW_TILE, seq)
    per_b = seq // tm
    n_sub = d // LANES
    y_specs = [pl.BlockSpec((tm * n_sub, LANES), functools.partial(lambda k, i: (k * (n // tm) + i, 0), k))
               for k in range(TOP_K)]
    return pl.pallas_call(
        _combine_kernel,
        grid=(n // tm,),
        in_specs=y_specs + [
                  pl.BlockSpec((tm, LANES), lambda i: (i, 0)),
                  pl.BlockSpec((tm, d), lambda i: (i, 0)),
                  pl.BlockSpec((1,) + mod.shape[1:], lambda i: (i // per_b, 0, 0)),
                  pl.BlockSpec((1, d), lambda i: (0, 0))],
        out_specs=pl.BlockSpec((tm, d), lambda i: (i, 0)),
        out_shape=jax.ShapeDtypeStruct((n, d), F32),
        compiler_params=_cparams(("arbitrary",)),
        name="combine",
    )(*([y4] * TOP_K), top_w, x1, mod, g_post.reshape(1, d))


def _layer(x, c, w_ada, b_ada, g_pre_mix, g_post_mix, w_in, rel_bias, w_branch_sb, w_branch_ca, w_gate,
           b_gate, w_out, g_pre_ffn, g_post_ffn, w_router, b_router, w_e1, b_e1, w_e2, b_e2):
    batch, seq, d = x.shape
    n = batch * seq
    sb_w = w_branch_sb.shape[0]
    ca_w = w_branch_ca.shape[0]
    n_exp = w_router.shape[1]
    x2 = x.reshape(n, d)

    mod = _modulation(c, w_ada, b_ada)

    scale = HEAD_DIM ** -0.5
    col_scale = np.ones((w_in.shape[1],), np.float32)
    col_scale[:sb_w] = scale
    col_scale[3 * sb_w:3 * sb_w + ca_w] = scale * LOG2_E
    w_in_b = (w_in * col_scale[None, :]).astype(BF16)
    proj, gate = _premix(x2, mod, g_pre_mix, w_in_b, w_gate.astype(BF16), b_gate, seq)

    sbl, cal = sb_w // LANES, ca_w // LANES
    o_sb = _sb_attention(proj, batch, seq, sb_w, 0, sbl, 2 * sbl)
    o_ca = _ca_attention(proj, _ca_tables(rel_bias), batch, seq, ca_w, 3 * sbl, 3 * sbl + cal, 3 * sbl + 2 * cal)

    wr_pad = jnp.zeros((d, LANES), F32).at[:, :n_exp].set(w_router)
    br_pad = jnp.full((1, LANES), NEG, F32).at[0, :n_exp].set(b_router)
    x1, h2, top_e, top_w, cnt = _postmix(o_sb, o_ca, gate, x2, mod, g_post_mix, g_pre_ffn,
                                         w_branch_sb.astype(BF16), w_branch_ca.astype(BF16),
                                         w_out.astype(BF16), wr_pad, br_pad, seq)

    nk = n * TOP_K
    counts = cnt[0, :n_exp].astype(I32)
    padded = (counts + MOE_BLOCK - 1) // MOE_BLOCK * MOE_BLOCK
    pad_end = jnp.cumsum(padded)
    pad_start = pad_end - padded
    group_start = jnp.zeros((SUBLANES, LANES), F32).at[0, :n_exp].set(pad_start.astype(F32))
    dest = _routing_ranks(top_e, group_start)[:, :TOP_K].reshape(nk)
    n_blocks = nk // MOE_BLOCK + n_exp
    rows = n_blocks * MOE_BLOCK
    spare = nk + np.arange(rows, dtype=np.int32) % (2 * MOE_BLOCK)
    assert n & (n - 1) == 0, "row ids are k * n + token with the token taken back by a bit mask"
    ids = np.arange(nk, dtype=np.int32)
    row_flat = jnp.asarray(spare).at[dest].set(jnp.asarray((ids % TOP_K) * n + ids // TOP_K))
    blk_start = jnp.arange(n_blocks, dtype=I32) * MOE_BLOCK
    block_e = jnp.minimum(jnp.sum((pad_end[None, :] <= blk_start[:, None]).astype(I32), axis=1), n_exp - 1)
    n_valid = (pad_end[-1] // MOE_BLOCK).astype(I32).reshape(1)

    y4 = _expert_ffn(h2, row_flat, block_e, n_valid, w_e1, b_e1, w_e2, b_e2)
    out = _combine(y4, top_w, x1, mod, g_post_ffn, seq)
    return out.reshape(batch, seq, d)


def kernel(x, c, w_ada, b_ada, g_pre_mix, g_post_mix, w_in, rel_bias, w_branch_sb, w_branch_ca, w_gate, b_gate,
           w_out, g_pre_ffn, g_post_ffn, w_router, b_router, w_e1, b_e1, w_e2, b_e2):
    for l in range(w_ada.shape[0]):
        x = _layer(x, c, w_ada[l], b_ada[l], g_pre_mix[l], g_post_mix[l], w_in[l], rel_bias[l],
                   w_branch_sb[l], w_branch_ca[l], w_gate[l], b_gate[l], w_out[l], g_pre_ffn[l],
                   g_post_ffn[l], w_router[l], b_router[l], w_e1[l], b_e1[l], w_e2[l], b_e2[l])
    return x
```

```python
import functools

import numpy as np
import jax
import jax.numpy as jnp
from jax import lax
from jax.experimental import pallas as pl
from jax.experimental.pallas import tpu as pltpu

F32 = jnp.float32
BF16 = jnp.bfloat16
I32 = jnp.int32

RMS_EPS = 1e-6
HEAD_DIM = 64
LANES = 128
SUBLANES = 8
CHUNK = 64
CA_LEFT_CHUNKS = 8
REL_CLIP = 256
TOP_K = 4
SWIGLU_ALPHA = 1.702
SWIGLU_LIMIT = 7.0
NEG = -1e30
LOG2_E = 1.4426950408889634

SB_TILE = 512
SB_KEY_TILE = 256
SB_ROWS = 128
CA_TILE = 2 * CHUNK
CA_KEY_TILES = CA_LEFT_CHUNKS * CHUNK // CA_TILE + 1
CA_WINDOW = CA_KEY_TILES * CA_TILE
ROW_TILE = 512
MOE_BLOCK = 256
MOE_COLS = 256
VMEM_LIMIT = 56 * 1024 * 1024


def _cparams(sem):
    return pltpu.CompilerParams(dimension_semantics=sem, vmem_limit_bytes=VMEM_LIMIT)


def _dot(a, b):
    return jnp.dot(a, b, preferred_element_type=F32)


def _dot_nt(a, b):
    return lax.dot_general(a, b, (((1,), (1,)), ((), ())), preferred_element_type=F32)


def _rms(x, g):
    return x * lax.rsqrt(jnp.mean(x * x, axis=-1, keepdims=True) + RMS_EPS) * g


def _mod_kernel(c_ref, w_ref, b_ref, o_ref):
    o_ref[...] = jnp.dot(c_ref[...], w_ref[...], preferred_element_type=F32,
                         precision=lax.Precision.HIGHEST) + b_ref[...]


def _modulation(c, w_ada, b_ada):
    b, d = c.shape
    n_out = w_ada.shape[1]
    c_pad = jnp.zeros((SUBLANES, d), F32).at[:b].set(c)
    out = pl.pallas_call(
        _mod_kernel,
        grid=(n_out // d,),
        in_specs=[pl.BlockSpec((SUBLANES, d), lambda j: (0, 0)),
                  pl.BlockSpec((d, d), lambda j: (0, j)),
                  pl.BlockSpec((1, d), lambda j: (0, j))],
        out_specs=pl.BlockSpec((SUBLANES, d), lambda j: (0, j)),
        out_shape=jax.ShapeDtypeStruct((SUBLANES, n_out), F32),
        compiler_params=_cparams(("arbitrary",)),
        name="adaln_mod",
    )(c_pad, w_ada, b_ada.reshape(1, n_out))
    return out[:b].reshape(b, n_out // d, d)


def _premix_kernel(x_ref, mod_ref, g_ref, win_ref, wg_ref, bg_ref, proj_ref, gate_ref, *, col):
    h = _rms(x_ref[...], g_ref[...]) * (1.0 + mod_ref[0, 1:2, :]) + mod_ref[0, 0:1, :]
    hb = h.astype(BF16)
    for c0 in range(0, proj_ref.shape[1], col):
        proj_ref[:, c0:c0 + col] = _dot(hb, win_ref[:, c0:c0 + col]).astype(BF16)
    for c0 in range(0, gate_ref.shape[1], col):
        g = jax.nn.sigmoid(_dot(hb, wg_ref[:, c0:c0 + col]) + bg_ref[:, c0:c0 + col])
        gate_ref[:, c0:c0 + col] = g.astype(BF16)


def _premix(x2, mod, g_pre, w_in_b, w_gate_b, b_gate, seq):
    n, d = x2.shape
    pw, gw = w_in_b.shape[1], w_gate_b.shape[1]
    tm = min(ROW_TILE, seq)
    per_b = seq // tm
    return pl.pallas_call(
        functools.partial(_premix_kernel, col=512),
        grid=(n // tm,),
        in_specs=[pl.BlockSpec((tm, d), lambda i: (i, 0)),
                  pl.BlockSpec((1,) + mod.shape[1:], lambda i: (i // per_b, 0, 0)),
                  pl.BlockSpec((1, d), lambda i: (0, 0)),
                  pl.BlockSpec((d, pw), lambda i: (0, 0)),
                  pl.BlockSpec((d, gw), lambda i: (0, 0)),
                  pl.BlockSpec((1, gw), lambda i: (0, 0))],
        out_specs=[pl.BlockSpec((tm, pw), lambda i: (i, 0)),
                   pl.BlockSpec((tm, gw), lambda i: (i, 0))],
        out_shape=[jax.ShapeDtypeStruct((n, pw), BF16), jax.ShapeDtypeStruct((n, gw), BF16)],
        compiler_params=_cparams(("arbitrary",)),
        name="premix",
    )(x2, mod, g_pre.reshape(1, d), w_in_b, w_gate_b, b_gate.reshape(1, gw))


def _sb_kernel(q_ref, k_ref, v_ref, u_ref, o_ref, z_scr, cum_scr, carry_scr, acc_scr):
    i = pl.program_id(2)
    t = q_ref.shape[0]
    tk = u_ref.shape[1]
    per_q = t // tk
    rows = SB_ROWS
    q2 = q_ref[...]
    lane = lax.broadcasted_iota(I32, (t, LANES), 1)
    u2 = u_ref[...]
    chains = [(hh, r0) for hh in range(2) for r0 in range(0, t, rows)]
    qs = []
    for hh, r0 in chains:
        in_head = (lane < HEAD_DIM) if hh == 0 else (lane >= HEAD_DIM)
        qs.append(jnp.where(in_head, q2, jnp.zeros_like(q2))[r0:r0 + rows])

    def causal(j_local):
        return [lax.broadcasted_iota(I32, (rows, tk), 1) + j_local * tk
                < lax.broadcasted_iota(I32, (rows, tk), 0) + r0 for _, r0 in chains]

    n_ch = len(chains)

    def logits(j, slot):
        start = pl.multiple_of(j * tk, tk)
        kt = k_ref[pl.ds(start, tk), :]
        for c, qh in enumerate(qs):
            z_scr[slot, c] = _dot_nt(qh, kt)

    def sums(slot, mask):
        for c in range(n_ch):
            z = z_scr[slot, c].astype(BF16)
            sp = jnp.maximum(z, 0) + jnp.log(1 + jnp.exp(-jnp.abs(z)))
            if mask is not None:
                sp = jnp.where(mask[c], sp, jnp.zeros_like(sp))
            cum_scr[slot, c] = _dot(sp, u2)

    def finish(j, slot, mask):
        start = pl.multiple_of(j * tk, tk)
        vt = v_ref[pl.ds(start, tk), :]
        for c in range(n_ch):
            cum = cum_scr[slot, c]
            carry = carry_scr[c]
            w = jnp.exp(z_scr[slot, c] - cum - jnp.concatenate([carry] * (tk // LANES), axis=1))
            if mask is not None:
                w = jnp.where(mask[c], w, 0.0)
            acc_scr[c] += _dot(w.astype(BF16), vt)
            carry_scr[c] = carry + jnp.broadcast_to(cum[:, 0:1], carry.shape)

    def sweep_tile(j, slot, mask, next_mask):
        logits(jnp.maximum(j - 1, 0), 1 - slot)
        finish(j, slot, mask)
        sums(1 - slot, next_mask)

    carry_scr[...] = jnp.zeros_like(carry_scr)
    acc_scr[...] = jnp.zeros_like(acc_scr)
    first = i * per_q
    logits(first + per_q - 1, (per_q - 1) % 2)
    sums((per_q - 1) % 2, causal(per_q - 1))
    for jl in range(per_q - 1, -1, -1):
        sweep_tile(first + jl, jl % 2, causal(jl), causal(jl - 1) if jl > 0 else None)

    def body(n, carry):
        j = first - 1 - 2 * n
        sweep_tile(j, 1, None, None)
        sweep_tile(j - 1, 0, None, None)
        return carry

    lax.fori_loop(0, first // 2, body, 0)
    accs = [jnp.concatenate([acc_scr[c] for c, (h2, _) in enumerate(chains) if h2 == hh], axis=0)
            for hh in range(2)]
    o_ref[...] = jnp.where(lane < HEAD_DIM, accs[0], accs[1]).astype(o_ref.dtype)


def _sb_attention(proj, batch, seq, width, q_col, k_col, v_col):
    n = proj.shape[0]
    t = min(SB_TILE, seq)
    tk = min(SB_KEY_TILE, t)
    assert (t // tk) % 2 == 0, "the sweep alternates two staging slots per key tile"
    nq = seq // t
    pairs = width // LANES
    n_chains = 2 * (t // SB_ROWS)
    tri = jnp.asarray((np.arange(tk)[:, None] >= np.arange(tk)[None, :]).astype(np.float32)).astype(BF16)
    return pl.pallas_call(
        _sb_kernel,
        grid=(batch, pairs, nq),
        in_specs=[pl.BlockSpec((t, LANES), lambda b, p, i: (b * nq + i, q_col + p)),
                  pl.BlockSpec((seq, LANES), lambda b, p, i: (b, k_col + p)),
                  pl.BlockSpec((seq, LANES), lambda b, p, i: (b, v_col + p)),
                  pl.BlockSpec((tk, tk), lambda b, p, i: (0, 0))],
        out_specs=pl.BlockSpec((t, LANES), lambda b, p, i: (b * nq + i, p)),
        out_shape=jax.ShapeDtypeStruct((n, width), BF16),
        scratch_shapes=[pltpu.VMEM((2, n_chains, SB_ROWS, tk), F32), pltpu.VMEM((2, n_chains, SB_ROWS, tk), F32),
                        pltpu.VMEM((n_chains, SB_ROWS, LANES), F32), pltpu.VMEM((n_chains, SB_ROWS, LANES), F32)],
        compiler_params=_cparams(("arbitrary", "arbitrary", "arbitrary")),
        name="stickbreak_attn",
    )(proj, proj, proj, tri)


def _ca_kernel(q_ref, k_ref, v_ref, t_ref, o_ref):
    r = pl.program_id(2)
    t = q_ref.shape[0]
    window = t_ref.shape[-1]
    start = pl.multiple_of(jnp.maximum(r - (CA_KEY_TILES - 1), 0) * t, t)
    kw = k_ref[pl.ds(start, window), :]
    vw = v_ref[pl.ds(start, window), :]
    q2 = q_ref[...]
    lane = lax.broadcasted_iota(I32, (t, LANES), 1)
    chains = [(hh, r0) for hh in range(2) for r0 in range(0, t, CHUNK)]
    zs = []
    for hh, r0 in chains:
        in_head = (lane < HEAD_DIM) if hh == 0 else (lane >= HEAD_DIM)
        qh = jnp.where(in_head, q2, jnp.zeros_like(q2))[r0:r0 + CHUNK]
        zs.append(_dot_nt(qh, kw) + t_ref[0, hh, r0:r0 + CHUNK, :])
    ps = [jnp.exp2(z - jnp.max(z, axis=-1, keepdims=True)) for z in zs]
    outs = [_dot(p.astype(BF16), vw) / jnp.sum(p, axis=-1, keepdims=True) for p in ps]
    heads = [jnp.concatenate([o for (h2, _), o in zip(chains, outs) if h2 == hh], axis=0) for hh in range(2)]
    o_ref[...] = jnp.where(lane < HEAD_DIM, heads[0], heads[1]).astype(o_ref.dtype)


def _ca_tables(rel_bias):
    h = rel_bias.shape[0]
    left = CA_LEFT_CHUNKS * CHUNK
    period = CA_WINDOW + CA_TILE
    u = np.arange(period)
    rel = np.where(u < CA_WINDOW, left - u, left + (period - u))
    v = rel_bias[:, np.clip(rel, -REL_CLIP, REL_CLIP) + REL_CLIP].astype(F32) * LOG2_E
    flat = jnp.tile(v, (1, CA_TILE))[:, :CA_TILE * (period - 1)]
    bias = flat.reshape(h, CA_TILE, period - 1)[:, :, :CA_WINDOW]
    i = np.arange(CA_TILE)[:, None]
    j = np.arange(CA_WINDOW)[None, :]
    first = (i // CHUNK) * CHUNK
    visible = (j >= first) & (j < first + (CA_LEFT_CHUNKS + 1) * CHUNK)
    full = jnp.where(visible[None], bias, NEG)
    tables = []
    for r in range(CA_KEY_TILES):
        shift = (CA_KEY_TILES - 1 - r) * CA_TILE
        tables.append(jnp.pad(full[:, :, shift:], ((0, 0), (0, 0), (0, shift)), constant_values=NEG))
    return jnp.stack(tables)


def _ca_attention(proj, tables, batch, seq, width, q_col, k_col, v_col):
    n = proj.shape[0]
    t = CA_TILE
    nq = seq // t
    pairs = width // LANES
    assert seq >= CA_WINDOW
    return pl.pallas_call(
        _ca_kernel,
        grid=(batch, pairs, nq),
        in_specs=[pl.BlockSpec((t, LANES), lambda b, p, i: (b * nq + i, q_col + p)),
                  pl.BlockSpec((seq, LANES), lambda b, p, i: (b, k_col + p)),
                  pl.BlockSpec((seq, LANES), lambda b, p, i: (b, v_col + p)),
                  pl.BlockSpec((1, 2, t, CA_WINDOW),
                               lambda b, p, i: (jnp.minimum(i, CA_KEY_TILES - 1), p, 0, 0))],
        out_specs=pl.BlockSpec((t, LANES), lambda b, p, i: (b * nq + i, p)),
        out_shape=jax.ShapeDtypeStruct((n, width), BF16),
        compiler_params=_cparams(("arbitrary", "arbitrary", "arbitrary")),
        name="chunkband_attn",
    )(proj, proj, proj, tables)


def _postmix_kernel(osb_ref, oca_ref, gate_ref, x_ref, mod_ref, gpost_ref, gffn_ref, wbs_ref, wbc_ref,
                    wout_ref, wr_ref, br_ref, x1_ref, h2_ref, tope_ref, topw_ref, cnt_ref):
    d = x_ref.shape[1]
    y_sb = _dot(osb_ref[...], wbs_ref[...])
    y_ca = _dot(oca_ref[...], wbc_ref[...])
    mixed_in = gate_ref[:, :d].astype(F32) * y_sb + gate_ref[:, d:].astype(F32) * y_ca
    mixed = _dot(mixed_in.astype(BF16), wout_ref[...])
    x1 = x_ref[...] + mod_ref[0, 2:3, :] * _rms(mixed, gpost_ref[...])
    x1_ref[...] = x1
    h2 = _rms(x1, gffn_ref[...]) * (1.0 + mod_ref[0, 4:5, :]) + mod_ref[0, 3:4, :]
    n_sub = d // LANES
    for s in range(n_sub):
        h2_ref[pl.ds(s, h2.shape[0], stride=n_sub), :] = h2[:, s * LANES:(s + 1) * LANES]
    logits = jnp.dot(h2, wr_ref[...], preferred_element_type=F32,
                     precision=lax.Precision.HIGHEST) + br_ref[...]
    lane = lax.broadcasted_iota(I32, logits.shape, 1)
    vals, idxs = [], []
    for _ in range(TOP_K):
        m = jnp.max(logits, axis=-1, keepdims=True)
        idx = jnp.min(jnp.where(logits == m, lane, LANES), axis=-1, keepdims=True)
        vals.append(m)
        idxs.append(idx)
        logits = jnp.where(lane == idx, 2 * NEG, logits)
    ex = [jnp.exp(v - vals[0]) for v in vals]
    denom = functools.reduce(jnp.add, ex)
    tope = jnp.zeros(logits.shape, I32)
    topw = jnp.zeros(logits.shape, F32)
    for k in range(TOP_K):
        tope = jnp.where(lane == k, idxs[k], tope)
        topw = jnp.where(lane == k, ex[k] / denom, topw)
    tope_ref[...] = tope
    topw_ref[...] = topw

    @pl.when(pl.program_id(0) == 0)
    def _():
        cnt_ref[...] = jnp.zeros_like(cnt_ref)

    chosen = functools.reduce(jnp.logical_or, [lane == idx for idx in idxs])
    cnt_ref[0:1, :] += jnp.sum(chosen.astype(F32), axis=0, keepdims=True)


def _postmix(o_sb, o_ca, gate, x2, mod, g_post, g_ffn, wbs_b, wbc_b, wout_b, wr_pad, br_pad, seq):
    n, d = x2.shape
    w = o_sb.shape[1]
    tm = min(ROW_TILE, seq)
    per_b = seq // tm
    row = lambda i: (i, 0)
    fixed = lambda i: (0, 0)
    return pl.pallas_call(
        _postmix_kernel,
        grid=(n // tm,),
        in_specs=[pl.BlockSpec((tm, w), row), pl.BlockSpec((tm, w), row),
                  pl.BlockSpec((tm, 2 * d), row), pl.BlockSpec((tm, d), row),
                  pl.BlockSpec((1,) + mod.shape[1:], lambda i: (i // per_b, 0, 0)),
                  pl.BlockSpec((1, d), fixed), pl.BlockSpec((1, d), fixed),
                  pl.BlockSpec((w, d), fixed), pl.BlockSpec((w, d), fixed),
                  pl.BlockSpec((d, d), fixed), pl.BlockSpec((d, LANES), fixed),
                  pl.BlockSpec((1, LANES), fixed)],
        out_specs=[pl.BlockSpec((tm, d), row), pl.BlockSpec((tm * (d // LANES), LANES), row),
                   pl.BlockSpec((tm, LANES), row), pl.BlockSpec((tm, LANES), row),
                   pl.BlockSpec((SUBLANES, LANES), fixed)],
        out_shape=[jax.ShapeDtypeStruct((n, d), F32), jax.ShapeDtypeStruct((n * (d // LANES), LANES), F32),
                   jax.ShapeDtypeStruct((n, LANES), I32), jax.ShapeDtypeStruct((n, LANES), F32),
                   jax.ShapeDtypeStruct((SUBLANES, LANES), F32)],
        compiler_params=_cparams(("arbitrary",)),
        name="postmix_router",
    )(o_sb, o_ca, gate, x2, mod, g_post.reshape(1, d), g_ffn.reshape(1, d), wbs_b, wbc_b, wout_b,
      wr_pad, br_pad)


def _rank_kernel(e_ref, l_ref, start_ref, dest_ref, run_ref):
    @pl.when(pl.program_id(0) == 0)
    def _():
        run_ref[...] = start_ref[...]

    e = e_ref[...]
    lane = lax.broadcasted_iota(I32, e.shape, 1)
    hots = [lane == e[:, k:k + 1] for k in range(TOP_K)]
    chosen = functools.reduce(jnp.logical_or, hots)
    before = _dot(l_ref[...], chosen.astype(BF16)) + run_ref[0:1, :]
    dest = jnp.zeros(e.shape, I32)
    for k in range(TOP_K):
        dk = jnp.sum(jnp.where(hots[k], before, 0.0), axis=-1, keepdims=True).astype(I32)
        dest = jnp.where(lane == k, dk, dest)
    dest_ref[...] = dest
    run_ref[0:1, :] += jnp.sum(chosen.astype(F32), axis=0, keepdims=True)


def _routing_ranks(top_e, group_start):
    n = top_e.shape[0]
    t = min(ROW_TILE, n)
    strict_lower = (jnp.arange(t)[:, None] > jnp.arange(t)[None, :]).astype(BF16)
    return pl.pallas_call(
        _rank_kernel,
        grid=(n // t,),
        in_specs=[pl.BlockSpec((t, LANES), lambda i: (i, 0)),
                  pl.BlockSpec((t, t), lambda i: (0, 0)),
                  pl.BlockSpec((SUBLANES, LANES), lambda i: (0, 0))],
        out_specs=pl.BlockSpec((t, LANES), lambda i: (i, 0)),
        out_shape=jax.ShapeDtypeStruct((n, LANES), I32),
        scratch_shapes=[pltpu.VMEM((SUBLANES, LANES), F32)],
        compiler_params=_cparams(("arbitrary",)),
        name="routing_ranks",
    )(top_e, strict_lower, group_start)


def _moe_kernel(be_ref, nvalid_ref, rowflat_ref, h2_hbm, w1_ref, b1_ref, w2_ref, b2_ref, y4_hbm,
                xbuf, ybuf, w1b, w2b, hid, sem_in, sem_out, *, n_tokens):
    i = pl.program_id(0)
    n_blocks = pl.num_programs(0)
    nv = nvalid_ref[0]
    n_sub = h2_hbm.shape[1]
    blk = xbuf.shape[1] // n_sub
    d = n_sub * LANES
    di = w2b.shape[0]

    def tile(buf, slot, r):
        return buf.at[slot, pl.ds(r * n_sub, n_sub), :]

    def gather_copy(flat, r, slot):
        return pltpu.make_async_copy(h2_hbm.at[flat & (n_tokens - 1)], tile(xbuf, slot, r), sem_in.at[slot])

    def scatter_copy(flat, r, slot):
        return pltpu.make_async_copy(tile(ybuf, slot, r), y4_hbm.at[flat], sem_out.at[slot])

    def wait_gather(slot):
        pltpu.make_async_copy(xbuf.at[slot], xbuf.at[slot], sem_in.at[slot]).wait()

    def wait_scatter(slot):
        pltpu.make_async_copy(ybuf.at[slot], ybuf.at[slot], sem_out.at[slot]).wait()

    def step(slot):
        other = 1 - slot

        @pl.when(i == 0)
        def _():
            ybuf[...] = jnp.zeros_like(ybuf)

            def clear(r, c):
                for s in range(2):
                    scatter_copy(n_tokens * TOP_K + s * blk + r, r, s).start()
                return c
            lax.fori_loop(0, blk, clear, 0)
            wait_scatter(0)
            wait_scatter(1)

            def first(r, c):
                gather_copy(rowflat_ref[r], r, slot).start()
                return c
            lax.fori_loop(0, blk, first, 0)

        wait_gather(slot)

        @pl.when(jnp.logical_or(i == 0, be_ref[i] != be_ref[jnp.maximum(i - 1, 0)]))
        def _():
            w1b[...] = w1_ref[0].astype(BF16)
            w2b[...] = w2_ref[0].astype(BF16)

        nxt = jnp.minimum(i + 1, n_blocks - 1) * blk
        prev = jnp.where(i == 0, n_blocks - 1, i - 1) * blk
        n_hid, n_out = di // MOE_COLS, d // MOE_COLS
        g_stages = max(n_hid // 2, 1)
        s_stages = max(n_hid - g_stages, 1)

        def issue(stage):
            if stage < g_stages:
                for r in range(stage * blk // g_stages, (stage + 1) * blk // g_stages):
                    gather_copy(rowflat_ref[nxt + r], r, other).start(priority=r % 2)
            if stage >= n_hid - s_stages:
                st = stage - (n_hid - s_stages)
                for r in range(st * blk // s_stages, (st + 1) * blk // s_stages):
                    scatter_copy(rowflat_ref[prev + r], r, other).start(priority=r % 2)

        xb = jnp.concatenate([xbuf[slot, pl.ds(s, blk, stride=n_sub), :] for s in range(n_sub)],
                             axis=1).astype(BF16)
        for c in range(n_hid):
            issue(c)
            c0 = c * MOE_COLS
            gate = _dot(xb, w1b[:, c0:c0 + MOE_COLS]) + b1_ref[0, :, c0:c0 + MOE_COLS]
            up = _dot(xb, w1b[:, di + c0:di + c0 + MOE_COLS]) + b1_ref[0, :, di + c0:di + c0 + MOE_COLS]
            gate = jnp.minimum(gate, SWIGLU_LIMIT)
            up = jnp.clip(up, -SWIGLU_LIMIT, SWIGLU_LIMIT)
            hid[:, c0:c0 + MOE_COLS] = (gate * jax.nn.sigmoid(SWIGLU_ALPHA * gate) * (up + 1.0)).astype(BF16)
        @pl.when(i >= 1)
        def _():
            wait_scatter(slot)

        hb = hid[...]
        for c in range(n_out):
            c0 = c * MOE_COLS
            y = _dot(hb, w2b[:, c0:c0 + MOE_COLS]) + b2_ref[0, :, c0:c0 + MOE_COLS]
            for s in range(MOE_COLS // LANES):
                ybuf[slot, pl.ds(c0 // LANES + s, blk, stride=n_sub), :] = y[:, s * LANES:(s + 1) * LANES]

        @pl.when(i == nv - 1)
        def _():
            def last(r, c):
                scatter_copy(rowflat_ref[i * blk + r], r, slot).start()
                return c
            lax.fori_loop(0, blk, last, 0)
            wait_scatter(other)
            wait_scatter(slot)
            wait_gather(other)

    @pl.when(jnp.logical_and(i < nv, i % 2 == 0))
    def _():
        step(0)

    @pl.when(jnp.logical_and(i < nv, i % 2 == 1))
    def _():
        step(1)


def _expert_ffn(h2_tiles, row_flat, block_e, n_valid, w1, b1, w2, b2):
    e, d, two_i = w1.shape
    n_sub = d // LANES
    n = h2_tiles.shape[0] // n_sub
    h2 = h2_tiles.reshape(n, n_sub, LANES)
    n_blocks = block_e.shape[0]
    by_expert = lambda i, be, nv, rf: (be[i], 0, 0)
    grid_spec = pltpu.PrefetchScalarGridSpec(
        num_scalar_prefetch=3,
        grid=(n_blocks,),
        in_specs=[pl.BlockSpec(memory_space=pl.ANY),
                  pl.BlockSpec((1, d, two_i), by_expert),
                  pl.BlockSpec((1, 1, two_i), by_expert),
                  pl.BlockSpec((1, two_i // 2, d), by_expert),
                  pl.BlockSpec((1, 1, d), by_expert)],
        out_specs=pl.BlockSpec(memory_space=pl.ANY),
        scratch_shapes=[pltpu.VMEM((2, MOE_BLOCK * n_sub, LANES), F32), pltpu.VMEM((2, MOE_BLOCK * n_sub, LANES), F32),
                        pltpu.VMEM((d, two_i), BF16), pltpu.VMEM((two_i // 2, d), BF16),
                        pltpu.VMEM((MOE_BLOCK, two_i // 2), BF16),
                        pltpu.SemaphoreType.DMA((2,)), pltpu.SemaphoreType.DMA((2,))],
    )
    y4 = pl.pallas_call(
        functools.partial(_moe_kernel, n_tokens=n),
        grid_spec=grid_spec,
        out_shape=jax.ShapeDtypeStruct((n * TOP_K + 2 * MOE_BLOCK, n_sub, LANES), F32),
        compiler_params=_cparams(("arbitrary",)),
        name="expert_ffn",
    )(block_e, n_valid, row_flat, h2, w1, b1.reshape(e, 1, two_i), w2, b2.reshape(e, 1, d))
    return y4.reshape((n * TOP_K + 2 * MOE_BLOCK) * n_sub, LANES)


def _combine_kernel(*refs):
    y_refs, (w_ref, x1_ref, mod_ref, g_ref, o_ref) = refs[:TOP_K], refs[TOP_K:]
    tm, d = x1_ref.shape
    n_sub = d // LANES
    ff = jnp.concatenate(
        [functools.reduce(jnp.add, [y_refs[k][pl.ds(s, tm, stride=n_sub), :] * w_ref[:, k:k + 1]
                                    for k in range(TOP_K)])
         for s in range(n_sub)], axis=1)
    o_ref[...] = x1_ref[...] + mod_ref[0, 5:6, :] * _rms(ff, g_ref[...])


def _combine(y4, top_w, x1, mod, g_post, seq):
    n, d = x1.shape
    tm = min(ROW_TILE, seq)
    per_b = seq // tm
    n_sub = d // LANES
    y_specs = [pl.BlockSpec((tm * n_sub, LANES), functools.partial(lambda k, i: (k * (n // tm) + i, 0), k))
               for k in range(TOP_K)]
    return pl.pallas_call(
        _combine_kernel,
        grid=(n // tm,),
        in_specs=y_specs + [
                  pl.BlockSpec((tm, LANES), lambda i: (i, 0)),
                  pl.BlockSpec((tm, d), lambda i: (i, 0)),
                  pl.BlockSpec((1,) + mod.shape[1:], lambda i: (i // per_b, 0, 0)),
                  pl.BlockSpec((1, d), lambda i: (0, 0))],
        out_specs=pl.BlockSpec((tm, d), lambda i: (i, 0)),
        out_shape=jax.ShapeDtypeStruct((n, d), F32),
        compiler_params=_cparams(("arbitrary",)),
        name="combine",
    )(*([y4] * TOP_K), top_w, x1, mod, g_post.reshape(1, d))


def _layer(x, c, w_ada, b_ada, g_pre_mix, g_post_mix, w_in, rel_bias, w_branch_sb, w_branch_ca, w_gate,
           b_gate, w_out, g_pre_ffn, g_post_ffn, w_router, b_router, w_e1, b_e1, w_e2, b_e2):
    batch, seq, d = x.shape
    n = batch * seq
    sb_w = w_branch_sb.shape[0]
    ca_w = w_branch_ca.shape[0]
    n_exp = w_router.shape[1]
    x2 = x.reshape(n, d)

    mod = _modulation(c, w_ada, b_ada)

    scale = HEAD_DIM ** -0.5
    col_scale = np.ones((w_in.shape[1],), np.float32)
    col_scale[:sb_w] = scale
    col_scale[3 * sb_w:3 * sb_w + ca_w] = scale * LOG2_E
    w_in_b = (w_in * col_scale[None, :]).astype(BF16)
    proj, gate = _premix(x2, mod, g_pre_mix, w_in_b, w_gate.astype(BF16), b_gate, seq)

    sbl, cal = sb_w // LANES, ca_w // LANES
    o_sb = _sb_attention(proj, batch, seq, sb_w, 0, sbl, 2 * sbl)
    o_ca = _ca_attention(proj, _ca_tables(rel_bias), batch, seq, ca_w, 3 * sbl, 3 * sbl + cal, 3 * sbl + 2 * cal)

    wr_pad = jnp.zeros((d, LANES), F32).at[:, :n_exp].set(w_router)
    br_pad = jnp.full((1, LANES), NEG, F32).at[0, :n_exp].set(b_router)
    x1, h2, top_e, top_w, cnt = _postmix(o_sb, o_ca, gate, x2, mod, g_post_mix, g_pre_ffn,
                                         w_branch_sb.astype(BF16), w_branch_ca.astype(BF16),
                                         w_out.astype(BF16), wr_pad, br_pad, seq)

    nk = n * TOP_K
    counts = cnt[0, :n_exp].astype(I32)
    padded = (counts + MOE_BLOCK - 1) // MOE_BLOCK * MOE_BLOCK
    pad_end = jnp.cumsum(padded)
    pad_start = pad_end - padded
    group_start = jnp.zeros((SUBLANES, LANES), F32).at[0, :n_exp].set(pad_start.astype(F32))
    dest = _routing_ranks(top_e, group_start)[:, :TOP_K].reshape(nk)
    n_blocks = nk // MOE_BLOCK + n_exp
    rows = n_blocks * MOE_BLOCK
    spare = nk + np.arange(rows, dtype=np.int32) % (2 * MOE_BLOCK)
    assert n & (n - 1) == 0, "row ids are k * n + token with the token taken back by a bit mask"
    ids = np.arange(nk, dtype=np.int32)
    row_flat = jnp.asarray(spare).at[dest].set(jnp.asarray((ids % TOP_K) * n + ids // TOP_K))
    blk_start = jnp.arange(n_blocks, dtype=I32) * MOE_BLOCK
    block_e = jnp.minimum(jnp.sum((pad_end[None, :] <= blk_start[:, None]).astype(I32), axis=1), n_exp - 1)
    n_valid = (pad_end[-1] // MOE_BLOCK).astype(I32).reshape(1)

    y4 = _expert_ffn(h2, row_flat, block_e, n_valid, w_e1, b_e1, w_e2, b_e2)
    out = _combine(y4, top_w, x1, mod, g_post_ffn, seq)
    return out.reshape(batch, seq, d)


def kernel(x, c, w_ada, b_ada, g_pre_mix, g_post_mix, w_in, rel_bias, w_branch_sb, w_branch_ca, w_gate, b_gate,
           w_out, g_pre_ffn, g_post_ffn, w_router, b_router, w_e1, b_e1, w_e2, b_e2):
    for l in range(w_ada.shape[0]):
        x = _layer(x, c, w_ada[l], b_ada[l], g_pre_mix[l], g_post_mix[l], w_in[l], rel_bias[l],
                   w_branch_sb[l], w_branch_ca[l], w_gate[l], b_gate[l], w_out[l], g_pre_ffn[l],
                   g_post_ffn[l], w_router[l], b_router[l], w_e1[l], b_e1[l], w_e2[l], b_e2[l])
    return x
```

```python
import functools

import numpy as np
import jax
import jax.numpy as jnp
from jax import lax
from jax.experimental import pallas as pl
from jax.experimental.pallas import tpu as pltpu

F32 = jnp.float32
BF16 = jnp.bfloat16
I32 = jnp.int32

RMS_EPS = 1e-6
HEAD_DIM = 64
LANES = 128
SUBLANES = 8
CHUNK = 64
CA_LEFT_CHUNKS = 8
REL_CLIP = 256
TOP_K = 4
SWIGLU_ALPHA = 1.702
SWIGLU_LIMIT = 7.0
NEG = -1e30
LOG2_E = 1.4426950408889634

SB_TILE = 512
SB_KEY_TILE = 256
SB_ROWS = 128
CA_TILE = 4 * CHUNK
CA_KEY_TILES = CA_LEFT_CHUNKS * CHUNK // CA_TILE + 1
CA_WINDOW = CA_KEY_TILES * CA_TILE
ROW_TILE = 512
MOE_BLOCK = 256
MOE_COLS = 256
VMEM_LIMIT = 56 * 1024 * 1024


def _cparams(sem):
    return pltpu.CompilerParams(dimension_semantics=sem, vmem_limit_bytes=VMEM_LIMIT)


def _dot(a, b):
    return jnp.dot(a, b, preferred_element_type=F32)


def _dot_nt(a, b):
    return lax.dot_general(a, b, (((1,), (1,)), ((), ())), preferred_element_type=F32)


def _rms(x, g):
    return x * lax.rsqrt(jnp.mean(x * x, axis=-1, keepdims=True) + RMS_EPS) * g


def _mod_kernel(c_ref, w_ref, b_ref, o_ref):
    o_ref[...] = jnp.dot(c_ref[...], w_ref[...], preferred_element_type=F32,
                         precision=lax.Precision.HIGHEST) + b_ref[...]


def _modulation(c, w_ada, b_ada):
    b, d = c.shape
    n_out = w_ada.shape[1]
    c_pad = jnp.zeros((SUBLANES, d), F32).at[:b].set(c)
    out = pl.pallas_call(
        _mod_kernel,
        grid=(n_out // d,),
        in_specs=[pl.BlockSpec((SUBLANES, d), lambda j: (0, 0)),
                  pl.BlockSpec((d, d), lambda j: (0, j)),
                  pl.BlockSpec((1, d), lambda j: (0, j))],
        out_specs=pl.BlockSpec((SUBLANES, d), lambda j: (0, j)),
        out_shape=jax.ShapeDtypeStruct((SUBLANES, n_out), F32),
        compiler_params=_cparams(("arbitrary",)),
        name="adaln_mod",
    )(c_pad, w_ada, b_ada.reshape(1, n_out))
    return out[:b].reshape(b, n_out // d, d)


def _premix_kernel(x_ref, mod_ref, g_ref, win_ref, wg_ref, bg_ref, proj_ref, gate_ref, *, col):
    h = _rms(x_ref[...], g_ref[...]) * (1.0 + mod_ref[0, 1:2, :]) + mod_ref[0, 0:1, :]
    hb = h.astype(BF16)
    for c0 in range(0, proj_ref.shape[1], col):
        proj_ref[:, c0:c0 + col] = _dot(hb, win_ref[:, c0:c0 + col]).astype(BF16)
    for c0 in range(0, gate_ref.shape[1], col):
        g = jax.nn.sigmoid(_dot(hb, wg_ref[:, c0:c0 + col]) + bg_ref[:, c0:c0 + col])
        gate_ref[:, c0:c0 + col] = g.astype(BF16)


def _premix(x2, mod, g_pre, w_in_b, w_gate_b, b_gate, seq):
    n, d = x2.shape
    pw, gw = w_in_b.shape[1], w_gate_b.shape[1]
    tm = min(ROW_TILE, seq)
    per_b = seq // tm
    return pl.pallas_call(
        functools.partial(_premix_kernel, col=512),
        grid=(n // tm,),
        in_specs=[pl.BlockSpec((tm, d), lambda i: (i, 0)),
                  pl.BlockSpec((1,) + mod.shape[1:], lambda i: (i // per_b, 0, 0)),
                  pl.BlockSpec((1, d), lambda i: (0, 0)),
                  pl.BlockSpec((d, pw), lambda i: (0, 0)),
                  pl.BlockSpec((d, gw), lambda i: (0, 0)),
                  pl.BlockSpec((1, gw), lambda i: (0, 0))],
        out_specs=[pl.BlockSpec((tm, pw), lambda i: (i, 0)),
                   pl.BlockSpec((tm, gw), lambda i: (i, 0))],
        out_shape=[jax.ShapeDtypeStruct((n, pw), BF16), jax.ShapeDtypeStruct((n, gw), BF16)],
        compiler_params=_cparams(("arbitrary",)),
        name="premix",
    )(x2, mod, g_pre.reshape(1, d), w_in_b, w_gate_b, b_gate.reshape(1, gw))


def _sb_kernel(q_ref, k_ref, v_ref, u_ref, o_ref, z_scr, cum_scr, carry_scr, acc_scr):
    i = pl.program_id(2)
    t = q_ref.shape[0]
    tk = u_ref.shape[1]
    per_q = t // tk
    rows = SB_ROWS
    q2 = q_ref[...]
    lane = lax.broadcasted_iota(I32, (t, LANES), 1)
    u2 = u_ref[...]
    chains = [(hh, r0) for hh in range(2) for r0 in range(0, t, rows)]
    qs = []
    for hh, r0 in chains:
        in_head = (lane < HEAD_DIM) if hh == 0 else (lane >= HEAD_DIM)
        qs.append(jnp.where(in_head, q2, jnp.zeros_like(q2))[r0:r0 + rows])

    def causal(j_local):
        return [lax.broadcasted_iota(I32, (rows, tk), 1) + j_local * tk
                < lax.broadcasted_iota(I32, (rows, tk), 0) + r0 for _, r0 in chains]

    n_ch = len(chains)

    def logits(j, slot):
        start = pl.multiple_of(j * tk, tk)
        kt = k_ref[pl.ds(start, tk), :]
        for c, qh in enumerate(qs):
            z_scr[slot, c] = _dot_nt(qh, kt)

    def sums(slot, mask):
        for c in range(n_ch):
            z = z_scr[slot, c].astype(BF16)
            sp = jnp.maximum(z, 0) + jnp.log(1 + jnp.exp(-jnp.abs(z)))
            if mask is not None:
                sp = jnp.where(mask[c], sp, jnp.zeros_like(sp))
            cum_scr[slot, c] = _dot(sp, u2)

    def finish(j, slot, mask):
        start = pl.multiple_of(j * tk, tk)
        vt = v_ref[pl.ds(start, tk), :]
        for c in range(n_ch):
            cum = cum_scr[slot, c]
            carry = carry_scr[c]
            w = jnp.exp(z_scr[slot, c] - cum - jnp.concatenate([carry] * (tk // LANES), axis=1))
            if mask is not None:
                w = jnp.where(mask[c], w, 0.0)
            acc_scr[c] += _dot(w.astype(BF16), vt)
            carry_scr[c] = carry + jnp.broadcast_to(cum[:, 0:1], carry.shape)

    def sweep_tile(j, slot, mask, next_mask):
        logits(jnp.maximum(j - 1, 0), 1 - slot)
        finish(j, slot, mask)
        sums(1 - slot, next_mask)

    carry_scr[...] = jnp.zeros_like(carry_scr)
    acc_scr[...] = jnp.zeros_like(acc_scr)
    first = i * per_q
    logits(first + per_q - 1, (per_q - 1) % 2)
    sums((per_q - 1) % 2, causal(per_q - 1))
    for jl in range(per_q - 1, -1, -1):
        sweep_tile(first + jl, jl % 2, causal(jl), causal(jl - 1) if jl > 0 else None)

    def body(n, carry):
        j = first - 1 - 2 * n
        sweep_tile(j, 1, None, None)
        sweep_tile(j - 1, 0, None, None)
        return carry

    lax.fori_loop(0, first // 2, body, 0)
    accs = [jnp.concatenate([acc_scr[c] for c, (h2, _) in enumerate(chains) if h2 == hh], axis=0)
            for hh in range(2)]
    o_ref[...] = jnp.where(lane < HEAD_DIM, accs[0], accs[1]).astype(o_ref.dtype)


def _sb_attention(proj, batch, seq, width, q_col, k_col, v_col):
    n = proj.shape[0]
    t = min(SB_TILE, seq)
    tk = min(SB_KEY_TILE, t)
    assert (t // tk) % 2 == 0, "the sweep alternates two staging slots per key tile"
    nq = seq // t
    pairs = width // LANES
    n_chains = 2 * (t // SB_ROWS)
    tri = jnp.asarray((np.arange(tk)[:, None] >= np.arange(tk)[None, :]).astype(np.float32)).astype(BF16)
    return pl.pallas_call(
        _sb_kernel,
        grid=(batch, pairs, nq),
        in_specs=[pl.BlockSpec((t, LANES), lambda b, p, i: (b * nq + i, q_col + p)),
                  pl.BlockSpec((seq, LANES), lambda b, p, i: (b, k_col + p)),
                  pl.BlockSpec((seq, LANES), lambda b, p, i: (b, v_col + p)),
                  pl.BlockSpec((tk, tk), lambda b, p, i: (0, 0))],
        out_specs=pl.BlockSpec((t, LANES), lambda b, p, i: (b * nq + i, p)),
        out_shape=jax.ShapeDtypeStruct((n, width), BF16),
        scratch_shapes=[pltpu.VMEM((2, n_chains, SB_ROWS, tk), F32), pltpu.VMEM((2, n_chains, SB_ROWS, tk), F32),
                        pltpu.VMEM((n_chains, SB_ROWS, LANES), F32), pltpu.VMEM((n_chains, SB_ROWS, LANES), F32)],
        compiler_params=_cparams(("arbitrary", "arbitrary", "arbitrary")),
        name="stickbreak_attn",
    )(proj, proj, proj, tri)


def _ca_kernel(q_ref, k_ref, v_ref, t_ref, o_ref):
    r = pl.program_id(2)
    t = q_ref.shape[0]
    window = t_ref.shape[-1]
    start = pl.multiple_of(jnp.maximum(r - (CA_KEY_TILES - 1), 0) * t, t)
    kw = k_ref[pl.ds(start, window), :]
    vw = v_ref[pl.ds(start, window), :]
    q2 = q_ref[...]
    lane = lax.broadcasted_iota(I32, (t, LANES), 1)
    chains = [(hh, r0) for hh in range(2) for r0 in range(0, t, CHUNK)]
    zs = []
    for hh, r0 in chains:
        in_head = (lane < HEAD_DIM) if hh == 0 else (lane >= HEAD_DIM)
        qh = jnp.where(in_head, q2, jnp.zeros_like(q2))[r0:r0 + CHUNK]
        zs.append(_dot_nt(qh, kw) + t_ref[0, hh, r0:r0 + CHUNK, :])
    ps = [jnp.exp2(z - jnp.max(z, axis=-1, keepdims=True)) for z in zs]
    outs = [_dot(p.astype(BF16), vw) / jnp.sum(p, axis=-1, keepdims=True) for p in ps]
    heads = [jnp.concatenate([o for (h2, _), o in zip(chains, outs) if h2 == hh], axis=0) for hh in range(2)]
    o_ref[...] = jnp.where(lane < HEAD_DIM, heads[0], heads[1]).astype(o_ref.dtype)


def _ca_tables(rel_bias):
    h = rel_bias.shape[0]
    left = CA_LEFT_CHUNKS * CHUNK
    period = CA_WINDOW + CA_TILE
    u = np.arange(period)
    rel = np.where(u < CA_WINDOW, left - u, left + (period - u))
    v = rel_bias[:, np.clip(rel, -REL_CLIP, REL_CLIP) + REL_CLIP].astype(F32) * LOG2_E
    flat = jnp.tile(v, (1, CA_TILE))[:, :CA_TILE * (period - 1)]
    bias = flat.reshape(h, CA_TILE, period - 1)[:, :, :CA_WINDOW]
    i = np.arange(CA_TILE)[:, None]
    j = np.arange(CA_WINDOW)[None, :]
    first = (i // CHUNK) * CHUNK
    visible = (j >= first) & (j < first + (CA_LEFT_CHUNKS + 1) * CHUNK)
    full = jnp.where(visible[None], bias, NEG)
    tables = []
    for r in range(CA_KEY_TILES):
        shift = (CA_KEY_TILES - 1 - r) * CA_TILE
        tables.append(jnp.pad(full[:, :, shift:], ((0, 0), (0, 0), (0, shift)), constant_values=NEG))
    return jnp.stack(tables)


def _ca_attention(proj, tables, batch, seq, width, q_col, k_col, v_col):
    n = proj.shape[0]
    t = CA_TILE
    nq = seq // t
    pairs = width // LANES
    assert seq >= CA_WINDOW
    return pl.pallas_call(
        _ca_kernel,
        grid=(batch, pairs, nq),
        in_specs=[pl.BlockSpec((t, LANES), lambda b, p, i: (b * nq + i, q_col + p)),
                  pl.BlockSpec((seq, LANES), lambda b, p, i: (b, k_col + p)),
                  pl.BlockSpec((seq, LANES), lambda b, p, i: (b, v_col + p)),
                  pl.BlockSpec((1, 2, t, CA_WINDOW),
                               lambda b, p, i: (jnp.minimum(i, CA_KEY_TILES - 1), p, 0, 0))],
        out_specs=pl.BlockSpec((t, LANES), lambda b, p, i: (b * nq + i, p)),
        out_shape=jax.ShapeDtypeStruct((n, width), BF16),
        compiler_params=_cparams(("arbitrary", "arbitrary", "arbitrary")),
        name="chunkband_attn",
    )(proj, proj, proj, tables)


def _postmix_kernel(osb_ref, oca_ref, gate_ref, x_ref, mod_ref, gpost_ref, gffn_ref, wbs_ref, wbc_ref,
                    wout_ref, wr_ref, br_ref, x1_ref, h2_ref, tope_ref, topw_ref, cnt_ref):
    d = x_ref.shape[1]
    y_sb = _dot(osb_ref[...], wbs_ref[...])
    y_ca = _dot(oca_ref[...], wbc_ref[...])
    mixed_in = gate_ref[:, :d].astype(F32) * y_sb + gate_ref[:, d:].astype(F32) * y_ca
    mixed = _dot(mixed_in.astype(BF16), wout_ref[...])
    x1 = x_ref[...] + mod_ref[0, 2:3, :] * _rms(mixed, gpost_ref[...])
    x1_ref[...] = x1
    h2 = _rms(x1, gffn_ref[...]) * (1.0 + mod_ref[0, 4:5, :]) + mod_ref[0, 3:4, :]
    n_sub = d // LANES
    for s in range(n_sub):
        h2_ref[pl.ds(s, h2.shape[0], stride=n_sub), :] = h2[:, s * LANES:(s + 1) * LANES]
    h_hi = h2.astype(BF16)
    h_lo = (h2 - h_hi.astype(F32)).astype(BF16)
    logits = _dot(jnp.concatenate([h_hi, h_lo, h_hi], axis=1), wr_ref[...]) + br_ref[...]
    lane = lax.broadcasted_iota(I32, logits.shape, 1)
    vals, idxs = [], []
    for _ in range(TOP_K):
        m = jnp.max(logits, axis=-1, keepdims=True)
        idx = jnp.min(jnp.where(logits == m, lane, LANES), axis=-1, keepdims=True)
        vals.append(m)
        idxs.append(idx)
        logits = jnp.where(lane == idx, 2 * NEG, logits)
    ex = [jnp.exp(v - vals[0]) for v in vals]
    denom = functools.reduce(jnp.add, ex)
    tope = jnp.zeros(logits.shape, I32)
    topw = jnp.zeros(logits.shape, F32)
    for k in range(TOP_K):
        tope = jnp.where(lane == k, idxs[k], tope)
        topw = jnp.where(lane == k, ex[k] / denom, topw)
    tope_ref[...] = tope
    topw_ref[...] = topw

    @pl.when(pl.program_id(0) == 0)
    def _():
        cnt_ref[...] = jnp.zeros_like(cnt_ref)

    chosen = functools.reduce(jnp.logical_or, [lane == idx for idx in idxs])
    cnt_ref[0:1, :] += jnp.sum(chosen.astype(F32), axis=0, keepdims=True)


def _postmix(o_sb, o_ca, gate, x2, mod, g_post, g_ffn, wbs_b, wbc_b, wout_b, wr_pad, br_pad, seq):
    n, d = x2.shape
    w = o_sb.shape[1]
    tm = min(ROW_TILE, seq)
    per_b = seq // tm
    row = lambda i: (i, 0)
    fixed = lambda i: (0, 0)
    return pl.pallas_call(
        _postmix_kernel,
        grid=(n // tm,),
        in_specs=[pl.BlockSpec((tm, w), row), pl.BlockSpec((tm, w), row),
                  pl.BlockSpec((tm, 2 * d), row), pl.BlockSpec((tm, d), row),
                  pl.BlockSpec((1,) + mod.shape[1:], lambda i: (i // per_b, 0, 0)),
                  pl.BlockSpec((1, d), fixed), pl.BlockSpec((1, d), fixed),
                  pl.BlockSpec((w, d), fixed), pl.BlockSpec((w, d), fixed),
                  pl.BlockSpec((d, d), fixed), pl.BlockSpec((3 * d, LANES), fixed),
                  pl.BlockSpec((1, LANES), fixed)],
        out_specs=[pl.BlockSpec((tm, d), row), pl.BlockSpec((tm * (d // LANES), LANES), row),
                   pl.BlockSpec((tm, LANES), row), pl.BlockSpec((tm, LANES), row),
                   pl.BlockSpec((SUBLANES, LANES), fixed)],
        out_shape=[jax.ShapeDtypeStruct((n, d), F32), jax.ShapeDtypeStruct((n * (d // LANES), LANES), F32),
                   jax.ShapeDtypeStruct((n, LANES), I32), jax.ShapeDtypeStruct((n, LANES), F32),
                   jax.ShapeDtypeStruct((SUBLANES, LANES), F32)],
        compiler_params=_cparams(("arbitrary",)),
        name="postmix_router",
    )(o_sb, o_ca, gate, x2, mod, g_post.reshape(1, d), g_ffn.reshape(1, d), wbs_b, wbc_b, wout_b,
      wr_pad, br_pad)


def _rank_kernel(e_ref, l_ref, start_ref, dest_ref, run_ref):
    @pl.when(pl.program_id(0) == 0)
    def _():
        run_ref[...] = start_ref[...]

    e = e_ref[...]
    lane = lax.broadcasted_iota(I32, e.shape, 1)
    hots = [lane == e[:, k:k + 1] for k in range(TOP_K)]
    chosen = functools.reduce(jnp.logical_or, hots)
    before = _dot(l_ref[...], chosen.astype(BF16)) + run_ref[0:1, :]
    dest = jnp.zeros(e.shape, I32)
    for k in range(TOP_K):
        dk = jnp.sum(jnp.where(hots[k], before, 0.0), axis=-1, keepdims=True).astype(I32)
        dest = jnp.where(lane == k, dk, dest)
    dest_ref[...] = dest
    run_ref[0:1, :] += jnp.sum(chosen.astype(F32), axis=0, keepdims=True)


def _routing_ranks(top_e, group_start):
    n = top_e.shape[0]
    t = min(ROW_TILE, n)
    strict_lower = (jnp.arange(t)[:, None] > jnp.arange(t)[None, :]).astype(BF16)
    return pl.pallas_call(
        _rank_kernel,
        grid=(n // t,),
        in_specs=[pl.BlockSpec((t, LANES), lambda i: (i, 0)),
                  pl.BlockSpec((t, t), lambda i: (0, 0)),
                  pl.BlockSpec((SUBLANES, LANES), lambda i: (0, 0))],
        out_specs=pl.BlockSpec((t, LANES), lambda i: (i, 0)),
        out_shape=jax.ShapeDtypeStruct((n, LANES), I32),
        scratch_shapes=[pltpu.VMEM((SUBLANES, LANES), F32)],
        compiler_params=_cparams(("arbitrary",)),
        name="routing_ranks",
    )(top_e, strict_lower, group_start)


def _moe_kernel(be_ref, nvalid_ref, rowflat_ref, h2_hbm, w1_ref, b1_ref, w2_ref, b2_ref, y4_hbm,
                xbuf, ybuf, w1b, w2b, hid, sem_in, sem_out, *, n_tokens):
    i = pl.program_id(0)
    n_blocks = pl.num_programs(0)
    nv = nvalid_ref[0]
    n_sub = h2_hbm.shape[1]
    blk = xbuf.shape[1] // n_sub
    d = n_sub * LANES
    di = w2b.shape[0]

    def tile(buf, slot, r):
        return buf.at[slot, pl.ds(r * n_sub, n_sub), :]

    def gather_copy(flat, r, slot):
        return pltpu.make_async_copy(h2_hbm.at[flat & (n_tokens - 1)], tile(xbuf, slot, r), sem_in.at[slot])

    def scatter_copy(flat, r, slot):
        return pltpu.make_async_copy(tile(ybuf, slot, r), y4_hbm.at[flat], sem_out.at[slot])

    def wait_gather(slot):
        pltpu.make_async_copy(xbuf.at[slot], xbuf.at[slot], sem_in.at[slot]).wait()

    def wait_scatter(slot):
        pltpu.make_async_copy(ybuf.at[slot], ybuf.at[slot], sem_out.at[slot]).wait()

    def step(slot):
        other = 1 - slot

        @pl.when(i == 0)
        def _():
            ybuf[...] = jnp.zeros_like(ybuf)

            def clear(r, c):
                for s in range(2):
                    scatter_copy(n_tokens * TOP_K + s * blk + r, r, s).start()
                return c
            lax.fori_loop(0, blk, clear, 0)
            wait_scatter(0)
            wait_scatter(1)

            def first(r, c):
                gather_copy(rowflat_ref[r], r, slot).start()
                return c
            lax.fori_loop(0, blk, first, 0)

        wait_gather(slot)

        @pl.when(jnp.logical_or(i == 0, be_ref[i] != be_ref[jnp.maximum(i - 1, 0)]))
        def _():
            w1b[...] = w1_ref[0].astype(BF16)
            w2b[...] = w2_ref[0].astype(BF16)

        nxt = jnp.minimum(i + 1, n_blocks - 1) * blk
        prev = jnp.where(i == 0, n_blocks - 1, i - 1) * blk
        n_hid, n_out = di // MOE_COLS, d // MOE_COLS
        g_stages = max(n_hid // 2, 1)
        s_stages = max(n_hid - g_stages, 1)

        def issue(stage):
            if stage < g_stages:
                for r in range(stage * blk // g_stages, (stage + 1) * blk // g_stages):
                    gather_copy(rowflat_ref[nxt + r], r, other).start(priority=r % 2)
            if stage >= n_hid - s_stages:
                st = stage - (n_hid - s_stages)
                for r in range(st * blk // s_stages, (st + 1) * blk // s_stages):
                    scatter_copy(rowflat_ref[prev + r], r, other).start(priority=r % 2)

        xb = jnp.concatenate([xbuf[slot, pl.ds(s, blk, stride=n_sub), :] for s in range(n_sub)],
                             axis=1).astype(BF16)
        for c in range(n_hid):
            issue(c)
            c0 = c * MOE_COLS
            gate = _dot(xb, w1b[:, c0:c0 + MOE_COLS]) + b1_ref[0, :, c0:c0 + MOE_COLS]
            up = _dot(xb, w1b[:, di + c0:di + c0 + MOE_COLS]) + b1_ref[0, :, di + c0:di + c0 + MOE_COLS]
            gate = jnp.minimum(gate, SWIGLU_LIMIT)
            up = jnp.clip(up, -SWIGLU_LIMIT, SWIGLU_LIMIT)
            hid[:, c0:c0 + MOE_COLS] = (gate * jax.nn.sigmoid(SWIGLU_ALPHA * gate) * (up + 1.0)).astype(BF16)
        @pl.when(i >= 1)
        def _():
            wait_scatter(slot)

        hb = hid[...]
        for c in range(n_out):
            c0 = c * MOE_COLS
            y = _dot(hb, w2b[:, c0:c0 + MOE_COLS]) + b2_ref[0, :, c0:c0 + MOE_COLS]
            for s in range(MOE_COLS // LANES):
                ybuf[slot, pl.ds(c0 // LANES + s, blk, stride=n_sub), :] = y[:, s * LANES:(s + 1) * LANES]

        @pl.when(i == nv - 1)
        def _():
            def last(r, c):
                scatter_copy(rowflat_ref[i * blk + r], r, slot).start()
                return c
            lax.fori_loop(0, blk, last, 0)
            wait_scatter(other)
            wait_scatter(slot)
            wait_gather(other)

    @pl.when(jnp.logical_and(i < nv, i % 2 == 0))
    def _():
        step(0)

    @pl.when(jnp.logical_and(i < nv, i % 2 == 1))
    def _():
        step(1)


def _expert_ffn(h2_tiles, row_flat, block_e, n_valid, w1, b1, w2, b2):
    e, d, two_i = w1.shape
    n_sub = d // LANES
    n = h2_tiles.shape[0] // n_sub
    h2 = h2_tiles.reshape(n, n_sub, LANES)
    n_blocks = block_e.shape[0]
    by_expert = lambda i, be, nv, rf: (be[i], 0, 0)
    grid_spec = pltpu.PrefetchScalarGridSpec(
        num_scalar_prefetch=3,
        grid=(n_blocks,),
        in_specs=[pl.BlockSpec(memory_space=pl.ANY),
                  pl.BlockSpec((1, d, two_i), by_expert),
                  pl.BlockSpec((1, 1, two_i), by_expert),
                  pl.BlockSpec((1, two_i // 2, d), by_expert),
                  pl.BlockSpec((1, 1, d), by_expert)],
        out_specs=pl.BlockSpec(memory_space=pl.ANY),
        scratch_shapes=[pltpu.VMEM((2, MOE_BLOCK * n_sub, LANES), F32), pltpu.VMEM((2, MOE_BLOCK * n_sub, LANES), F32),
                        pltpu.VMEM((d, two_i), BF16), pltpu.VMEM((two_i // 2, d), BF16),
                        pltpu.VMEM((MOE_BLOCK, two_i // 2), BF16),
                        pltpu.SemaphoreType.DMA((2,)), pltpu.SemaphoreType.DMA((2,))],
    )
    y4 = pl.pallas_call(
        functools.partial(_moe_kernel, n_tokens=n),
        grid_spec=grid_spec,
        out_shape=jax.ShapeDtypeStruct((n * TOP_K + 2 * MOE_BLOCK, n_sub, LANES), F32),
        compiler_params=_cparams(("arbitrary",)),
        name="expert_ffn",
    )(block_e, n_valid, row_flat, h2, w1, b1.reshape(e, 1, two_i), w2, b2.reshape(e, 1, d))
    return y4.reshape((n * TOP_K + 2 * MOE_BLOCK) * n_sub, LANES)


def _combine_kernel(*refs):
    y_refs, (w_ref, x1_ref, mod_ref, g_ref, o_ref) = refs[:TOP_K], refs[TOP_K:]
    tm, d = x1_ref.shape
    n_sub = d // LANES
    ff = jnp.concatenate(
        [functools.reduce(jnp.add, [y_refs[k][pl.ds(s, tm, stride=n_sub), :] * w_ref[:, k:k + 1]
                                    for k in range(TOP_K)])
         for s in range(n_sub)], axis=1)
    o_ref[...] = x1_ref[...] + mod_ref[0, 5:6, :] * _rms(ff, g_ref[...])


def _combine(y4, top_w, x1, mod, g_post, seq):
    n, d = x1.shape
    tm = min(ROW_TILE, seq)
    per_b = seq // tm
    n_sub = d // LANES
    y_specs = [pl.BlockSpec((tm * n_sub, LANES), functools.partial(lambda k, i: (k * (n // tm) + i, 0), k))
               for k in range(TOP_K)]
    return pl.pallas_call(
        _combine_kernel,
        grid=(n // tm,),
        in_specs=y_specs + [
                  pl.BlockSpec((tm, LANES), lambda i: (i, 0)),
                  pl.BlockSpec((tm, d), lambda i: (i, 0)),
                  pl.BlockSpec((1,) + mod.shape[1:], lambda i: (i // per_b, 0, 0)),
                  pl.BlockSpec((1, d), lambda i: (0, 0))],
        out_specs=pl.BlockSpec((tm, d), lambda i: (i, 0)),
        out_shape=jax.ShapeDtypeStruct((n, d), F32),
        compiler_params=_cparams(("arbitrary",)),
        name="combine",
    )(*([y4] * TOP_K), top_w, x1, mod, g_post.reshape(1, d))


def _layer(x, c, w_ada, b_ada, g_pre_mix, g_post_mix, w_in, rel_bias, w_branch_sb, w_branch_ca, w_gate,
           b_gate, w_out, g_pre_ffn, g_post_ffn, w_router, b_router, w_e1, b_e1, w_e2, b_e2):
    batch, seq, d = x.shape
    n = batch * seq
    sb_w = w_branch_sb.shape[0]
    ca_w = w_branch_ca.shape[0]
    n_exp = w_router.shape[1]
    x2 = x.reshape(n, d)

    mod = _modulation(c, w_ada, b_ada)

    scale = HEAD_DIM ** -0.5
    col_scale = np.ones((w_in.shape[1],), np.float32)
    col_scale[:sb_w] = scale
    col_scale[3 * sb_w:3 * sb_w + ca_w] = scale * LOG2_E
    w_in_b = (w_in * col_scale[None, :]).astype(BF16)
    proj, gate = _premix(x2, mod, g_pre_mix, w_in_b, w_gate.astype(BF16), b_gate, seq)

    sbl, cal = sb_w // LANES, ca_w // LANES
    o_sb = _sb_attention(proj, batch, seq, sb_w, 0, sbl, 2 * sbl)
    o_ca = _ca_attention(proj, _ca_tables(rel_bias), batch, seq, ca_w, 3 * sbl, 3 * sbl + cal, 3 * sbl + 2 * cal)

    wr_f32 = jnp.zeros((d, LANES), F32).at[:, :n_exp].set(w_router)
    wr_hi = wr_f32.astype(BF16)
    wr_lo = (wr_f32 - wr_hi.astype(F32)).astype(BF16)
    wr_pad = jnp.concatenate([wr_hi, wr_hi, wr_lo], axis=0)
    br_pad = jnp.full((1, LANES), NEG, F32).at[0, :n_exp].set(b_router)
    x1, h2, top_e, top_w, cnt = _postmix(o_sb, o_ca, gate, x2, mod, g_post_mix, g_pre_ffn,
                                         w_branch_sb.astype(BF16), w_branch_ca.astype(BF16),
                                         w_out.astype(BF16), wr_pad, br_pad, seq)

    nk = n * TOP_K
    counts = cnt[0, :n_exp].astype(I32)
    padded = (counts + MOE_BLOCK - 1) // MOE_BLOCK * MOE_BLOCK
    pad_end = jnp.cumsum(padded)
    pad_start = pad_end - padded
    group_start = jnp.zeros((SUBLANES, LANES), F32).at[0, :n_exp].set(pad_start.astype(F32))
    dest = _routing_ranks(top_e, group_start)[:, :TOP_K].reshape(nk)
    n_blocks = nk // MOE_BLOCK + n_exp
    rows = n_blocks * MOE_BLOCK
    spare = nk + np.arange(rows, dtype=np.int32) % (2 * MOE_BLOCK)
    assert n & (n - 1) == 0, "row ids are k * n + token with the token taken back by a bit mask"
    ids = np.arange(nk, dtype=np.int32)
    row_flat = jnp.asarray(spare).at[dest].set(jnp.asarray((ids % TOP_K) * n + ids // TOP_K))
    blk_start = jnp.arange(n_blocks, dtype=I32) * MOE_BLOCK
    block_e = jnp.minimum(jnp.sum((pad_end[None, :] <= blk_start[:, None]).astype(I32), axis=1), n_exp - 1)
    n_valid = (pad_end[-1] // MOE_BLOCK).astype(I32).reshape(1)

    y4 = _expert_ffn(h2, row_flat, block_e, n_valid, w_e1, b_e1, w_e2, b_e2)
    out = _combine(y4, top_w, x1, mod, g_post_ffn, seq)
    return out.reshape(batch, seq, d)


def kernel(x, c, w_ada, b_ada, g_pre_mix, g_post_mix, w_in, rel_bias, w_branch_sb, w_branch_ca, w_gate, b_gate,
           w_out, g_pre_ffn, g_post_ffn, w_router, b_router, w_e1, b_e1, w_e2, b_e2):
    for l in range(w_ada.shape[0]):
        x = _layer(x, c, w_ada[l], b_ada[l], g_pre_mix[l], g_post_mix[l], w_in[l], rel_bias[l],
                   w_branch_sb[l], w_branch_ca[l], w_gate[l], b_gate[l], w_out[l], g_pre_ffn[l],
                   g_post_ffn[l], w_router[l], b_router[l], w_e1[l], b_e1[l], w_e2[l], b_e2[l])
    return x
```

```python
import functools

import numpy as np
import jax
import jax.numpy as jnp
from jax import lax
from jax.experimental import pallas as pl
from jax.experimental.pallas import tpu as pltpu

F32 = jnp.float32
BF16 = jnp.bfloat16
I32 = jnp.int32

RMS_EPS = 1e-6
HEAD_DIM = 64
LANES = 128
SUBLANES = 8
CHUNK = 64
CA_LEFT_CHUNKS = 8
REL_CLIP = 256
TOP_K = 4
SWIGLU_ALPHA = 1.702
SWIGLU_LIMIT = 7.0
NEG = -1e30
LOG2_E = 1.4426950408889634

SB_TILE = 512
SB_KEY_TILE = 256
SB_ROWS = 128
CA_TILE = 4 * CHUNK
CA_KEY_TILES = CA_LEFT_CHUNKS * CHUNK // CA_TILE + 1
CA_WINDOW = CA_KEY_TILES * CA_TILE
ROW_TILE = 512
MOE_BLOCK = 512
MOE_COLS = 256
VMEM_LIMIT = 56 * 1024 * 1024


def _cparams(sem):
    return pltpu.CompilerParams(dimension_semantics=sem, vmem_limit_bytes=VMEM_LIMIT)


def _dot(a, b):
    return jnp.dot(a, b, preferred_element_type=F32)


def _dot_nt(a, b):
    return lax.dot_general(a, b, (((1,), (1,)), ((), ())), preferred_element_type=F32)


def _rms(x, g):
    return x * lax.rsqrt(jnp.mean(x * x, axis=-1, keepdims=True) + RMS_EPS) * g


def _mod_kernel(c_ref, w_ref, b_ref, o_ref):
    o_ref[...] = jnp.dot(c_ref[...], w_ref[...], preferred_element_type=F32,
                         precision=lax.Precision.HIGHEST) + b_ref[...]


def _modulation(c, w_ada, b_ada):
    b, d = c.shape
    n_out = w_ada.shape[1]
    c_pad = jnp.zeros((SUBLANES, d), F32).at[:b].set(c)
    out = pl.pallas_call(
        _mod_kernel,
        grid=(n_out // d,),
        in_specs=[pl.BlockSpec((SUBLANES, d), lambda j: (0, 0)),
                  pl.BlockSpec((d, d), lambda j: (0, j)),
                  pl.BlockSpec((1, d), lambda j: (0, j))],
        out_specs=pl.BlockSpec((SUBLANES, d), lambda j: (0, j)),
        out_shape=jax.ShapeDtypeStruct((SUBLANES, n_out), F32),
        compiler_params=_cparams(("arbitrary",)),
        name="adaln_mod",
    )(c_pad, w_ada, b_ada.reshape(1, n_out))
    return out[:b].reshape(b, n_out // d, d)


def _premix_kernel(x_ref, mod_ref, g_ref, win_ref, wg_ref, bg_ref, proj_ref, gate_ref, *, col):
    h = _rms(x_ref[...], g_ref[...]) * (1.0 + mod_ref[0, 1:2, :]) + mod_ref[0, 0:1, :]
    hb = h.astype(BF16)
    for c0 in range(0, proj_ref.shape[1], col):
        proj_ref[:, c0:c0 + col] = _dot(hb, win_ref[:, c0:c0 + col]).astype(BF16)
    for c0 in range(0, gate_ref.shape[1], col):
        g = jax.nn.sigmoid(_dot(hb, wg_ref[:, c0:c0 + col]) + bg_ref[:, c0:c0 + col])
        gate_ref[:, c0:c0 + col] = g.astype(BF16)


def _premix(x2, mod, g_pre, w_in_b, w_gate_b, b_gate, seq):
    n, d = x2.shape
    pw, gw = w_in_b.shape[1], w_gate_b.shape[1]
    tm = min(ROW_TILE, seq)
    per_b = seq // tm
    return pl.pallas_call(
        functools.partial(_premix_kernel, col=512),
        grid=(n // tm,),
        in_specs=[pl.BlockSpec((tm, d), lambda i: (i, 0)),
                  pl.BlockSpec((1,) + mod.shape[1:], lambda i: (i // per_b, 0, 0)),
                  pl.BlockSpec((1, d), lambda i: (0, 0)),
                  pl.BlockSpec((d, pw), lambda i: (0, 0)),
                  pl.BlockSpec((d, gw), lambda i: (0, 0)),
                  pl.BlockSpec((1, gw), lambda i: (0, 0))],
        out_specs=[pl.BlockSpec((tm, pw), lambda i: (i, 0)),
                   pl.BlockSpec((tm, gw), lambda i: (i, 0))],
        out_shape=[jax.ShapeDtypeStruct((n, pw), BF16), jax.ShapeDtypeStruct((n, gw), BF16)],
        compiler_params=_cparams(("arbitrary",)),
        name="premix",
    )(x2, mod, g_pre.reshape(1, d), w_in_b, w_gate_b, b_gate.reshape(1, gw))


def _sb_kernel(q_ref, k_ref, v_ref, u_ref, o_ref, z_scr, cum_scr, carry_scr, acc_scr):
    i = pl.program_id(2)
    t = q_ref.shape[0]
    tk = u_ref.shape[1]
    per_q = t // tk
    rows = SB_ROWS
    q2 = q_ref[...]
    lane = lax.broadcasted_iota(I32, (t, LANES), 1)
    u2 = u_ref[...]
    chains = [(hh, r0) for hh in range(2) for r0 in range(0, t, rows)]
    qs = []
    for hh, r0 in chains:
        in_head = (lane < HEAD_DIM) if hh == 0 else (lane >= HEAD_DIM)
        qs.append(jnp.where(in_head, q2, jnp.zeros_like(q2))[r0:r0 + rows])

    def causal(j_local):
        return [lax.broadcasted_iota(I32, (rows, tk), 1) + j_local * tk
                < lax.broadcasted_iota(I32, (rows, tk), 0) + r0 for _, r0 in chains]

    n_ch = len(chains)

    def logits(j, slot):
        start = pl.multiple_of(j * tk, tk)
        kt = k_ref[pl.ds(start, tk), :]
        for c, qh in enumerate(qs):
            z_scr[slot, c] = _dot_nt(qh, kt)

    def sums(slot, mask):
        for c in range(n_ch):
            z = z_scr[slot, c].astype(BF16)
            sp = jnp.maximum(z, 0) + jnp.log(1 + jnp.exp(-jnp.abs(z)))
            if mask is not None:
                sp = jnp.where(mask[c], sp, jnp.zeros_like(sp))
            cum_scr[slot, c] = _dot(sp, u2)

    def finish(j, slot, mask):
        start = pl.multiple_of(j * tk, tk)
        vt = v_ref[pl.ds(start, tk), :]
        for c in range(n_ch):
            cum = cum_scr[slot, c]
            carry = carry_scr[c]
            w = jnp.exp(z_scr[slot, c] - cum - jnp.concatenate([carry] * (tk // LANES), axis=1))
            if mask is not None:
                w = jnp.where(mask[c], w, 0.0)
            acc_scr[c] += _dot(w.astype(BF16), vt)
            carry_scr[c] = carry + jnp.broadcast_to(cum[:, 0:1], carry.shape)

    def sweep_tile(j, slot, mask, next_mask):
        logits(jnp.maximum(j - 1, 0), 1 - slot)
        finish(j, slot, mask)
        sums(1 - slot, next_mask)

    carry_scr[...] = jnp.zeros_like(carry_scr)
    acc_scr[...] = jnp.zeros_like(acc_scr)
    first = i * per_q
    logits(first + per_q - 1, (per_q - 1) % 2)
    sums((per_q - 1) % 2, causal(per_q - 1))
    for jl in range(per_q - 1, -1, -1):
        sweep_tile(first + jl, jl % 2, causal(jl), causal(jl - 1) if jl > 0 else None)

    def body(n, carry):
        j = first - 1 - 2 * n
        sweep_tile(j, 1, None, None)
        sweep_tile(j - 1, 0, None, None)
        return carry

    lax.fori_loop(0, first // 2, body, 0)
    accs = [jnp.concatenate([acc_scr[c] for c, (h2, _) in enumerate(chains) if h2 == hh], axis=0)
            for hh in range(2)]
    o_ref[...] = jnp.where(lane < HEAD_DIM, accs[0], accs[1]).astype(o_ref.dtype)


def _sb_attention(proj, batch, seq, width, q_col, k_col, v_col):
    n = proj.shape[0]
    t = min(SB_TILE, seq)
    tk = min(SB_KEY_TILE, t)
    assert (t // tk) % 2 == 0, "the sweep alternates two staging slots per key tile"
    nq = seq // t
    pairs = width // LANES
    n_chains = 2 * (t // SB_ROWS)
    tri = jnp.asarray((np.arange(tk)[:, None] >= np.arange(tk)[None, :]).astype(np.float32)).astype(BF16)
    return pl.pallas_call(
        _sb_kernel,
        grid=(batch, pairs, nq),
        in_specs=[pl.BlockSpec((t, LANES), lambda b, p, i: (b * nq + i, q_col + p)),
                  pl.BlockSpec((seq, LANES), lambda b, p, i: (b, k_col + p)),
                  pl.BlockSpec((seq, LANES), lambda b, p, i: (b, v_col + p)),
                  pl.BlockSpec((tk, tk), lambda b, p, i: (0, 0))],
        out_specs=pl.BlockSpec((t, LANES), lambda b, p, i: (b * nq + i, p)),
        out_shape=jax.ShapeDtypeStruct((n, width), BF16),
        scratch_shapes=[pltpu.VMEM((2, n_chains, SB_ROWS, tk), F32), pltpu.VMEM((2, n_chains, SB_ROWS, tk), F32),
                        pltpu.VMEM((n_chains, SB_ROWS, LANES), F32), pltpu.VMEM((n_chains, SB_ROWS, LANES), F32)],
        compiler_params=_cparams(("arbitrary", "arbitrary", "arbitrary")),
        name="stickbreak_attn",
    )(proj, proj, proj, tri)


def _ca_kernel(q_ref, k_ref, v_ref, t_ref, o_ref):
    r = pl.program_id(2)
    t = q_ref.shape[0]
    window = t_ref.shape[-1]
    start = pl.multiple_of(jnp.maximum(r - (CA_KEY_TILES - 1), 0) * t, t)
    kw = k_ref[pl.ds(start, window), :]
    vw = v_ref[pl.ds(start, window), :]
    q2 = q_ref[...]
    lane = lax.broadcasted_iota(I32, (t, LANES), 1)
    chains = [(hh, r0) for hh in range(2) for r0 in range(0, t, CHUNK)]
    zs = []
    for hh, r0 in chains:
        in_head = (lane < HEAD_DIM) if hh == 0 else (lane >= HEAD_DIM)
        qh = jnp.where(in_head, q2, jnp.zeros_like(q2))[r0:r0 + CHUNK]
        zs.append(_dot_nt(qh, kw) + t_ref[0, hh, r0:r0 + CHUNK, :])
    ps = [jnp.exp2(z - jnp.max(z, axis=-1, keepdims=True)) for z in zs]
    outs = [_dot(p.astype(BF16), vw) / jnp.sum(p, axis=-1, keepdims=True) for p in ps]
    heads = [jnp.concatenate([o for (h2, _), o in zip(chains, outs) if h2 == hh], axis=0) for hh in range(2)]
    o_ref[...] = jnp.where(lane < HEAD_DIM, heads[0], heads[1]).astype(o_ref.dtype)


def _ca_tables(rel_bias):
    h = rel_bias.shape[0]
    left = CA_LEFT_CHUNKS * CHUNK
    period = CA_WINDOW + CA_TILE
    u = np.arange(period)
    rel = np.where(u < CA_WINDOW, left - u, left + (period - u))
    v = rel_bias[:, np.clip(rel, -REL_CLIP, REL_CLIP) + REL_CLIP].astype(F32) * LOG2_E
    flat = jnp.tile(v, (1, CA_TILE))[:, :CA_TILE * (period - 1)]
    bias = flat.reshape(h, CA_TILE, period - 1)[:, :, :CA_WINDOW]
    i = np.arange(CA_TILE)[:, None]
    j = np.arange(CA_WINDOW)[None, :]
    first = (i // CHUNK) * CHUNK
    visible = (j >= first) & (j < first + (CA_LEFT_CHUNKS + 1) * CHUNK)
    full = jnp.where(visible[None], bias, NEG)
    tables = []
    for r in range(CA_KEY_TILES):
        shift = (CA_KEY_TILES - 1 - r) * CA_TILE
        tables.append(jnp.pad(full[:, :, shift:], ((0, 0), (0, 0), (0, shift)), constant_values=NEG))
    return jnp.stack(tables)


def _ca_attention(proj, tables, batch, seq, width, q_col, k_col, v_col):
    n = proj.shape[0]
    t = CA_TILE
    nq = seq // t
    pairs = width // LANES
    assert seq >= CA_WINDOW
    return pl.pallas_call(
        _ca_kernel,
        grid=(batch, pairs, nq),
        in_specs=[pl.BlockSpec((t, LANES), lambda b, p, i: (b * nq + i, q_col + p)),
                  pl.BlockSpec((seq, LANES), lambda b, p, i: (b, k_col + p)),
                  pl.BlockSpec((seq, LANES), lambda b, p, i: (b, v_col + p)),
                  pl.BlockSpec((1, 2, t, CA_WINDOW),
                               lambda b, p, i: (jnp.minimum(i, CA_KEY_TILES - 1), p, 0, 0))],
        out_specs=pl.BlockSpec((t, LANES), lambda b, p, i: (b * nq + i, p)),
        out_shape=jax.ShapeDtypeStruct((n, width), BF16),
        compiler_params=_cparams(("arbitrary", "arbitrary", "arbitrary")),
        name="chunkband_attn",
    )(proj, proj, proj, tables)


def _postmix_kernel(osb_ref, oca_ref, gate_ref, x_ref, mod_ref, gpost_ref, gffn_ref, wbs_ref, wbc_ref,
                    wout_ref, wr_ref, br_ref, x1_ref, h2_ref, tope_ref, topw_ref, cnt_ref):
    d = x_ref.shape[1]
    y_sb = _dot(osb_ref[...], wbs_ref[...])
    y_ca = _dot(oca_ref[...], wbc_ref[...])
    mixed_in = gate_ref[:, :d].astype(F32) * y_sb + gate_ref[:, d:].astype(F32) * y_ca
    mixed = _dot(mixed_in.astype(BF16), wout_ref[...])
    x1 = x_ref[...] + mod_ref[0, 2:3, :] * _rms(mixed, gpost_ref[...])
    x1_ref[...] = x1
    h2 = _rms(x1, gffn_ref[...]) * (1.0 + mod_ref[0, 4:5, :]) + mod_ref[0, 3:4, :]
    n_sub = d // LANES
    for s in range(n_sub):
        h2_ref[pl.ds(s, h2.shape[0], stride=n_sub), :] = h2[:, s * LANES:(s + 1) * LANES]
    h_hi = h2.astype(BF16)
    h_lo = (h2 - h_hi.astype(F32)).astype(BF16)
    logits = _dot(jnp.concatenate([h_hi, h_lo, h_hi], axis=1), wr_ref[...]) + br_ref[...]
    lane = lax.broadcasted_iota(I32, logits.shape, 1)
    vals, idxs = [], []
    for _ in range(TOP_K):
        m = jnp.max(logits, axis=-1, keepdims=True)
        idx = jnp.min(jnp.where(logits == m, lane, LANES), axis=-1, keepdims=True)
        vals.append(m)
        idxs.append(idx)
        logits = jnp.where(lane == idx, 2 * NEG, logits)
    ex = [jnp.exp(v - vals[0]) for v in vals]
    denom = functools.reduce(jnp.add, ex)
    tope = jnp.zeros(logits.shape, I32)
    topw = jnp.zeros(logits.shape, F32)
    for k in range(TOP_K):
        tope = jnp.where(lane == k, idxs[k], tope)
        topw = jnp.where(lane == k, ex[k] / denom, topw)
    tope_ref[...] = tope
    topw_ref[...] = topw

    @pl.when(pl.program_id(0) == 0)
    def _():
        cnt_ref[...] = jnp.zeros_like(cnt_ref)

    chosen = functools.reduce(jnp.logical_or, [lane == idx for idx in idxs])
    cnt_ref[0:1, :] += jnp.sum(chosen.astype(F32), axis=0, keepdims=True)


def _postmix(o_sb, o_ca, gate, x2, mod, g_post, g_ffn, wbs_b, wbc_b, wout_b, wr_pad, br_pad, seq):
    n, d = x2.shape
    w = o_sb.shape[1]
    tm = min(ROW_TILE, seq)
    per_b = seq // tm
    row = lambda i: (i, 0)
    fixed = lambda i: (0, 0)
    return pl.pallas_call(
        _postmix_kernel,
        grid=(n // tm,),
        in_specs=[pl.BlockSpec((tm, w), row), pl.BlockSpec((tm, w), row),
                  pl.BlockSpec((tm, 2 * d), row), pl.BlockSpec((tm, d), row),
                  pl.BlockSpec((1,) + mod.shape[1:], lambda i: (i // per_b, 0, 0)),
                  pl.BlockSpec((1, d), fixed), pl.BlockSpec((1, d), fixed),
                  pl.BlockSpec((w, d), fixed), pl.BlockSpec((w, d), fixed),
                  pl.BlockSpec((d, d), fixed), pl.BlockSpec((3 * d, LANES), fixed),
                  pl.BlockSpec((1, LANES), fixed)],
        out_specs=[pl.BlockSpec((tm, d), row), pl.BlockSpec((tm * (d // LANES), LANES), row),
                   pl.BlockSpec((tm, LANES), row), pl.BlockSpec((tm, LANES), row),
                   pl.BlockSpec((SUBLANES, LANES), fixed)],
        out_shape=[jax.ShapeDtypeStruct((n, d), F32), jax.ShapeDtypeStruct((n * (d // LANES), LANES), F32),
                   jax.ShapeDtypeStruct((n, LANES), I32), jax.ShapeDtypeStruct((n, LANES), F32),
                   jax.ShapeDtypeStruct((SUBLANES, LANES), F32)],
        compiler_params=_cparams(("arbitrary",)),
        name="postmix_router",
    )(o_sb, o_ca, gate, x2, mod, g_post.reshape(1, d), g_ffn.reshape(1, d), wbs_b, wbc_b, wout_b,
      wr_pad, br_pad)


def _rank_kernel(e_ref, l_ref, start_ref, dest_ref, run_ref):
    @pl.when(pl.program_id(0) == 0)
    def _():
        run_ref[...] = start_ref[...]

    e = e_ref[...]
    lane = lax.broadcasted_iota(I32, e.shape, 1)
    hots = [lane == e[:, k:k + 1] for k in range(TOP_K)]
    chosen = functools.reduce(jnp.logical_or, hots)
    before = _dot(l_ref[...], chosen.astype(BF16)) + run_ref[0:1, :]
    dest = jnp.zeros(e.shape, I32)
    for k in range(TOP_K):
        dk = jnp.sum(jnp.where(hots[k], before, 0.0), axis=-1, keepdims=True).astype(I32)
        dest = jnp.where(lane == k, dk, dest)
    dest_ref[...] = dest
    run_ref[0:1, :] += jnp.sum(chosen.astype(F32), axis=0, keepdims=True)


def _routing_ranks(top_e, group_start):
    n = top_e.shape[0]
    t = min(ROW_TILE, n)
    strict_lower = (jnp.arange(t)[:, None] > jnp.arange(t)[None, :]).astype(BF16)
    return pl.pallas_call(
        _rank_kernel,
        grid=(n // t,),
        in_specs=[pl.BlockSpec((t, LANES), lambda i: (i, 0)),
                  pl.BlockSpec((t, t), lambda i: (0, 0)),
                  pl.BlockSpec((SUBLANES, LANES), lambda i: (0, 0))],
        out_specs=pl.BlockSpec((t, LANES), lambda i: (i, 0)),
        out_shape=jax.ShapeDtypeStruct((n, LANES), I32),
        scratch_shapes=[pltpu.VMEM((SUBLANES, LANES), F32)],
        compiler_params=_cparams(("arbitrary",)),
        name="routing_ranks",
    )(top_e, strict_lower, group_start)


def _moe_kernel(be_ref, nvalid_ref, rowflat_ref, h2_hbm, w1_ref, b1_ref, w2_ref, b2_ref, y4_hbm,
                xbuf0, xbuf1, ybuf0, ybuf1, w1b, w2b, hid, sem_in, sem_out, *, n_tokens):
    i = pl.program_id(0)
    n_blocks = pl.num_programs(0)
    nv = nvalid_ref[0]
    n_sub = h2_hbm.shape[1]
    xbuf, ybuf = (xbuf0, xbuf1), (ybuf0, ybuf1)
    blk = xbuf0.shape[0] // n_sub
    d = n_sub * LANES
    di = w2b.shape[0]

    def tile(buf, r):
        return buf.at[pl.ds(r * n_sub, n_sub), :]

    def gather_copy(flat, r, slot):
        return pltpu.make_async_copy(h2_hbm.at[flat & (n_tokens - 1)], tile(xbuf[slot], r), sem_in.at[slot])

    def scatter_copy(flat, r, slot):
        return pltpu.make_async_copy(tile(ybuf[slot], r), y4_hbm.at[flat], sem_out.at[slot])

    def wait_gather(slot):
        pltpu.make_async_copy(xbuf[slot], xbuf[slot], sem_in.at[slot]).wait()

    def wait_scatter(slot):
        pltpu.make_async_copy(ybuf[slot], ybuf[slot], sem_out.at[slot]).wait()

    def step(slot):
        other = 1 - slot

        @pl.when(i == 0)
        def _():
            ybuf0[...] = jnp.zeros_like(ybuf0)
            ybuf1[...] = jnp.zeros_like(ybuf1)

            def clear(r, c):
                for s in range(2):
                    scatter_copy(n_tokens * TOP_K + s * blk + r, r, s).start()
                return c
            lax.fori_loop(0, blk, clear, 0)
            wait_scatter(0)
            wait_scatter(1)

            def first(r, c):
                gather_copy(rowflat_ref[r], r, slot).start()
                return c
            lax.fori_loop(0, blk, first, 0)

        wait_gather(slot)

        @pl.when(jnp.logical_or(i == 0, be_ref[i] != be_ref[jnp.maximum(i - 1, 0)]))
        def _():
            w1b[...] = w1_ref[0].astype(BF16)
            w2b[...] = w2_ref[0].astype(BF16)

        nxt = jnp.minimum(i + 1, n_blocks - 1) * blk
        prev = jnp.where(i == 0, n_blocks - 1, i - 1) * blk
        n_hid, n_out = di // MOE_COLS, d // MOE_COLS
        g_stages = max(n_hid // 2, 1)
        s_stages = max(n_hid - g_stages, 1)

        def issue(stage):
            if stage < g_stages:
                for r in range(stage * blk // g_stages, (stage + 1) * blk // g_stages):
                    gather_copy(rowflat_ref[nxt + r], r, other).start(priority=r % 2)
            if stage >= n_hid - s_stages:
                st = stage - (n_hid - s_stages)
                for r in range(st * blk // s_stages, (st + 1) * blk // s_stages):
                    scatter_copy(rowflat_ref[prev + r], r, other).start(priority=r % 2)

        xb = jnp.concatenate([xbuf[slot][pl.ds(s, blk, stride=n_sub), :] for s in range(n_sub)],
                             axis=1).astype(BF16)
        for c in range(n_hid):
            issue(c)
            c0 = c * MOE_COLS
            gate = _dot(xb, w1b[:, c0:c0 + MOE_COLS]) + b1_ref[0, :, c0:c0 + MOE_COLS]
            up = _dot(xb, w1b[:, di + c0:di + c0 + MOE_COLS]) + b1_ref[0, :, di + c0:di + c0 + MOE_COLS]
            gate = jnp.minimum(gate, SWIGLU_LIMIT)
            up = jnp.clip(up, -SWIGLU_LIMIT, SWIGLU_LIMIT)
            hid[:, c0:c0 + MOE_COLS] = (gate * jax.nn.sigmoid(SWIGLU_ALPHA * gate) * (up + 1.0)).astype(BF16)
        @pl.when(i >= 1)
        def _():
            wait_scatter(slot)

        hb = hid[...]
        for c in range(n_out):
            c0 = c * MOE_COLS
            y = _dot(hb, w2b[:, c0:c0 + MOE_COLS]) + b2_ref[0, :, c0:c0 + MOE_COLS]
            for s in range(MOE_COLS // LANES):
                ybuf[slot][pl.ds(c0 // LANES + s, blk, stride=n_sub), :] = y[:, s * LANES:(s + 1) * LANES]

        @pl.when(i == nv - 1)
        def _():
            def last(r, c):
                scatter_copy(rowflat_ref[i * blk + r], r, slot).start()
                return c
            lax.fori_loop(0, blk, last, 0)
            wait_scatter(other)
            wait_scatter(slot)
            wait_gather(other)

    @pl.when(jnp.logical_and(i < nv, i % 2 == 0))
    def _():
        step(0)

    @pl.when(jnp.logical_and(i < nv, i % 2 == 1))
    def _():
        step(1)


def _expert_ffn(h2_tiles, row_flat, block_e, n_valid, w1, b1, w2, b2):
    e, d, two_i = w1.shape
    n_sub = d // LANES
    n = h2_tiles.shape[0] // n_sub
    h2 = h2_tiles.reshape(n, n_sub, LANES)
    n_blocks = block_e.shape[0]
    by_expert = lambda i, be, nv, rf: (be[i], 0, 0)
    grid_spec = pltpu.PrefetchScalarGridSpec(
        num_scalar_prefetch=3,
        grid=(n_blocks,),
        in_specs=[pl.BlockSpec(memory_space=pl.ANY),
                  pl.BlockSpec((1, d, two_i), by_expert),
                  pl.BlockSpec((1, 1, two_i), by_expert),
                  pl.BlockSpec((1, two_i // 2, d), by_expert),
                  pl.BlockSpec((1, 1, d), by_expert)],
        out_specs=pl.BlockSpec(memory_space=pl.ANY),
        scratch_shapes=[pltpu.VMEM((MOE_BLOCK * n_sub, LANES), F32)] * 4 + [
                        pltpu.VMEM((d, two_i), BF16), pltpu.VMEM((two_i // 2, d), BF16),
                        pltpu.VMEM((MOE_BLOCK, two_i // 2), BF16),
                        pltpu.SemaphoreType.DMA((2,)), pltpu.SemaphoreType.DMA((2,))],
    )
    y4 = pl.pallas_call(
        functools.partial(_moe_kernel, n_tokens=n),
        grid_spec=grid_spec,
        out_shape=jax.ShapeDtypeStruct((n * TOP_K + 2 * MOE_BLOCK, n_sub, LANES), F32),
        compiler_params=_cparams(("arbitrary",)),
        name="expert_ffn",
    )(block_e, n_valid, row_flat, h2, w1, b1.reshape(e, 1, two_i), w2, b2.reshape(e, 1, d))
    return y4.reshape((n * TOP_K + 2 * MOE_BLOCK) * n_sub, LANES)


def _combine_kernel(*refs):
    y_refs, (w_ref, x1_ref, mod_ref, g_ref, o_ref) = refs[:TOP_K], refs[TOP_K:]
    tm, d = x1_ref.shape
    n_sub = d // LANES
    ff = jnp.concatenate(
        [functools.reduce(jnp.add, [y_refs[k][pl.ds(s, tm, stride=n_sub), :] * w_ref[:, k:k + 1]
                                    for k in range(TOP_K)])
         for s in range(n_sub)], axis=1)
    o_ref[...] = x1_ref[...] + mod_ref[0, 5:6, :] * _rms(ff, g_ref[...])


def _combine(y4, top_w, x1, mod, g_post, seq):
    n, d = x1.shape
    tm = min(ROW_TILE, seq)
    per_b = seq // tm
    n_sub = d // LANES
    y_specs = [pl.BlockSpec((tm * n_sub, LANES), functools.partial(lambda k, i: (k * (n // tm) + i, 0), k))
               for k in range(TOP_K)]
    return pl.pallas_call(
        _combine_kernel,
        grid=(n // tm,),
        in_specs=y_specs + [
                  pl.BlockSpec((tm, LANES), lambda i: (i, 0)),
                  pl.BlockSpec((tm, d), lambda i: (i, 0)),
                  pl.BlockSpec((1,) + mod.shape[1:], lambda i: (i // per_b, 0, 0)),
                  pl.BlockSpec((1, d), lambda i: (0, 0))],
        out_specs=pl.BlockSpec((tm, d), lambda i: (i, 0)),
        out_shape=jax.ShapeDtypeStruct((n, d), F32),
        compiler_params=_cparams(("arbitrary",)),
        name="combine",
    )(*([y4] * TOP_K), top_w, x1, mod, g_post.reshape(1, d))


def _layer(x, c, w_ada, b_ada, g_pre_mix, g_post_mix, w_in, rel_bias, w_branch_sb, w_branch_ca, w_gate,
           b_gate, w_out, g_pre_ffn, g_post_ffn, w_router, b_router, w_e1, b_e1, w_e2, b_e2):
    batch, seq, d = x.shape
    n = batch * seq
    sb_w = w_branch_sb.shape[0]
    ca_w = w_branch_ca.shape[0]
    n_exp = w_router.shape[1]
    x2 = x.reshape(n, d)

    mod = _modulation(c, w_ada, b_ada)

    scale = HEAD_DIM ** -0.5
    col_scale = np.ones((w_in.shape[1],), np.float32)
    col_scale[:sb_w] = scale
    col_scale[3 * sb_w:3 * sb_w + ca_w] = scale * LOG2_E
    w_in_b = (w_in * col_scale[None, :]).astype(BF16)
    proj, gate = _premix(x2, mod, g_pre_mix, w_in_b, w_gate.astype(BF16), b_gate, seq)

    sbl, cal = sb_w // LANES, ca_w // LANES
    o_sb = _sb_attention(proj, batch, seq, sb_w, 0, sbl, 2 * sbl)
    o_ca = _ca_attention(proj, _ca_tables(rel_bias), batch, seq, ca_w, 3 * sbl, 3 * sbl + cal, 3 * sbl + 2 * cal)

    wr_f32 = jnp.zeros((d, LANES), F32).at[:, :n_exp].set(w_router)
    wr_hi = wr_f32.astype(BF16)
    wr_lo = (wr_f32 - wr_hi.astype(F32)).astype(BF16)
    wr_pad = jnp.concatenate([wr_hi, wr_hi, wr_lo], axis=0)
    br_pad = jnp.full((1, LANES), NEG, F32).at[0, :n_exp].set(b_router)
    x1, h2, top_e, top_w, cnt = _postmix(o_sb, o_ca, gate, x2, mod, g_post_mix, g_pre_ffn,
                                         w_branch_sb.astype(BF16), w_branch_ca.astype(BF16),
                                         w_out.astype(BF16), wr_pad, br_pad, seq)

    nk = n * TOP_K
    counts = cnt[0, :n_exp].astype(I32)
    padded = (counts + MOE_BLOCK - 1) // MOE_BLOCK * MOE_BLOCK
    pad_end = jnp.cumsum(padded)
    pad_start = pad_end - padded
    group_start = jnp.zeros((SUBLANES, LANES), F32).at[0, :n_exp].set(pad_start.astype(F32))
    dest = _routing_ranks(top_e, group_start)[:, :TOP_K].reshape(nk)
    n_blocks = nk // MOE_BLOCK + n_exp
    rows = n_blocks * MOE_BLOCK
    spare = nk + np.arange(rows, dtype=np.int32) % (2 * MOE_BLOCK)
    assert n & (n - 1) == 0, "row ids are k * n + token with the token taken back by a bit mask"
    ids = np.arange(nk, dtype=np.int32)
    row_flat = jnp.asarray(spare).at[dest].set(jnp.asarray((ids % TOP_K) * n + ids // TOP_K))
    blk_start = jnp.arange(n_blocks, dtype=I32) * MOE_BLOCK
    block_e = jnp.minimum(jnp.sum((pad_end[None, :] <= blk_start[:, None]).astype(I32), axis=1), n_exp - 1)
    n_valid = (pad_end[-1] // MOE_BLOCK).astype(I32).reshape(1)

    y4 = _expert_ffn(h2, row_flat, block_e, n_valid, w_e1, b_e1, w_e2, b_e2)
    out = _combine(y4, top_w, x1, mod, g_post_ffn, seq)
    return out.reshape(batch, seq, d)


def kernel(x, c, w_ada, b_ada, g_pre_mix, g_post_mix, w_in, rel_bias, w_branch_sb, w_branch_ca, w_gate, b_gate,
           w_out, g_pre_ffn, g_post_ffn, w_router, b_router, w_e1, b_e1, w_e2, b_e2):
    for l in range(w_ada.shape[0]):
        x = _layer(x, c, w_ada[l], b_ada[l], g_pre_mix[l], g_post_mix[l], w_in[l], rel_bias[l],
                   w_branch_sb[l], w_branch_ca[l], w_gate[l], b_gate[l], w_out[l], g_pre_ffn[l],
                   g_post_ffn[l], w_router[l], b_router[l], w_e1[l], b_e1[l], w_e2[l], b_e2[l])
    return x
```

```python
import functools

import numpy as np
import jax
import jax.numpy as jnp
from jax import lax
from jax.experimental import pallas as pl
from jax.experimental.pallas import tpu as pltpu

F32 = jnp.float32
BF16 = jnp.bfloat16
I32 = jnp.int32

RMS_EPS = 1e-6
HEAD_DIM = 64
LANES = 128
SUBLANES = 8
CHUNK = 64
CA_LEFT_CHUNKS = 8
REL_CLIP = 256
TOP_K = 4
SWIGLU_ALPHA = 1.702
SWIGLU_LIMIT = 7.0
NEG = -1e30
LOG2_E = 1.4426950408889634

SB_TILE = 512
SB_KEY_TILE = 256
SB_ROWS = 128
CA_TILE = 4 * CHUNK
CA_KEY_TILES = CA_LEFT_CHUNKS * CHUNK // CA_TILE + 1
CA_WINDOW = CA_KEY_TILES * CA_TILE
ROW_TILE = 512
MOE_BLOCK = 512
MOE_COLS = 256
VMEM_LIMIT = 56 * 1024 * 1024


def _cparams(sem):
    return pltpu.CompilerParams(dimension_semantics=sem, vmem_limit_bytes=VMEM_LIMIT)


def _dot(a, b):
    return jnp.dot(a, b, preferred_element_type=F32)


def _dot_nt(a, b):
    return lax.dot_general(a, b, (((1,), (1,)), ((), ())), preferred_element_type=F32)


def _rms(x, g):
    return x * lax.rsqrt(jnp.mean(x * x, axis=-1, keepdims=True) + RMS_EPS) * g


def _mod_kernel(c_ref, w_ref, b_ref, o_ref):
    o_ref[...] = jnp.dot(c_ref[...], w_ref[...], preferred_element_type=F32,
                         precision=lax.Precision.HIGHEST) + b_ref[...]


def _modulation(c, w_ada, b_ada):
    b, d = c.shape
    n_out = w_ada.shape[1]
    c_pad = jnp.zeros((SUBLANES, d), F32).at[:b].set(c)
    out = pl.pallas_call(
        _mod_kernel,
        grid=(n_out // d,),
        in_specs=[pl.BlockSpec((SUBLANES, d), lambda j: (0, 0)),
                  pl.BlockSpec((d, d), lambda j: (0, j)),
                  pl.BlockSpec((1, d), lambda j: (0, j))],
        out_specs=pl.BlockSpec((SUBLANES, d), lambda j: (0, j)),
        out_shape=jax.ShapeDtypeStruct((SUBLANES, n_out), F32),
        compiler_params=_cparams(("arbitrary",)),
        name="adaln_mod",
    )(c_pad, w_ada, b_ada.reshape(1, n_out))
    return out[:b].reshape(b, n_out // d, d)


def _premix_kernel(x_ref, mod_ref, g_ref, win_ref, wg_ref, bg_ref, proj_ref, gate_ref, *, col):
    h = _rms(x_ref[...], g_ref[...]) * (1.0 + mod_ref[0, 1:2, :]) + mod_ref[0, 0:1, :]
    hb = h.astype(BF16)
    for c0 in range(0, proj_ref.shape[1], col):
        proj_ref[:, c0:c0 + col] = _dot(hb, win_ref[:, c0:c0 + col]).astype(BF16)
    for c0 in range(0, gate_ref.shape[1], col):
        g = jax.nn.sigmoid(_dot(hb, wg_ref[:, c0:c0 + col]) + bg_ref[:, c0:c0 + col])
        gate_ref[:, c0:c0 + col] = g.astype(BF16)


def _premix(x2, mod, g_pre, w_in_b, w_gate_b, b_gate, seq):
    n, d = x2.shape
    pw, gw = w_in_b.shape[1], w_gate_b.shape[1]
    tm = min(ROW_TILE, seq)
    per_b = seq // tm
    return pl.pallas_call(
        functools.partial(_premix_kernel, col=512),
        grid=(n // tm,),
        in_specs=[pl.BlockSpec((tm, d), lambda i: (i, 0)),
                  pl.BlockSpec((1,) + mod.shape[1:], lambda i: (i // per_b, 0, 0)),
                  pl.BlockSpec((1, d), lambda i: (0, 0)),
                  pl.BlockSpec((d, pw), lambda i: (0, 0)),
                  pl.BlockSpec((d, gw), lambda i: (0, 0)),
                  pl.BlockSpec((1, gw), lambda i: (0, 0))],
        out_specs=[pl.BlockSpec((tm, pw), lambda i: (i, 0)),
                   pl.BlockSpec((tm, gw), lambda i: (i, 0))],
        out_shape=[jax.ShapeDtypeStruct((n, pw), BF16), jax.ShapeDtypeStruct((n, gw), BF16)],
        compiler_params=_cparams(("arbitrary",)),
        name="premix",
    )(x2, mod, g_pre.reshape(1, d), w_in_b, w_gate_b, b_gate.reshape(1, gw))


def _sb_kernel(q_ref, k_ref, v_ref, u_ref, o_ref, z_scr, cum_scr, carry_scr, acc_scr):
    i = pl.program_id(2)
    t = q_ref.shape[0]
    tk = u_ref.shape[1]
    per_q = t // tk
    rows = SB_ROWS
    q2 = q_ref[...]
    lane = lax.broadcasted_iota(I32, (t, LANES), 1)
    u2 = u_ref[...]
    chains = [(hh, r0) for hh in range(2) for r0 in range(0, t, rows)]
    qs = []
    for hh, r0 in chains:
        in_head = (lane < HEAD_DIM) if hh == 0 else (lane >= HEAD_DIM)
        qs.append(jnp.where(in_head, q2, jnp.zeros_like(q2))[r0:r0 + rows])

    def causal(j_local):
        return [lax.broadcasted_iota(I32, (rows, tk), 1) + j_local * tk
                < lax.broadcasted_iota(I32, (rows, tk), 0) + r0 for _, r0 in chains]

    n_ch = len(chains)

    def logits(j, slot):
        start = pl.multiple_of(j * tk, tk)
        kt = k_ref[pl.ds(start, tk), :]
        for c, qh in enumerate(qs):
            z_scr[slot, c] = _dot_nt(qh, kt)

    def sums(slot, mask):
        for c in range(n_ch):
            z = z_scr[slot, c].astype(BF16)
            sp = jnp.maximum(z, 0) + jnp.log(1 + jnp.exp(-jnp.abs(z)))
            if mask is not None:
                sp = jnp.where(mask[c], sp, jnp.zeros_like(sp))
            cum_scr[slot, c] = _dot(sp, u2)

    def finish(j, slot, mask):
        start = pl.multiple_of(j * tk, tk)
        vt = v_ref[pl.ds(start, tk), :]
        for c in range(n_ch):
            cum = cum_scr[slot, c]
            carry = carry_scr[c]
            w = jnp.exp(z_scr[slot, c] - cum - jnp.concatenate([carry] * (tk // LANES), axis=1))
            if mask is not None:
                w = jnp.where(mask[c], w, 0.0)
            acc_scr[c] += _dot(w.astype(BF16), vt)
            carry_scr[c] = carry + jnp.broadcast_to(cum[:, 0:1], carry.shape)

    def sweep_tile(j, slot, mask, next_mask):
        logits(jnp.maximum(j - 1, 0), 1 - slot)
        finish(j, slot, mask)
        sums(1 - slot, next_mask)

    carry_scr[...] = jnp.zeros_like(carry_scr)
    acc_scr[...] = jnp.zeros_like(acc_scr)
    first = i * per_q
    logits(first + per_q - 1, (per_q - 1) % 2)
    sums((per_q - 1) % 2, causal(per_q - 1))
    for jl in range(per_q - 1, -1, -1):
        sweep_tile(first + jl, jl % 2, causal(jl), causal(jl - 1) if jl > 0 else None)

    def body(n, carry):
        j = first - 1 - 2 * n
        sweep_tile(j, 1, None, None)
        sweep_tile(j - 1, 0, None, None)
        return carry

    lax.fori_loop(0, first // 2, body, 0)
    accs = [jnp.concatenate([acc_scr[c] for c, (h2, _) in enumerate(chains) if h2 == hh], axis=0)
            for hh in range(2)]
    o_ref[...] = jnp.where(lane < HEAD_DIM, accs[0], accs[1]).astype(o_ref.dtype)


def _sb_attention(proj, batch, seq, width, q_col, k_col, v_col):
    n = proj.shape[0]
    t = min(SB_TILE, seq)
    tk = min(SB_KEY_TILE, t)
    assert (t // tk) % 2 == 0, "the sweep alternates two staging slots per key tile"
    nq = seq // t
    pairs = width // LANES
    n_chains = 2 * (t // SB_ROWS)
    tri = jnp.asarray((np.arange(tk)[:, None] >= np.arange(tk)[None, :]).astype(np.float32)).astype(BF16)
    return pl.pallas_call(
        _sb_kernel,
        grid=(batch, pairs, nq),
        in_specs=[pl.BlockSpec((t, LANES), lambda b, p, i: (b * nq + i, q_col + p)),
                  pl.BlockSpec((seq, LANES), lambda b, p, i: (b, k_col + p)),
                  pl.BlockSpec((seq, LANES), lambda b, p, i: (b, v_col + p)),
                  pl.BlockSpec((tk, tk), lambda b, p, i: (0, 0))],
        out_specs=pl.BlockSpec((t, LANES), lambda b, p, i: (b * nq + i, p)),
        out_shape=jax.ShapeDtypeStruct((n, width), BF16),
        scratch_shapes=[pltpu.VMEM((2, n_chains, SB_ROWS, tk), F32), pltpu.VMEM((2, n_chains, SB_ROWS, tk), F32),
                        pltpu.VMEM((n_chains, SB_ROWS, LANES), F32), pltpu.VMEM((n_chains, SB_ROWS, LANES), F32)],
        compiler_params=_cparams(("arbitrary", "arbitrary", "arbitrary")),
        name="stickbreak_attn",
    )(proj, proj, proj, tri)


def _ca_kernel(q_ref, k_ref, v_ref, t_ref, o_ref):
    r = pl.program_id(2)
    t = q_ref.shape[0]
    window = t_ref.shape[-1]
    start = pl.multiple_of(jnp.maximum(r - (CA_KEY_TILES - 1), 0) * t, t)
    kw = k_ref[pl.ds(start, window), :]
    vw = v_ref[pl.ds(start, window), :]
    q2 = q_ref[...]
    lane = lax.broadcasted_iota(I32, (t, LANES), 1)
    chains = [(hh, r0) for hh in range(2) for r0 in range(0, t, CHUNK)]
    zs = []
    for hh, r0 in chains:
        in_head = (lane < HEAD_DIM) if hh == 0 else (lane >= HEAD_DIM)
        qh = jnp.where(in_head, q2, jnp.zeros_like(q2))[r0:r0 + CHUNK]
        zs.append(_dot_nt(qh, kw) + t_ref[0, hh, r0:r0 + CHUNK, :])
    ps = [jnp.exp2(z - jnp.max(z, axis=-1, keepdims=True)) for z in zs]
    outs = [_dot(p.astype(BF16), vw) / jnp.sum(p, axis=-1, keepdims=True) for p in ps]
    heads = [jnp.concatenate([o for (h2, _), o in zip(chains, outs) if h2 == hh], axis=0) for hh in range(2)]
    o_ref[...] = jnp.where(lane < HEAD_DIM, heads[0], heads[1]).astype(o_ref.dtype)


def _ca_tables(rel_bias):
    h = rel_bias.shape[0]
    left = CA_LEFT_CHUNKS * CHUNK
    period = CA_WINDOW + CA_TILE
    u = np.arange(period)
    rel = np.where(u < CA_WINDOW, left - u, left + (period - u))
    v = rel_bias[:, np.clip(rel, -REL_CLIP, REL_CLIP) + REL_CLIP].astype(F32) * LOG2_E
    flat = jnp.tile(v, (1, CA_TILE))[:, :CA_TILE * (period - 1)]
    bias = flat.reshape(h, CA_TILE, period - 1)[:, :, :CA_WINDOW]
    i = np.arange(CA_TILE)[:, None]
    j = np.arange(CA_WINDOW)[None, :]
    first = (i // CHUNK) * CHUNK
    visible = (j >= first) & (j < first + (CA_LEFT_CHUNKS + 1) * CHUNK)
    full = jnp.where(visible[None], bias, NEG)
    tables = []
    for r in range(CA_KEY_TILES):
        shift = (CA_KEY_TILES - 1 - r) * CA_TILE
        tables.append(jnp.pad(full[:, :, shift:], ((0, 0), (0, 0), (0, shift)), constant_values=NEG))
    return jnp.stack(tables)


def _ca_attention(proj, tables, batch, seq, width, q_col, k_col, v_col):
    n = proj.shape[0]
    t = CA_TILE
    nq = seq // t
    pairs = width // LANES
    assert seq >= CA_WINDOW
    return pl.pallas_call(
        _ca_kernel,
        grid=(batch, pairs, nq),
        in_specs=[pl.BlockSpec((t, LANES), lambda b, p, i: (b * nq + i, q_col + p)),
                  pl.BlockSpec((seq, LANES), lambda b, p, i: (b, k_col + p)),
                  pl.BlockSpec((seq, LANES), lambda b, p, i: (b, v_col + p)),
                  pl.BlockSpec((1, 2, t, CA_WINDOW),
                               lambda b, p, i: (jnp.minimum(i, CA_KEY_TILES - 1), p, 0, 0))],
        out_specs=pl.BlockSpec((t, LANES), lambda b, p, i: (b * nq + i, p)),
        out_shape=jax.ShapeDtypeStruct((n, width), BF16),
        compiler_params=_cparams(("arbitrary", "arbitrary", "arbitrary")),
        name="chunkband_attn",
    )(proj, proj, proj, tables)


def _postmix_kernel(osb_ref, oca_ref, gate_ref, x_ref, mod_ref, gpost_ref, gffn_ref, wbs_ref, wbc_ref,
                    wout_ref, wr_ref, br_ref, x1_ref, h2_ref, tope_ref, topw_ref, cnt_ref):
    d = x_ref.shape[1]
    y_sb = _dot(osb_ref[...], wbs_ref[...])
    y_ca = _dot(oca_ref[...], wbc_ref[...])
    mixed_in = gate_ref[:, :d].astype(F32) * y_sb + gate_ref[:, d:].astype(F32) * y_ca
    mixed = _dot(mixed_in.astype(BF16), wout_ref[...])
    x1 = x_ref[...] + mod_ref[0, 2:3, :] * _rms(mixed, gpost_ref[...])
    x1_ref[...] = x1
    h2 = _rms(x1, gffn_ref[...]) * (1.0 + mod_ref[0, 4:5, :]) + mod_ref[0, 3:4, :]
    n_sub = d // LANES
    for s in range(n_sub):
        h2_ref[pl.ds(s, h2.shape[0], stride=n_sub), :] = h2[:, s * LANES:(s + 1) * LANES]
    h_hi = h2.astype(BF16)
    h_lo = (h2 - h_hi.astype(F32)).astype(BF16)
    logits = _dot(jnp.concatenate([h_hi, h_lo, h_hi], axis=1), wr_ref[...]) + br_ref[...]
    lane = lax.broadcasted_iota(I32, logits.shape, 1)
    vals, idxs = [], []
    for _ in range(TOP_K):
        m = jnp.max(logits, axis=-1, keepdims=True)
        idx = jnp.min(jnp.where(logits == m, lane, LANES), axis=-1, keepdims=True)
        vals.append(m)
        idxs.append(idx)
        logits = jnp.where(lane == idx, 2 * NEG, logits)
    ex = [jnp.exp(v - vals[0]) for v in vals]
    denom = functools.reduce(jnp.add, ex)
    tope = jnp.zeros(logits.shape, I32)
    topw = jnp.zeros(logits.shape, F32)
    for k in range(TOP_K):
        tope = jnp.where(lane == k, idxs[k], tope)
        topw = jnp.where(lane == k, ex[k] / denom, topw)
    tope_ref[...] = tope
    topw_ref[...] = topw

    @pl.when(pl.program_id(0) == 0)
    def _():
        cnt_ref[...] = jnp.zeros_like(cnt_ref)

    chosen = functools.reduce(jnp.logical_or, [lane == idx for idx in idxs])
    cnt_ref[0:1, :] += jnp.sum(chosen.astype(F32), axis=0, keepdims=True)


def _postmix(o_sb, o_ca, gate, x2, mod, g_post, g_ffn, wbs_b, wbc_b, wout_b, wr_pad, br_pad, seq):
    n, d = x2.shape
    w = o_sb.shape[1]
    tm = min(ROW_TILE, seq)
    per_b = seq // tm
    row = lambda i: (i, 0)
    fixed = lambda i: (0, 0)
    return pl.pallas_call(
        _postmix_kernel,
        grid=(n // tm,),
        in_specs=[pl.BlockSpec((tm, w), row), pl.BlockSpec((tm, w), row),
                  pl.BlockSpec((tm, 2 * d), row), pl.BlockSpec((tm, d), row),
                  pl.BlockSpec((1,) + mod.shape[1:], lambda i: (i // per_b, 0, 0)),
                  pl.BlockSpec((1, d), fixed), pl.BlockSpec((1, d), fixed),
                  pl.BlockSpec((w, d), fixed), pl.BlockSpec((w, d), fixed),
                  pl.BlockSpec((d, d), fixed), pl.BlockSpec((3 * d, LANES), fixed),
                  pl.BlockSpec((1, LANES), fixed)],
        out_specs=[pl.BlockSpec((tm, d), row), pl.BlockSpec((tm * (d // LANES), LANES), row),
                   pl.BlockSpec((tm, LANES), row), pl.BlockSpec((tm, LANES), row),
                   pl.BlockSpec((SUBLANES, LANES), fixed)],
        out_shape=[jax.ShapeDtypeStruct((n, d), F32), jax.ShapeDtypeStruct((n * (d // LANES), LANES), F32),
                   jax.ShapeDtypeStruct((n, LANES), I32), jax.ShapeDtypeStruct((n, LANES), F32),
                   jax.ShapeDtypeStruct((SUBLANES, LANES), F32)],
        compiler_params=_cparams(("arbitrary",)),
        name="postmix_router",
    )(o_sb, o_ca, gate, x2, mod, g_post.reshape(1, d), g_ffn.reshape(1, d), wbs_b, wbc_b, wout_b,
      wr_pad, br_pad)


def _rank_kernel(e_ref, l_ref, start_ref, dest_ref, run_ref):
    @pl.when(pl.program_id(0) == 0)
    def _():
        run_ref[...] = start_ref[...]

    e = e_ref[...]
    lane = lax.broadcasted_iota(I32, e.shape, 1)
    hots = [lane == e[:, k:k + 1] for k in range(TOP_K)]
    chosen = functools.reduce(jnp.logical_or, hots)
    before = _dot(l_ref[...], chosen.astype(BF16)) + run_ref[0:1, :]
    dest = jnp.zeros(e.shape, I32)
    for k in range(TOP_K):
        dk = jnp.sum(jnp.where(hots[k], before, 0.0), axis=-1, keepdims=True).astype(I32)
        dest = jnp.where(lane == k, dk, dest)
    dest_ref[...] = dest
    run_ref[0:1, :] += jnp.sum(chosen.astype(F32), axis=0, keepdims=True)


def _routing_ranks(top_e, group_start):
    n = top_e.shape[0]
    t = min(ROW_TILE, n)
    strict_lower = (jnp.arange(t)[:, None] > jnp.arange(t)[None, :]).astype(BF16)
    return pl.pallas_call(
        _rank_kernel,
        grid=(n // t,),
        in_specs=[pl.BlockSpec((t, LANES), lambda i: (i, 0)),
                  pl.BlockSpec((t, t), lambda i: (0, 0)),
                  pl.BlockSpec((SUBLANES, LANES), lambda i: (0, 0))],
        out_specs=pl.BlockSpec((t, LANES), lambda i: (i, 0)),
        out_shape=jax.ShapeDtypeStruct((n, LANES), I32),
        scratch_shapes=[pltpu.VMEM((SUBLANES, LANES), F32)],
        compiler_params=_cparams(("arbitrary",)),
        name="routing_ranks",
    )(top_e, strict_lower, group_start)


def _moe_kernel(be_ref, nvalid_ref, padlo_ref, padhi_ref, dest_ref, h2_hbm, w1_ref, b1_ref, w2_ref, b2_ref, y4_hbm,
                xbuf0, xbuf1, ybuf0, ybuf1, w1b, w2b, hid, sem_in, sem_out, rowflat_ref, *, n_tokens):
    i = pl.program_id(0)
    n_blocks = pl.num_programs(0)
    nv = nvalid_ref[0]
    n_sub = h2_hbm.shape[1]
    xbuf, ybuf = (xbuf0, xbuf1), (ybuf0, ybuf1)
    blk = xbuf0.shape[0] // n_sub
    d = n_sub * LANES
    di = w2b.shape[0]

    def tile(buf, r):
        return buf.at[pl.ds(r * n_sub, n_sub), :]

    def gather_copy(flat, r, slot):
        return pltpu.make_async_copy(h2_hbm.at[flat & (n_tokens - 1)], tile(xbuf[slot], r), sem_in.at[slot])

    def scatter_copy(flat, r, slot):
        return pltpu.make_async_copy(tile(ybuf[slot], r), y4_hbm.at[flat], sem_out.at[slot])

    def wait_gather(slot):
        pltpu.make_async_copy(xbuf[slot], xbuf[slot], sem_in.at[slot]).wait()

    def wait_scatter(slot):
        pltpu.make_async_copy(ybuf[slot], ybuf[slot], sem_out.at[slot]).wait()

    def step(slot):
        other = 1 - slot

        @pl.when(i == 0)
        def _():
            ybuf0[...] = jnp.zeros_like(ybuf0)
            ybuf1[...] = jnp.zeros_like(ybuf1)

            def clear(r, c):
                for s in range(2):
                    scatter_copy(n_tokens * TOP_K + s * blk + r, r, s).start()
                return c
            lax.fori_loop(0, blk, clear, 0)
            wait_scatter(0)
            wait_scatter(1)

            def first(r, c):
                gather_copy(rowflat_ref[r], r, slot).start()
                return c
            lax.fori_loop(0, blk, first, 0)

        wait_gather(slot)

        @pl.when(jnp.logical_or(i == 0, be_ref[i] != be_ref[jnp.maximum(i - 1, 0)]))
        def _():
            w1b[...] = w1_ref[0].astype(BF16)
            w2b[...] = w2_ref[0].astype(BF16)

        nxt = jnp.minimum(i + 1, n_blocks - 1) * blk
        prev = jnp.where(i == 0, n_blocks - 1, i - 1) * blk
        n_hid, n_out = di // MOE_COLS, d // MOE_COLS
        g_stages = max(n_hid // 2, 1)
        s_stages = max(n_hid - g_stages, 1)

        def issue(stage):
            if stage < g_stages:
                for r in range(stage * blk // g_stages, (stage + 1) * blk // g_stages):
                    gather_copy(rowflat_ref[nxt + r], r, other).start(priority=r % 2)
            if stage >= n_hid - s_stages:
                st = stage - (n_hid - s_stages)
                for r in range(st * blk // s_stages, (st + 1) * blk // s_stages):
                    scatter_copy(rowflat_ref[prev + r], r, other).start(priority=r % 2)

        xb = jnp.concatenate([xbuf[slot][pl.ds(s, blk, stride=n_sub), :] for s in range(n_sub)],
                             axis=1).astype(BF16)
        for c in range(n_hid):
            issue(c)
            c0 = c * MOE_COLS
            gate = _dot(xb, w1b[:, c0:c0 + MOE_COLS]) + b1_ref[0, :, c0:c0 + MOE_COLS]
            up = _dot(xb, w1b[:, di + c0:di + c0 + MOE_COLS]) + b1_ref[0, :, di + c0:di + c0 + MOE_COLS]
            gate = jnp.minimum(gate, SWIGLU_LIMIT)
            up = jnp.clip(up, -SWIGLU_LIMIT, SWIGLU_LIMIT)
            hid[:, c0:c0 + MOE_COLS] = (gate * jax.nn.sigmoid(SWIGLU_ALPHA * gate) * (up + 1.0)).astype(BF16)
        @pl.when(i >= 1)
        def _():
            wait_scatter(slot)

        hb = hid[...]
        for c in range(n_out):
            c0 = c * MOE_COLS
            y = _dot(hb, w2b[:, c0:c0 + MOE_COLS]) + b2_ref[0, :, c0:c0 + MOE_COLS]
            for s in range(MOE_COLS // LANES):
                ybuf[slot][pl.ds(c0 // LANES + s, blk, stride=n_sub), :] = y[:, s * LANES:(s + 1) * LANES]

        @pl.when(i == nv - 1)
        def _():
            def last(r, c):
                scatter_copy(rowflat_ref[i * blk + r], r, slot).start()
                return c
            lax.fori_loop(0, blk, last, 0)
            wait_scatter(other)
            wait_scatter(slot)
            wait_gather(other)

    @pl.when(i == 0)
    def _():
        def pad_range(g, c):
            def pad(r, c2):
                rowflat_ref[r] = n_tokens * TOP_K + (r & (2 * blk - 1))
                return c2
            return lax.fori_loop(padlo_ref[g], padhi_ref[g], pad, c)
        lax.fori_loop(0, padlo_ref.shape[0], pad_range, 0)

        def place(t, c):
            for k in range(TOP_K):
                rowflat_ref[dest_ref[t * TOP_K + k]] = k * n_tokens + t
            return c
        lax.fori_loop(0, n_tokens, place, 0, unroll=8)

    @pl.when(jnp.logical_and(i < nv, i % 2 == 0))
    def _():
        step(0)

    @pl.when(jnp.logical_and(i < nv, i % 2 == 1))
    def _():
        step(1)


def _expert_ffn(h2_tiles, dest, block_e, n_valid, pad_lo, pad_hi, w1, b1, w2, b2):
    e, d, two_i = w1.shape
    n_sub = d // LANES
    n = h2_tiles.shape[0] // n_sub
    h2 = h2_tiles.reshape(n, n_sub, LANES)
    n_blocks = block_e.shape[0]
    by_expert = lambda i, be, *_: (be[i], 0, 0)
    grid_spec = pltpu.PrefetchScalarGridSpec(
        num_scalar_prefetch=5,
        grid=(n_blocks,),
        in_specs=[pl.BlockSpec(memory_space=pl.ANY),
                  pl.BlockSpec((1, d, two_i), by_expert),
                  pl.BlockSpec((1, 1, two_i), by_expert),
                  pl.BlockSpec((1, two_i // 2, d), by_expert),
                  pl.BlockSpec((1, 1, d), by_expert)],
        out_specs=pl.BlockSpec(memory_space=pl.ANY),
        scratch_shapes=[pltpu.VMEM((MOE_BLOCK * n_sub, LANES), F32)] * 4 + [
                        pltpu.VMEM((d, two_i), BF16), pltpu.VMEM((two_i // 2, d), BF16),
                        pltpu.VMEM((MOE_BLOCK, two_i // 2), BF16),
                        pltpu.SemaphoreType.DMA((2,)), pltpu.SemaphoreType.DMA((2,)),
                        pltpu.SMEM((n_blocks * MOE_BLOCK,), I32)],
    )
    y4 = pl.pallas_call(
        functools.partial(_moe_kernel, n_tokens=n),
        grid_spec=grid_spec,
        out_shape=jax.ShapeDtypeStruct((n * TOP_K + 2 * MOE_BLOCK, n_sub, LANES), F32),
        compiler_params=_cparams(("arbitrary",)),
        name="expert_ffn",
    )(block_e, n_valid, pad_lo, pad_hi, dest, h2, w1, b1.reshape(e, 1, two_i), w2, b2.reshape(e, 1, d))
    return y4.reshape((n * TOP_K + 2 * MOE_BLOCK) * n_sub, LANES)


def _combine_kernel(*refs):
    y_refs, (w_ref, x1_ref, mod_ref, g_ref, o_ref) = refs[:TOP_K], refs[TOP_K:]
    tm, d = x1_ref.shape
    n_sub = d // LANES
    ff = jnp.concatenate(
        [functools.reduce(jnp.add, [y_refs[k][pl.ds(s, tm, stride=n_sub), :] * w_ref[:, k:k + 1]
                                    for k in range(TOP_K)])
         for s in range(n_sub)], axis=1)
    o_ref[...] = x1_ref[...] + mod_ref[0, 5:6, :] * _rms(ff, g_ref[...])


def _combine(y4, top_w, x1, mod, g_post, seq):
    n, d = x1.shape
    tm = min(ROW_TILE, seq)
    per_b = seq // tm
    n_sub = d // LANES
    y_specs = [pl.BlockSpec((tm * n_sub, LANES), functools.partial(lambda k, i: (k * (n // tm) + i, 0), k))
               for k in range(TOP_K)]
    return pl.pallas_call(
        _combine_kernel,
        grid=(n // tm,),
        in_specs=y_specs + [
                  pl.BlockSpec((tm, LANES), lambda i: (i, 0)),
                  pl.BlockSpec((tm, d), lambda i: (i, 0)),
                  pl.BlockSpec((1,) + mod.shape[1:], lambda i: (i // per_b, 0, 0)),
                  pl.BlockSpec((1, d), lambda i: (0, 0))],
        out_specs=pl.BlockSpec((tm, d), lambda i: (i, 0)),
        out_shape=jax.ShapeDtypeStruct((n, d), F32),
        compiler_params=_cparams(("arbitrary",)),
        name="combine",
    )(*([y4] * TOP_K), top_w, x1, mod, g_post.reshape(1, d))


def _layer(x, c, w_ada, b_ada, g_pre_mix, g_post_mix, w_in, rel_bias, w_branch_sb, w_branch_ca, w_gate,
           b_gate, w_out, g_pre_ffn, g_post_ffn, w_router, b_router, w_e1, b_e1, w_e2, b_e2):
    batch, seq, d = x.shape
    n = batch * seq
    sb_w = w_branch_sb.shape[0]
    ca_w = w_branch_ca.shape[0]
    n_exp = w_router.shape[1]
    x2 = x.reshape(n, d)

    mod = _modulation(c, w_ada, b_ada)

    scale = HEAD_DIM ** -0.5
    col_scale = np.ones((w_in.shape[1],), np.float32)
    col_scale[:sb_w] = scale
    col_scale[3 * sb_w:3 * sb_w + ca_w] = scale * LOG2_E
    w_in_b = (w_in * col_scale[None, :]).astype(BF16)
    proj, gate = _premix(x2, mod, g_pre_mix, w_in_b, w_gate.astype(BF16), b_gate, seq)

    sbl, cal = sb_w // LANES, ca_w // LANES
    o_sb = _sb_attention(proj, batch, seq, sb_w, 0, sbl, 2 * sbl)
    o_ca = _ca_attention(proj, _ca_tables(rel_bias), batch, seq, ca_w, 3 * sbl, 3 * sbl + cal, 3 * sbl + 2 * cal)

    wr_f32 = jnp.zeros((d, LANES), F32).at[:, :n_exp].set(w_router)
    wr_hi = wr_f32.astype(BF16)
    wr_lo = (wr_f32 - wr_hi.astype(F32)).astype(BF16)
    wr_pad = jnp.concatenate([wr_hi, wr_hi, wr_lo], axis=0)
    br_pad = jnp.full((1, LANES), NEG, F32).at[0, :n_exp].set(b_router)
    x1, h2, top_e, top_w, cnt = _postmix(o_sb, o_ca, gate, x2, mod, g_post_mix, g_pre_ffn,
                                         w_branch_sb.astype(BF16), w_branch_ca.astype(BF16),
                                         w_out.astype(BF16), wr_pad, br_pad, seq)

    nk = n * TOP_K
    counts = cnt[0, :n_exp].astype(I32)
    padded = (counts + MOE_BLOCK - 1) // MOE_BLOCK * MOE_BLOCK
    pad_end = jnp.cumsum(padded)
    pad_start = pad_end - padded
    group_start = jnp.zeros((SUBLANES, LANES), F32).at[0, :n_exp].set(pad_start.astype(F32))
    dest = _routing_ranks(top_e, group_start)[:, :TOP_K].reshape(nk)
    n_blocks = nk // MOE_BLOCK + n_exp
    assert n & (n - 1) == 0, "row ids are k * n + token with the token taken back by a bit mask"
    blk_start = jnp.arange(n_blocks, dtype=I32) * MOE_BLOCK
    block_e = jnp.minimum(jnp.sum((pad_end[None, :] <= blk_start[:, None]).astype(I32), axis=1), n_exp - 1)
    n_valid = (pad_end[-1] // MOE_BLOCK).astype(I32).reshape(1)

    rows_end = jnp.full((1,), n_blocks * MOE_BLOCK, I32)
    pad_lo = jnp.concatenate([pad_start + counts, pad_end[-1:], rows_end - MOE_BLOCK]).astype(I32)
    pad_hi = jnp.concatenate([pad_end, jnp.minimum(pad_end[-1:] + MOE_BLOCK, rows_end), rows_end]).astype(I32)
    y4 = _expert_ffn(h2, dest, block_e, n_valid, pad_lo, pad_hi, w_e1, b_e1, w_e2, b_e2)
    out = _combine(y4, top_w, x1, mod, g_post_ffn, seq)
    return out.reshape(batch, seq, d)


def kernel(x, c, w_ada, b_ada, g_pre_mix, g_post_mix, w_in, rel_bias, w_branch_sb, w_branch_ca, w_gate, b_gate,
           w_out, g_pre_ffn, g_post_ffn, w_router, b_router, w_e1, b_e1, w_e2, b_e2):
    for l in range(w_ada.shape[0]):
        x = _layer(x, c, w_ada[l], b_ada[l], g_pre_mix[l], g_post_mix[l], w_in[l], rel_bias[l],
                   w_branch_sb[l], w_branch_ca[l], w_gate[l], b_gate[l], w_out[l], g_pre_ffn[l],
                   g_post_ffn[l], w_router[l], b_router[l], w_e1[l], b_e1[l], w_e2[l], b_e2[l])
    return x
```

```python
import functools

import numpy as np
import jax
import jax.numpy as jnp
from jax import lax
from jax.experimental import pallas as pl
from jax.experimental.pallas import tpu as pltpu

F32 = jnp.float32
BF16 = jnp.bfloat16
I32 = jnp.int32

RMS_EPS = 1e-6
HEAD_DIM = 64
LANES = 128
SUBLANES = 8
CHUNK = 64
CA_LEFT_CHUNKS = 8
REL_CLIP = 256
TOP_K = 4
SWIGLU_ALPHA = 1.702
SWIGLU_LIMIT = 7.0
NEG = -1e30
LOG2_E = 1.4426950408889634

SB_TILE = 512
SB_KEY_TILE = 256
SB_ROWS = 128
CA_TILE = 4 * CHUNK
CA_KEY_TILES = CA_LEFT_CHUNKS * CHUNK // CA_TILE + 1
CA_WINDOW = CA_KEY_TILES * CA_TILE
ROW_TILE = 512
MOE_BLOCK = 512
MOE_COLS = 256
VMEM_LIMIT = 56 * 1024 * 1024


def _cparams(sem):
    return pltpu.CompilerParams(dimension_semantics=sem, vmem_limit_bytes=VMEM_LIMIT)


def _dot(a, b):
    return jnp.dot(a, b, preferred_element_type=F32)


def _dot_nt(a, b):
    return lax.dot_general(a, b, (((1,), (1,)), ((), ())), preferred_element_type=F32)


def _rms(x, g):
    return x * lax.rsqrt(jnp.mean(x * x, axis=-1, keepdims=True) + RMS_EPS) * g


def _mod_kernel(c_ref, w_ref, b_ref, o_ref):
    o_ref[...] = jnp.dot(c_ref[...], w_ref[...], preferred_element_type=F32,
                         precision=lax.Precision.HIGHEST) + b_ref[...]


def _modulation(c, w_ada, b_ada):
    b, d = c.shape
    n_out = w_ada.shape[1]
    c_pad = jnp.zeros((SUBLANES, d), F32).at[:b].set(c)
    out = pl.pallas_call(
        _mod_kernel,
        grid=(n_out // d,),
        in_specs=[pl.BlockSpec((SUBLANES, d), lambda j: (0, 0)),
                  pl.BlockSpec((d, d), lambda j: (0, j)),
                  pl.BlockSpec((1, d), lambda j: (0, j))],
        out_specs=pl.BlockSpec((SUBLANES, d), lambda j: (0, j)),
        out_shape=jax.ShapeDtypeStruct((SUBLANES, n_out), F32),
        compiler_params=_cparams(("arbitrary",)),
        name="adaln_mod",
    )(c_pad, w_ada, b_ada.reshape(1, n_out))
    return out[:b].reshape(b, n_out // d, d)


def _premix_kernel(x_ref, mod_ref, g_ref, win_ref, wg_ref, bg_ref, proj_ref, gate_ref, *, col):
    h = _rms(x_ref[...], g_ref[...]) * (1.0 + mod_ref[0, 1:2, :]) + mod_ref[0, 0:1, :]
    hb = h.astype(BF16)
    for c0 in range(0, proj_ref.shape[1], col):
        proj_ref[:, c0:c0 + col] = _dot(hb, win_ref[:, c0:c0 + col]).astype(BF16)
    for c0 in range(0, gate_ref.shape[1], col):
        g = jax.nn.sigmoid(_dot(hb, wg_ref[:, c0:c0 + col]) + bg_ref[:, c0:c0 + col])
        gate_ref[:, c0:c0 + col] = g.astype(BF16)


def _premix(x2, mod, g_pre, w_in_b, w_gate_b, b_gate, seq):
    n, d = x2.shape
    pw, gw = w_in_b.shape[1], w_gate_b.shape[1]
    tm = min(ROW_TILE, seq)
    per_b = seq // tm
    return pl.pallas_call(
        functools.partial(_premix_kernel, col=512),
        grid=(n // tm,),
        in_specs=[pl.BlockSpec((tm, d), lambda i: (i, 0)),
                  pl.BlockSpec((1,) + mod.shape[1:], lambda i: (i // per_b, 0, 0)),
                  pl.BlockSpec((1, d), lambda i: (0, 0)),
                  pl.BlockSpec((d, pw), lambda i: (0, 0)),
                  pl.BlockSpec((d, gw), lambda i: (0, 0)),
                  pl.BlockSpec((1, gw), lambda i: (0, 0))],
        out_specs=[pl.BlockSpec((tm, pw), lambda i: (i, 0)),
                   pl.BlockSpec((tm, gw), lambda i: (i, 0))],
        out_shape=[jax.ShapeDtypeStruct((n, pw), BF16), jax.ShapeDtypeStruct((n, gw), BF16)],
        compiler_params=_cparams(("arbitrary",)),
        name="premix",
    )(x2, mod, g_pre.reshape(1, d), w_in_b, w_gate_b, b_gate.reshape(1, gw))


def _sb_kernel(q_ref, k_ref, v_ref, u_ref, o_ref, z_scr, cum_scr, carry_scr, acc_scr):
    i = pl.program_id(2)
    t = q_ref.shape[0]
    tk = u_ref.shape[1]
    per_q = t // tk
    rows = SB_ROWS
    q2 = q_ref[...]
    lane = lax.broadcasted_iota(I32, (t, LANES), 1)
    u2 = u_ref[...]
    chains = [(hh, r0) for hh in range(2) for r0 in range(0, t, rows)]
    qs = []
    for hh, r0 in chains:
        in_head = (lane < HEAD_DIM) if hh == 0 else (lane >= HEAD_DIM)
        qs.append(jnp.where(in_head, q2, jnp.zeros_like(q2))[r0:r0 + rows])

    def causal(j_local):
        return [lax.broadcasted_iota(I32, (rows, tk), 1) + j_local * tk
                < lax.broadcasted_iota(I32, (rows, tk), 0) + r0 for _, r0 in chains]

    n_ch = len(chains)

    every = tuple(range(n_ch))

    def seeing(j_local):
        return tuple(c for c, (_, r0) in enumerate(chains) if r0 + rows - 1 > j_local * tk)

    def logits(j, slot, active=every):
        start = pl.multiple_of(j * tk, tk)
        kt = k_ref[pl.ds(start, tk), :]
        for c in active:
            z_scr[slot, c] = _dot_nt(qs[c], kt)

    def sums(slot, mask, active=every):
        for c in active:
            z = z_scr[slot, c].astype(BF16)
            sp = jnp.maximum(z, 0) + jnp.log(1 + jnp.exp(-jnp.abs(z)))
            if mask is not None:
                sp = jnp.where(mask[c], sp, jnp.zeros_like(sp))
            cum_scr[slot, c] = _dot(sp, u2)

    def finish(j, slot, mask, active=every):
        start = pl.multiple_of(j * tk, tk)
        vt = v_ref[pl.ds(start, tk), :]
        for c in active:
            cum = cum_scr[slot, c]
            carry = carry_scr[c]
            w = jnp.exp(z_scr[slot, c] - cum - jnp.concatenate([carry] * (tk // LANES), axis=1))
            if mask is not None:
                w = jnp.where(mask[c], w, 0.0)
            acc_scr[c] += _dot(w.astype(BF16), vt)
            carry_scr[c] = carry + jnp.broadcast_to(cum[:, 0:1], carry.shape)

    def sweep_tile(j, slot, mask, next_mask, active=every, next_active=every):
        logits(jnp.maximum(j - 1, 0), 1 - slot, next_active)
        finish(j, slot, mask, active)
        sums(1 - slot, next_mask, next_active)

    carry_scr[...] = jnp.zeros_like(carry_scr)
    acc_scr[...] = jnp.zeros_like(acc_scr)
    first = i * per_q
    logits(first + per_q - 1, (per_q - 1) % 2, seeing(per_q - 1))
    sums((per_q - 1) % 2, causal(per_q - 1), seeing(per_q - 1))
    for jl in range(per_q - 1, -1, -1):
        sweep_tile(first + jl, jl % 2, causal(jl), causal(jl - 1) if jl > 0 else None,
                   seeing(jl), seeing(jl - 1) if jl > 0 else every)

    def body(n, carry):
        j = first - 1 - 2 * n
        sweep_tile(j, 1, None, None)
        sweep_tile(j - 1, 0, None, None)
        return carry

    lax.fori_loop(0, first // 2, body, 0)
    accs = [jnp.concatenate([acc_scr[c] for c, (h2, _) in enumerate(chains) if h2 == hh], axis=0)
            for hh in range(2)]
    o_ref[...] = jnp.where(lane < HEAD_DIM, accs[0], accs[1]).astype(o_ref.dtype)


def _sb_attention(proj, batch, seq, width, q_col, k_col, v_col):
    n = proj.shape[0]
    t = min(SB_TILE, seq)
    tk = min(SB_KEY_TILE, t)
    assert (t // tk) % 2 == 0, "the sweep alternates two staging slots per key tile"
    nq = seq // t
    pairs = width // LANES
    n_chains = 2 * (t // SB_ROWS)
    tri = jnp.asarray((np.arange(tk)[:, None] >= np.arange(tk)[None, :]).astype(np.float32)).astype(BF16)
    return pl.pallas_call(
        _sb_kernel,
        grid=(batch, pairs, nq),
        in_specs=[pl.BlockSpec((t, LANES), lambda b, p, i: (b * nq + i, q_col + p)),
                  pl.BlockSpec((seq, LANES), lambda b, p, i: (b, k_col + p)),
                  pl.BlockSpec((seq, LANES), lambda b, p, i: (b, v_col + p)),
                  pl.BlockSpec((tk, tk), lambda b, p, i: (0, 0))],
        out_specs=pl.BlockSpec((t, LANES), lambda b, p, i: (b * nq + i, p)),
        out_shape=jax.ShapeDtypeStruct((n, width), BF16),
        scratch_shapes=[pltpu.VMEM((2, n_chains, SB_ROWS, tk), F32), pltpu.VMEM((2, n_chains, SB_ROWS, tk), F32),
                        pltpu.VMEM((n_chains, SB_ROWS, LANES), F32), pltpu.VMEM((n_chains, SB_ROWS, LANES), F32)],
        compiler_params=_cparams(("arbitrary", "arbitrary", "arbitrary")),
        name="stickbreak_attn",
    )(proj, proj, proj, tri)


def _ca_kernel(q_ref, k_ref, v_ref, t_ref, o_ref):
    r = pl.program_id(2)
    t = q_ref.shape[0]
    window = t_ref.shape[-1]
    start = pl.multiple_of(jnp.maximum(r - (CA_KEY_TILES - 1), 0) * t, t)
    kw = k_ref[pl.ds(start, window), :]
    vw = v_ref[pl.ds(start, window), :]
    q2 = q_ref[...]
    lane = lax.broadcasted_iota(I32, (t, LANES), 1)
    chains = [(hh, r0) for hh in range(2) for r0 in range(0, t, CHUNK)]
    zs = []
    for hh, r0 in chains:
        in_head = (lane < HEAD_DIM) if hh == 0 else (lane >= HEAD_DIM)
        qh = jnp.where(in_head, q2, jnp.zeros_like(q2))[r0:r0 + CHUNK]
        zs.append(_dot_nt(qh, kw) + t_ref[0, hh, r0:r0 + CHUNK, :])
    ps = [jnp.exp2(z - jnp.max(z, axis=-1, keepdims=True)) for z in zs]
    outs = [_dot(p.astype(BF16), vw) / jnp.sum(p, axis=-1, keepdims=True) for p in ps]
    heads = [jnp.concatenate([o for (h2, _), o in zip(chains, outs) if h2 == hh], axis=0) for hh in range(2)]
    o_ref[...] = jnp.where(lane < HEAD_DIM, heads[0], heads[1]).astype(o_ref.dtype)


def _ca_tables(rel_bias):
    h = rel_bias.shape[0]
    left = CA_LEFT_CHUNKS * CHUNK
    period = CA_WINDOW + CA_TILE
    u = np.arange(period)
    rel = np.where(u < CA_WINDOW, left - u, left + (period - u))
    v = rel_bias[:, np.clip(rel, -REL_CLIP, REL_CLIP) + REL_CLIP].astype(F32) * LOG2_E
    flat = jnp.tile(v, (1, CA_TILE))[:, :CA_TILE * (period - 1)]
    bias = flat.reshape(h, CA_TILE, period - 1)[:, :, :CA_WINDOW]
    i = np.arange(CA_TILE)[:, None]
    j = np.arange(CA_WINDOW)[None, :]
    first = (i // CHUNK) * CHUNK
    visible = (j >= first) & (j < first + (CA_LEFT_CHUNKS + 1) * CHUNK)
    full = jnp.where(visible[None], bias, NEG)
    tables = []
    for r in range(CA_KEY_TILES):
        shift = (CA_KEY_TILES - 1 - r) * CA_TILE
        tables.append(jnp.pad(full[:, :, shift:], ((0, 0), (0, 0), (0, shift)), constant_values=NEG))
    return jnp.stack(tables)


def _ca_attention(proj, tables, batch, seq, width, q_col, k_col, v_col):
    n = proj.shape[0]
    t = CA_TILE
    nq = seq // t
    pairs = width // LANES
    assert seq >= CA_WINDOW
    return pl.pallas_call(
        _ca_kernel,
        grid=(batch, pairs, nq),
        in_specs=[pl.BlockSpec((t, LANES), lambda b, p, i: (b * nq + i, q_col + p)),
                  pl.BlockSpec((seq, LANES), lambda b, p, i: (b, k_col + p)),
                  pl.BlockSpec((seq, LANES), lambda b, p, i: (b, v_col + p)),
                  pl.BlockSpec((1, 2, t, CA_WINDOW),
                               lambda b, p, i: (jnp.minimum(i, CA_KEY_TILES - 1), p, 0, 0))],
        out_specs=pl.BlockSpec((t, LANES), lambda b, p, i: (b * nq + i, p)),
        out_shape=jax.ShapeDtypeStruct((n, width), BF16),
        compiler_params=_cparams(("arbitrary", "arbitrary", "arbitrary")),
        name="chunkband_attn",
    )(proj, proj, proj, tables)


def _postmix_kernel(osb_ref, oca_ref, gate_ref, x_ref, mod_ref, gpost_ref, gffn_ref, wbs_ref, wbc_ref,
                    wout_ref, wr_ref, br_ref, x1_ref, h2_ref, tope_ref, topw_ref, cnt_ref):
    d = x_ref.shape[1]
    y_sb = _dot(osb_ref[...], wbs_ref[...])
    y_ca = _dot(oca_ref[...], wbc_ref[...])
    mixed_in = gate_ref[:, :d].astype(F32) * y_sb + gate_ref[:, d:].astype(F32) * y_ca
    mixed = _dot(mixed_in.astype(BF16), wout_ref[...])
    x1 = x_ref[...] + mod_ref[0, 2:3, :] * _rms(mixed, gpost_ref[...])
    x1_ref[...] = x1
    h2 = _rms(x1, gffn_ref[...]) * (1.0 + mod_ref[0, 4:5, :]) + mod_ref[0, 3:4, :]
    n_sub = d // LANES
    for s in range(n_sub):
        h2_ref[pl.ds(s, h2.shape[0], stride=n_sub), :] = h2[:, s * LANES:(s + 1) * LANES]
    h_hi = h2.astype(BF16)
    h_lo = (h2 - h_hi.astype(F32)).astype(BF16)
    logits = _dot(jnp.concatenate([h_hi, h_lo, h_hi], axis=1), wr_ref[...]) + br_ref[...]
    lane = lax.broadcasted_iota(I32, logits.shape, 1)
    vals, idxs = [], []
    for _ in range(TOP_K):
        m = jnp.max(logits, axis=-1, keepdims=True)
        idx = jnp.min(jnp.where(logits == m, lane, LANES), axis=-1, keepdims=True)
        vals.append(m)
        idxs.append(idx)
        logits = jnp.where(lane == idx, 2 * NEG, logits)
    ex = [jnp.exp(v - vals[0]) for v in vals]
    denom = functools.reduce(jnp.add, ex)
    tope = jnp.zeros(logits.shape, I32)
    topw = jnp.zeros(logits.shape, F32)
    for k in range(TOP_K):
        tope = jnp.where(lane == k, idxs[k], tope)
        topw = jnp.where(lane == k, ex[k] / denom, topw)
    tope_ref[...] = tope
    topw_ref[...] = topw

    @pl.when(pl.program_id(0) == 0)
    def _():
        cnt_ref[...] = jnp.zeros_like(cnt_ref)

    chosen = functools.reduce(jnp.logical_or, [lane == idx for idx in idxs])
    cnt_ref[0:1, :] += jnp.sum(chosen.astype(F32), axis=0, keepdims=True)


def _postmix(o_sb, o_ca, gate, x2, mod, g_post, g_ffn, wbs_b, wbc_b, wout_b, wr_pad, br_pad, seq):
    n, d = x2.shape
    w = o_sb.shape[1]
    tm = min(ROW_TILE, seq)
    per_b = seq // tm
    row = lambda i: (i, 0)
    fixed = lambda i: (0, 0)
    return pl.pallas_call(
        _postmix_kernel,
        grid=(n // tm,),
        in_specs=[pl.BlockSpec((tm, w), row), pl.BlockSpec((tm, w), row),
                  pl.BlockSpec((tm, 2 * d), row), pl.BlockSpec((tm, d), row),
                  pl.BlockSpec((1,) + mod.shape[1:], lambda i: (i // per_b, 0, 0)),
                  pl.BlockSpec((1, d), fixed), pl.BlockSpec((1, d), fixed),
                  pl.BlockSpec((w, d), fixed), pl.BlockSpec((w, d), fixed),
                  pl.BlockSpec((d, d), fixed), pl.BlockSpec((3 * d, LANES), fixed),
                  pl.BlockSpec((1, LANES), fixed)],
        out_specs=[pl.BlockSpec((tm, d), row), pl.BlockSpec((tm * (d // LANES), LANES), row),
                   pl.BlockSpec((tm, LANES), row), pl.BlockSpec((tm, LANES), row),
                   pl.BlockSpec((SUBLANES, LANES), fixed)],
        out_shape=[jax.ShapeDtypeStruct((n, d), F32), jax.ShapeDtypeStruct((n * (d // LANES), LANES), F32),
                   jax.ShapeDtypeStruct((n, LANES), I32), jax.ShapeDtypeStruct((n, LANES), F32),
                   jax.ShapeDtypeStruct((SUBLANES, LANES), F32)],
        compiler_params=_cparams(("arbitrary",)),
        name="postmix_router",
    )(o_sb, o_ca, gate, x2, mod, g_post.reshape(1, d), g_ffn.reshape(1, d), wbs_b, wbc_b, wout_b,
      wr_pad, br_pad)


def _rank_kernel(e_ref, l_ref, start_ref, dest_ref, run_ref):
    @pl.when(pl.program_id(0) == 0)
    def _():
        run_ref[...] = start_ref[...]

    e = e_ref[...]
    lane = lax.broadcasted_iota(I32, e.shape, 1)
    hots = [lane == e[:, k:k + 1] for k in range(TOP_K)]
    chosen = functools.reduce(jnp.logical_or, hots)
    before = _dot(l_ref[...], chosen.astype(BF16)) + run_ref[0:1, :]
    dest = jnp.zeros(e.shape, I32)
    for k in range(TOP_K):
        dk = jnp.sum(jnp.where(hots[k], before, 0.0), axis=-1, keepdims=True).astype(I32)
        dest = jnp.where(lane == k, dk, dest)
    dest_ref[...] = dest
    run_ref[0:1, :] += jnp.sum(chosen.astype(F32), axis=0, keepdims=True)


def _routing_ranks(top_e, group_start):
    n = top_e.shape[0]
    t = min(ROW_TILE, n)
    strict_lower = (jnp.arange(t)[:, None] > jnp.arange(t)[None, :]).astype(BF16)
    return pl.pallas_call(
        _rank_kernel,
        grid=(n // t,),
        in_specs=[pl.BlockSpec((t, LANES), lambda i: (i, 0)),
                  pl.BlockSpec((t, t), lambda i: (0, 0)),
                  pl.BlockSpec((SUBLANES, LANES), lambda i: (0, 0))],
        out_specs=pl.BlockSpec((t, LANES), lambda i: (i, 0)),
        out_shape=jax.ShapeDtypeStruct((n, LANES), I32),
        scratch_shapes=[pltpu.VMEM((SUBLANES, LANES), F32)],
        compiler_params=_cparams(("arbitrary",)),
        name="routing_ranks",
    )(top_e, strict_lower, group_start)


def _moe_kernel(be_ref, nvalid_ref, padlo_ref, padhi_ref, dest_ref, h2_hbm, w1_ref, b1_ref, w2_ref, b2_ref, y4_hbm,
                xbuf0, xbuf1, ybuf0, ybuf1, w1b, w2b, hid, sem_in, sem_out, rowflat_ref, *, n_tokens):
    i = pl.program_id(0)
    n_blocks = pl.num_programs(0)
    nv = nvalid_ref[0]
    n_sub = h2_hbm.shape[1]
    xbuf, ybuf = (xbuf0, xbuf1), (ybuf0, ybuf1)
    blk = xbuf0.shape[0] // n_sub
    d = n_sub * LANES
    di = w2b.shape[0]

    def tile(buf, r):
        return buf.at[pl.ds(r * n_sub, n_sub), :]

    def gather_copy(flat, r, slot):
        return pltpu.make_async_copy(h2_hbm.at[flat & (n_tokens - 1)], tile(xbuf[slot], r), sem_in.at[slot])

    def scatter_copy(flat, r, slot):
        return pltpu.make_async_copy(tile(ybuf[slot], r), y4_hbm.at[flat], sem_out.at[slot])

    def wait_gather(slot):
        pltpu.make_async_copy(xbuf[slot], xbuf[slot], sem_in.at[slot]).wait()

    def wait_scatter(slot):
        pltpu.make_async_copy(ybuf[slot], ybuf[slot], sem_out.at[slot]).wait()

    def step(slot):
        other = 1 - slot

        @pl.when(i == 0)
        def _():
            ybuf0[...] = jnp.zeros_like(ybuf0)
            ybuf1[...] = jnp.zeros_like(ybuf1)

            def clear(r, c):
                for s in range(2):
                    scatter_copy(n_tokens * TOP_K + s * blk + r, r, s).start()
                return c
            lax.fori_loop(0, blk, clear, 0)
            wait_scatter(0)
            wait_scatter(1)

            def first(r, c):
                gather_copy(rowflat_ref[r], r, slot).start()
                return c
            lax.fori_loop(0, blk, first, 0)

        wait_gather(slot)

        @pl.when(jnp.logical_or(i == 0, be_ref[i] != be_ref[jnp.maximum(i - 1, 0)]))
        def _():
            w1b[...] = w1_ref[0].astype(BF16)
            w2b[...] = w2_ref[0].astype(BF16)

        nxt = jnp.minimum(i + 1, n_blocks - 1) * blk
        prev = jnp.where(i == 0, n_blocks - 1, i - 1) * blk
        n_hid, n_out = di // MOE_COLS, d // MOE_COLS
        def issue(stage):
            if stage < n_hid:
                for r in range(stage * blk // n_hid, (stage + 1) * blk // n_hid):
                    gather_copy(rowflat_ref[nxt + r], r, other).start(priority=r % 2)
            else:
                st = stage - n_hid
                for r in range(st * blk // n_out, (st + 1) * blk // n_out):
                    scatter_copy(rowflat_ref[prev + r], r, other).start(priority=r % 2)

        xb = jnp.concatenate([xbuf[slot][pl.ds(s, blk, stride=n_sub), :] for s in range(n_sub)],
                             axis=1).astype(BF16)
        for c in range(n_hid):
            issue(c)
            c0 = c * MOE_COLS
            gate = _dot(xb, w1b[:, c0:c0 + MOE_COLS]) + b1_ref[0, :, c0:c0 + MOE_COLS]
            up = _dot(xb, w1b[:, di + c0:di + c0 + MOE_COLS]) + b1_ref[0, :, di + c0:di + c0 + MOE_COLS]
            gate = jnp.minimum(gate, SWIGLU_LIMIT)
            up = jnp.clip(up, -SWIGLU_LIMIT, SWIGLU_LIMIT)
            hid[:, c0:c0 + MOE_COLS] = (gate * jax.nn.sigmoid(SWIGLU_ALPHA * gate) * (up + 1.0)).astype(BF16)

        @pl.when(i >= 1)
        def _():
            wait_scatter(slot)

        hb = hid[...]
        for c in range(n_out):
            issue(n_hid + c)
            c0 = c * MOE_COLS
            y = _dot(hb, w2b[:, c0:c0 + MOE_COLS]) + b2_ref[0, :, c0:c0 + MOE_COLS]
            for s in range(MOE_COLS // LANES):
                ybuf[slot][pl.ds(c0 // LANES + s, blk, stride=n_sub), :] = y[:, s * LANES:(s + 1) * LANES]

        @pl.when(i == nv - 1)
        def _():
            def last(r, c):
                scatter_copy(rowflat_ref[i * blk + r], r, slot).start()
                return c
            lax.fori_loop(0, blk, last, 0)
            wait_scatter(other)
            wait_scatter(slot)
            wait_gather(other)

    @pl.when(i == 0)
    def _():
        def pad_range(g, c):
            def pad(r, c2):
                rowflat_ref[r] = n_tokens * TOP_K + (r & (2 * blk - 1))
                return c2
            return lax.fori_loop(padlo_ref[g], padhi_ref[g], pad, c)
        lax.fori_loop(0, padlo_ref.shape[0], pad_range, 0)

        def place(t, c):
            for k in range(TOP_K):
                rowflat_ref[dest_ref[t * TOP_K + k]] = k * n_tokens + t
            return c
        lax.fori_loop(0, n_tokens, place, 0, unroll=8)

    @pl.when(jnp.logical_and(i < nv, i % 2 == 0))
    def _():
        step(0)

    @pl.when(jnp.logical_and(i < nv, i % 2 == 1))
    def _():
        step(1)


def _expert_ffn(h2_tiles, dest, block_e, n_valid, pad_lo, pad_hi, w1, b1, w2, b2):
    e, d, two_i = w1.shape
    n_sub = d // LANES
    n = h2_tiles.shape[0] // n_sub
    h2 = h2_tiles.reshape(n, n_sub, LANES)
    n_blocks = block_e.shape[0]
    by_expert = lambda i, be, *_: (be[i], 0, 0)
    grid_spec = pltpu.PrefetchScalarGridSpec(
        num_scalar_prefetch=5,
        grid=(n_blocks,),
        in_specs=[pl.BlockSpec(memory_space=pl.ANY),
                  pl.BlockSpec((1, d, two_i), by_expert),
                  pl.BlockSpec((1, 1, two_i), by_expert),
                  pl.BlockSpec((1, two_i // 2, d), by_expert),
                  pl.BlockSpec((1, 1, d), by_expert)],
        out_specs=pl.BlockSpec(memory_space=pl.ANY),
        scratch_shapes=[pltpu.VMEM((MOE_BLOCK * n_sub, LANES), F32)] * 4 + [
                        pltpu.VMEM((d, two_i), BF16), pltpu.VMEM((two_i // 2, d), BF16),
                        pltpu.VMEM((MOE_BLOCK, two_i // 2), BF16),
                        pltpu.SemaphoreType.DMA((2,)), pltpu.SemaphoreType.DMA((2,)),
                        pltpu.SMEM((n_blocks * MOE_BLOCK,), I32)],
    )
    y4 = pl.pallas_call(
        functools.partial(_moe_kernel, n_tokens=n),
        grid_spec=grid_spec,
        out_shape=jax.ShapeDtypeStruct((n * TOP_K + 2 * MOE_BLOCK, n_sub, LANES), F32),
        compiler_params=_cparams(("arbitrary",)),
        name="expert_ffn",
    )(block_e, n_valid, pad_lo, pad_hi, dest, h2, w1, b1.reshape(e, 1, two_i), w2, b2.reshape(e, 1, d))
    return y4.reshape((n * TOP_K + 2 * MOE_BLOCK) * n_sub, LANES)


def _combine_kernel(*refs):
    y_refs, (w_ref, x1_ref, mod_ref, g_ref, o_ref) = refs[:TOP_K], refs[TOP_K:]
    tm, d = x1_ref.shape
    n_sub = d // LANES
    ff = jnp.concatenate(
        [functools.reduce(jnp.add, [y_refs[k][pl.ds(s, tm, stride=n_sub), :] * w_ref[:, k:k + 1]
                                    for k in range(TOP_K)])
         for s in range(n_sub)], axis=1)
    o_ref[...] = x1_ref[...] + mod_ref[0, 5:6, :] * _rms(ff, g_ref[...])


def _combine(y4, top_w, x1, mod, g_post, seq):
    n, d = x1.shape
    tm = min(ROW_TILE, seq)
    per_b = seq // tm
    n_sub = d // LANES
    y_specs = [pl.BlockSpec((tm * n_sub, LANES), functools.partial(lambda k, i: (k * (n // tm) + i, 0), k))
               for k in range(TOP_K)]
    return pl.pallas_call(
        _combine_kernel,
        grid=(n // tm,),
        in_specs=y_specs + [
                  pl.BlockSpec((tm, LANES), lambda i: (i, 0)),
                  pl.BlockSpec((tm, d), lambda i: (i, 0)),
                  pl.BlockSpec((1,) + mod.shape[1:], lambda i: (i // per_b, 0, 0)),
                  pl.BlockSpec((1, d), lambda i: (0, 0))],
        out_specs=pl.BlockSpec((tm, d), lambda i: (i, 0)),
        out_shape=jax.ShapeDtypeStruct((n, d), F32),
        compiler_params=_cparams(("arbitrary",)),
        name="combine",
    )(*([y4] * TOP_K), top_w, x1, mod, g_post.reshape(1, d))


def _layer(x, c, w_ada, b_ada, g_pre_mix, g_post_mix, w_in, rel_bias, w_branch_sb, w_branch_ca, w_gate,
           b_gate, w_out, g_pre_ffn, g_post_ffn, w_router, b_router, w_e1, b_e1, w_e2, b_e2):
    batch, seq, d = x.shape
    n = batch * seq
    sb_w = w_branch_sb.shape[0]
    ca_w = w_branch_ca.shape[0]
    n_exp = w_router.shape[1]
    x2 = x.reshape(n, d)

    mod = _modulation(c, w_ada, b_ada)

    scale = HEAD_DIM ** -0.5
    col_scale = np.ones((w_in.shape[1],), np.float32)
    col_scale[:sb_w] = scale
    col_scale[3 * sb_w:3 * sb_w + ca_w] = scale * LOG2_E
    w_in_b = (w_in * col_scale[None, :]).astype(BF16)
    proj, gate = _premix(x2, mod, g_pre_mix, w_in_b, w_gate.astype(BF16), b_gate, seq)

    sbl, cal = sb_w // LANES, ca_w // LANES
    o_sb = _sb_attention(proj, batch, seq, sb_w, 0, sbl, 2 * sbl)
    o_ca = _ca_attention(proj, _ca_tables(rel_bias), batch, seq, ca_w, 3 * sbl, 3 * sbl + cal, 3 * sbl + 2 * cal)

    wr_f32 = jnp.zeros((d, LANES), F32).at[:, :n_exp].set(w_router)
    wr_hi = wr_f32.astype(BF16)
    wr_lo = (wr_f32 - wr_hi.astype(F32)).astype(BF16)
    wr_pad = jnp.concatenate([wr_hi, wr_hi, wr_lo], axis=0)
    br_pad = jnp.full((1, LANES), NEG, F32).at[0, :n_exp].set(b_router)
    x1, h2, top_e, top_w, cnt = _postmix(o_sb, o_ca, gate, x2, mod, g_post_mix, g_pre_ffn,
                                         w_branch_sb.astype(BF16), w_branch_ca.astype(BF16),
                                         w_out.astype(BF16), wr_pad, br_pad, seq)

    nk = n * TOP_K
    counts = cnt[0, :n_exp].astype(I32)
    padded = (counts + MOE_BLOCK - 1) // MOE_BLOCK * MOE_BLOCK
    pad_end = jnp.cumsum(padded)
    pad_start = pad_end - padded
    group_start = jnp.zeros((SUBLANES, LANES), F32).at[0, :n_exp].set(pad_start.astype(F32))
    dest = _routing_ranks(top_e, group_start)[:, :TOP_K].reshape(nk)
    n_blocks = nk // MOE_BLOCK + n_exp
    assert n & (n - 1) == 0, "row ids are k * n + token with the token taken back by a bit mask"
    blk_start = jnp.arange(n_blocks, dtype=I32) * MOE_BLOCK
    block_e = jnp.minimum(jnp.sum((pad_end[None, :] <= blk_start[:, None]).astype(I32), axis=1), n_exp - 1)
    n_valid = (pad_end[-1] // MOE_BLOCK).astype(I32).reshape(1)

    rows_end = jnp.full((1,), n_blocks * MOE_BLOCK, I32)
    pad_lo = jnp.concatenate([pad_start + counts, pad_end[-1:], rows_end - MOE_BLOCK]).astype(I32)
    pad_hi = jnp.concatenate([pad_end, jnp.minimum(pad_end[-1:] + MOE_BLOCK, rows_end), rows_end]).astype(I32)
    y4 = _expert_ffn(h2, dest, block_e, n_valid, pad_lo, pad_hi, w_e1, b_e1, w_e2, b_e2)
    out = _combine(y4, top_w, x1, mod, g_post_ffn, seq)
    return out.reshape(batch, seq, d)


def kernel(x, c, w_ada, b_ada, g_pre_mix, g_post_mix, w_in, rel_bias, w_branch_sb, w_branch_ca, w_gate, b_gate,
           w_out, g_pre_ffn, g_post_ffn, w_router, b_router, w_e1, b_e1, w_e2, b_e2):
    for l in range(w_ada.shape[0]):
        x = _layer(x, c, w_ada[l], b_ada[l], g_pre_mix[l], g_post_mix[l], w_in[l], rel_bias[l],
                   w_branch_sb[l], w_branch_ca[l], w_gate[l], b_gate[l], w_out[l], g_pre_ffn[l],
                   g_post_ffn[l], w_router[l], b_router[l], w_e1[l], b_e1[l], w_e2[l], b_e2[l])
    return x
```

```python
import functools

import numpy as np
import jax
import jax.numpy as jnp
from jax import lax
from jax.experimental import pallas as pl
from jax.experimental.pallas import tpu as pltpu

F32 = jnp.float32
BF16 = jnp.bfloat16
I32 = jnp.int32

RMS_EPS = 1e-6
HEAD_DIM = 64
LANES = 128
SUBLANES = 8
CHUNK = 64
CA_LEFT_CHUNKS = 8
REL_CLIP = 256
TOP_K = 4
SWIGLU_ALPHA = 1.702
SWIGLU_LIMIT = 7.0
NEG = -1e30
LOG2_E = 1.4426950408889634

SB_TILE = 512
SB_KEY_TILE = 256
SB_ROWS = 128
CA_TILE = 4 * CHUNK
CA_KEY_TILES = CA_LEFT_CHUNKS * CHUNK // CA_TILE + 1
CA_WINDOW = CA_KEY_TILES * CA_TILE
ROW_TILE = 512
MOE_BLOCK = 512
MOE_COLS = 256
VMEM_LIMIT = 56 * 1024 * 1024


def _cparams(sem):
    return pltpu.CompilerParams(dimension_semantics=sem, vmem_limit_bytes=VMEM_LIMIT)


def _dot(a, b):
    return jnp.dot(a, b, preferred_element_type=F32)


def _dot_nt(a, b):
    return lax.dot_general(a, b, (((1,), (1,)), ((), ())), preferred_element_type=F32)


def _rms(x, g):
    return x * lax.rsqrt(jnp.mean(x * x, axis=-1, keepdims=True) + RMS_EPS) * g


def _mod_kernel(c_ref, w_ref, b_ref, o_ref):
    o_ref[...] = jnp.dot(c_ref[...], w_ref[...], preferred_element_type=F32,
                         precision=lax.Precision.HIGHEST) + b_ref[...]


def _modulation(c, w_ada, b_ada):
    b, d = c.shape
    n_out = w_ada.shape[1]
    c_pad = jnp.zeros((SUBLANES, d), F32).at[:b].set(c)
    out = pl.pallas_call(
        _mod_kernel,
        grid=(n_out // d,),
        in_specs=[pl.BlockSpec((SUBLANES, d), lambda j: (0, 0)),
                  pl.BlockSpec((d, d), lambda j: (0, j)),
                  pl.BlockSpec((1, d), lambda j: (0, j))],
        out_specs=pl.BlockSpec((SUBLANES, d), lambda j: (0, j)),
        out_shape=jax.ShapeDtypeStruct((SUBLANES, n_out), F32),
        compiler_params=_cparams(("arbitrary",)),
        name="adaln_mod",
    )(c_pad, w_ada, b_ada.reshape(1, n_out))
    return out[:b].reshape(b, n_out // d, d)


def _premix_kernel(x_ref, mod_ref, g_ref, win_ref, wg_ref, bg_ref, proj_ref, gate_ref, *, col):
    h = _rms(x_ref[...], g_ref[...]) * (1.0 + mod_ref[0, 1:2, :]) + mod_ref[0, 0:1, :]
    hb = h.astype(BF16)
    for c0 in range(0, proj_ref.shape[1], col):
        proj_ref[:, c0:c0 + col] = _dot(hb, win_ref[:, c0:c0 + col]).astype(BF16)
    for c0 in range(0, gate_ref.shape[1], col):
        g = jax.nn.sigmoid(_dot(hb, wg_ref[:, c0:c0 + col]) + bg_ref[:, c0:c0 + col])
        gate_ref[:, c0:c0 + col] = g.astype(BF16)


def _premix(x2, mod, g_pre, w_in_b, w_gate_b, b_gate, seq):
    n, d = x2.shape
    pw, gw = w_in_b.shape[1], w_gate_b.shape[1]
    tm = min(ROW_TILE, seq)
    per_b = seq // tm
    return pl.pallas_call(
        functools.partial(_premix_kernel, col=512),
        grid=(n // tm,),
        in_specs=[pl.BlockSpec((tm, d), lambda i: (i, 0)),
                  pl.BlockSpec((1,) + mod.shape[1:], lambda i: (i // per_b, 0, 0)),
                  pl.BlockSpec((1, d), lambda i: (0, 0)),
                  pl.BlockSpec((d, pw), lambda i: (0, 0)),
                  pl.BlockSpec((d, gw), lambda i: (0, 0)),
                  pl.BlockSpec((1, gw), lambda i: (0, 0))],
        out_specs=[pl.BlockSpec((tm, pw), lambda i: (i, 0)),
                   pl.BlockSpec((tm, gw), lambda i: (i, 0))],
        out_shape=[jax.ShapeDtypeStruct((n, pw), BF16), jax.ShapeDtypeStruct((n, gw), BF16)],
        compiler_params=_cparams(("arbitrary",)),
        name="premix",
    )(x2, mod, g_pre.reshape(1, d), w_in_b, w_gate_b, b_gate.reshape(1, gw))


def _sb_kernel(q_ref, k_ref, v_ref, u_ref, o_ref, z_scr, cum_scr, carry_scr, acc_scr):
    i = pl.program_id(2)
    t = q_ref.shape[0]
    tk = u_ref.shape[1]
    per_q = t // tk
    rows = SB_ROWS
    q2 = q_ref[...]
    lane = lax.broadcasted_iota(I32, (t, LANES), 1)
    u2 = u_ref[...]
    chains = [(hh, r0) for hh in range(2) for r0 in range(0, t, rows)]
    qs = []
    for hh, r0 in chains:
        in_head = (lane < HEAD_DIM) if hh == 0 else (lane >= HEAD_DIM)
        qs.append(jnp.where(in_head, q2, jnp.zeros_like(q2))[r0:r0 + rows])

    def causal(j_local):
        return [lax.broadcasted_iota(I32, (rows, tk), 1) + j_local * tk
                < lax.broadcasted_iota(I32, (rows, tk), 0) + r0 for _, r0 in chains]

    n_ch = len(chains)

    every = tuple(range(n_ch))

    def seeing(j_local):
        return tuple(c for c, (_, r0) in enumerate(chains) if r0 + rows - 1 > j_local * tk)

    def logits(j, slot, active=every):
        start = pl.multiple_of(j * tk, tk)
        kt = k_ref[pl.ds(start, tk), :]
        for c in active:
            z_scr[slot, c] = _dot_nt(qs[c], kt)

    def sums(slot, mask, active=every):
        for c in active:
            z = z_scr[slot, c].astype(BF16)
            sp = jnp.maximum(z, 0) + jnp.log(1 + jnp.exp(-jnp.abs(z)))
            if mask is not None:
                sp = jnp.where(mask[c], sp, jnp.zeros_like(sp))
            cum_scr[slot, c] = _dot(sp, u2)

    def finish(j, slot, mask, active=every):
        start = pl.multiple_of(j * tk, tk)
        vt = v_ref[pl.ds(start, tk), :]
        for c in active:
            cum = cum_scr[slot, c]
            carry = carry_scr[c]
            w = jnp.exp(z_scr[slot, c] - cum - jnp.concatenate([carry] * (tk // LANES), axis=1))
            if mask is not None:
                w = jnp.where(mask[c], w, 0.0)
            acc_scr[c] += _dot(w.astype(BF16), vt)
            carry_scr[c] = carry + jnp.broadcast_to(cum[:, 0:1], carry.shape)

    def sweep_tile(j, slot, mask, next_mask, active=every, next_active=every):
        logits(jnp.maximum(j - 1, 0), 1 - slot, next_active)
        finish(j, slot, mask, active)
        sums(1 - slot, next_mask, next_active)

    carry_scr[...] = jnp.zeros_like(carry_scr)
    acc_scr[...] = jnp.zeros_like(acc_scr)
    first = i * per_q
    logits(first + per_q - 1, (per_q - 1) % 2, seeing(per_q - 1))
    sums((per_q - 1) % 2, causal(per_q - 1), seeing(per_q - 1))
    for jl in range(per_q - 1, -1, -1):
        sweep_tile(first + jl, jl % 2, causal(jl), causal(jl - 1) if jl > 0 else None,
                   seeing(jl), seeing(jl - 1) if jl > 0 else every)

    def body(n, carry):
        j = first - 1 - 2 * n
        sweep_tile(j, 1, None, None)
        sweep_tile(j - 1, 0, None, None)
        return carry

    lax.fori_loop(0, first // 2, body, 0)
    accs = [jnp.concatenate([acc_scr[c] for c, (h2, _) in enumerate(chains) if h2 == hh], axis=0)
            for hh in range(2)]
    o_ref[...] = jnp.where(lane < HEAD_DIM, accs[0], accs[1]).astype(o_ref.dtype)


def _sb_attention(proj, batch, seq, width, q_col, k_col, v_col):
    n = proj.shape[0]
    t = min(SB_TILE, seq)
    tk = min(SB_KEY_TILE, t)
    assert (t // tk) % 2 == 0, "the sweep alternates two staging slots per key tile"
    nq = seq // t
    pairs = width // LANES
    n_chains = 2 * (t // SB_ROWS)
    tri = jnp.asarray((np.arange(tk)[:, None] >= np.arange(tk)[None, :]).astype(np.float32)).astype(BF16)
    return pl.pallas_call(
        _sb_kernel,
        grid=(batch, pairs, nq),
        in_specs=[pl.BlockSpec((t, LANES), lambda b, p, i: (b * nq + i, q_col + p)),
                  pl.BlockSpec((seq, LANES), lambda b, p, i: (b, k_col + p)),
                  pl.BlockSpec((seq, LANES), lambda b, p, i: (b, v_col + p)),
                  pl.BlockSpec((tk, tk), lambda b, p, i: (0, 0))],
        out_specs=pl.BlockSpec((t, LANES), lambda b, p, i: (b * nq + i, p)),
        out_shape=jax.ShapeDtypeStruct((n, width), BF16),
        scratch_shapes=[pltpu.VMEM((2, n_chains, SB_ROWS, tk), F32), pltpu.VMEM((2, n_chains, SB_ROWS, tk), F32),
                        pltpu.VMEM((n_chains, SB_ROWS, LANES), F32), pltpu.VMEM((n_chains, SB_ROWS, LANES), F32)],
        compiler_params=_cparams(("arbitrary", "arbitrary", "arbitrary")),
        name="stickbreak_attn",
    )(proj, proj, proj, tri)


def _ca_kernel(q_ref, k_ref, v_ref, t_ref, o_ref):
    r = pl.program_id(2)
    t = q_ref.shape[0]
    window = t_ref.shape[-1]
    start = pl.multiple_of(jnp.maximum(r - (CA_KEY_TILES - 1), 0) * t, t)
    kw = k_ref[pl.ds(start, window), :]
    vw = v_ref[pl.ds(start, window), :]
    q2 = q_ref[...]
    lane = lax.broadcasted_iota(I32, (t, LANES), 1)
    chains = [(hh, r0) for hh in range(2) for r0 in range(0, t, CHUNK)]
    zs = []
    for hh, r0 in chains:
        in_head = (lane < HEAD_DIM) if hh == 0 else (lane >= HEAD_DIM)
        qh = jnp.where(in_head, q2, jnp.zeros_like(q2))[r0:r0 + CHUNK]
        zs.append(_dot_nt(qh, kw) + t_ref[0, hh, r0:r0 + CHUNK, :])
    ps = [jnp.exp2(z - jnp.max(z, axis=-1, keepdims=True)) for z in zs]
    outs = [_dot(p.astype(BF16), vw) / jnp.sum(p, axis=-1, keepdims=True) for p in ps]
    heads = [jnp.concatenate([o for (h2, _), o in zip(chains, outs) if h2 == hh], axis=0) for hh in range(2)]
    o_ref[...] = jnp.where(lane < HEAD_DIM, heads[0], heads[1]).astype(o_ref.dtype)


def _ca_tables(rel_bias):
    h = rel_bias.shape[0]
    left = CA_LEFT_CHUNKS * CHUNK
    period = CA_WINDOW + CA_TILE
    u = np.arange(period)
    rel = np.where(u < CA_WINDOW, left - u, left + (period - u))
    v = rel_bias[:, np.clip(rel, -REL_CLIP, REL_CLIP) + REL_CLIP].astype(F32) * LOG2_E
    flat = jnp.tile(v, (1, CA_TILE))[:, :CA_TILE * (period - 1)]
    bias = flat.reshape(h, CA_TILE, period - 1)[:, :, :CA_WINDOW]
    i = np.arange(CA_TILE)[:, None]
    j = np.arange(CA_WINDOW)[None, :]
    first = (i // CHUNK) * CHUNK
    visible = (j >= first) & (j < first + (CA_LEFT_CHUNKS + 1) * CHUNK)
    full = jnp.where(visible[None], bias, NEG)
    tables = []
    for r in range(CA_KEY_TILES):
        shift = (CA_KEY_TILES - 1 - r) * CA_TILE
        tables.append(jnp.pad(full[:, :, shift:], ((0, 0), (0, 0), (0, shift)), constant_values=NEG))
    return jnp.stack(tables)


def _ca_attention(proj, tables, batch, seq, width, q_col, k_col, v_col):
    n = proj.shape[0]
    t = CA_TILE
    nq = seq // t
    pairs = width // LANES
    assert seq >= CA_WINDOW
    return pl.pallas_call(
        _ca_kernel,
        grid=(batch, pairs, nq),
        in_specs=[pl.BlockSpec((t, LANES), lambda b, p, i: (b * nq + i, q_col + p)),
                  pl.BlockSpec((seq, LANES), lambda b, p, i: (b, k_col + p)),
                  pl.BlockSpec((seq, LANES), lambda b, p, i: (b, v_col + p)),
                  pl.BlockSpec((1, 2, t, CA_WINDOW),
                               lambda b, p, i: (jnp.minimum(i, CA_KEY_TILES - 1), p, 0, 0))],
        out_specs=pl.BlockSpec((t, LANES), lambda b, p, i: (b * nq + i, p)),
        out_shape=jax.ShapeDtypeStruct((n, width), BF16),
        compiler_params=_cparams(("arbitrary", "arbitrary", "arbitrary")),
        name="chunkband_attn",
    )(proj, proj, proj, tables)


def _postmix_kernel(osb_ref, oca_ref, gate_ref, x_ref, mod_ref, gpost_ref, gffn_ref, wbs_ref, wbc_ref,
                    wout_ref, wr_ref, br_ref, x1_ref, h2_ref, tope_ref, topw_ref, cnt_ref):
    d = x_ref.shape[1]
    y_sb = _dot(osb_ref[...], wbs_ref[...])
    y_ca = _dot(oca_ref[...], wbc_ref[...])
    mixed_in = gate_ref[:, :d].astype(F32) * y_sb + gate_ref[:, d:].astype(F32) * y_ca
    mixed = _dot(mixed_in.astype(BF16), wout_ref[...])
    x1 = x_ref[...] + mod_ref[0, 2:3, :] * _rms(mixed, gpost_ref[...])
    x1_ref[...] = x1
    h2 = _rms(x1, gffn_ref[...]) * (1.0 + mod_ref[0, 4:5, :]) + mod_ref[0, 3:4, :]
    n_sub = d // LANES
    for s in range(n_sub):
        h2_ref[pl.ds(s, h2.shape[0], stride=n_sub), :] = h2[:, s * LANES:(s + 1) * LANES]
    h_hi = h2.astype(BF16)
    h_lo = (h2 - h_hi.astype(F32)).astype(BF16)
    logits = _dot(jnp.concatenate([h_hi, h_lo, h_hi], axis=1), wr_ref[...]) + br_ref[...]
    lane = lax.broadcasted_iota(I32, logits.shape, 1)
    lane_f = lane.astype(F32)
    vals, idxs = [], []
    for _ in range(TOP_K):
        m = jnp.max(logits, axis=-1, keepdims=True)
        idx_f = jnp.min(jnp.where(logits == m, lane_f, float(LANES)), axis=-1, keepdims=True)
        vals.append(m)
        idxs.append(idx_f.astype(I32))
        logits = jnp.where(lane_f == idx_f, 2 * NEG, logits)
    ex = [jnp.exp(v - vals[0]) for v in vals]
    denom = functools.reduce(jnp.add, ex)
    tope = jnp.zeros(logits.shape, I32)
    topw = jnp.zeros(logits.shape, F32)
    for k in range(TOP_K):
        tope = jnp.where(lane == k, idxs[k], tope)
        topw = jnp.where(lane == k, ex[k] / denom, topw)
    tope_ref[...] = tope
    topw_ref[...] = topw

    @pl.when(pl.program_id(0) == 0)
    def _():
        cnt_ref[...] = jnp.zeros_like(cnt_ref)

    chosen = functools.reduce(jnp.logical_or, [lane == idx for idx in idxs])
    cnt_ref[0:1, :] += jnp.sum(chosen.astype(F32), axis=0, keepdims=True)


def _postmix(o_sb, o_ca, gate, x2, mod, g_post, g_ffn, wbs_b, wbc_b, wout_b, wr_pad, br_pad, seq):
    n, d = x2.shape
    w = o_sb.shape[1]
    tm = min(ROW_TILE, seq)
    per_b = seq // tm
    row = lambda i: (i, 0)
    fixed = lambda i: (0, 0)
    return pl.pallas_call(
        _postmix_kernel,
        grid=(n // tm,),
        in_specs=[pl.BlockSpec((tm, w), row), pl.BlockSpec((tm, w), row),
                  pl.BlockSpec((tm, 2 * d), row), pl.BlockSpec((tm, d), row),
                  pl.BlockSpec((1,) + mod.shape[1:], lambda i: (i // per_b, 0, 0)),
                  pl.BlockSpec((1, d), fixed), pl.BlockSpec((1, d), fixed),
                  pl.BlockSpec((w, d), fixed), pl.BlockSpec((w, d), fixed),
                  pl.BlockSpec((d, d), fixed), pl.BlockSpec((3 * d, LANES), fixed),
                  pl.BlockSpec((1, LANES), fixed)],
        out_specs=[pl.BlockSpec((tm, d), row), pl.BlockSpec((tm * (d // LANES), LANES), row),
                   pl.BlockSpec((tm, LANES), row), pl.BlockSpec((tm, LANES), row),
                   pl.BlockSpec((SUBLANES, LANES), fixed)],
        out_shape=[jax.ShapeDtypeStruct((n, d), F32), jax.ShapeDtypeStruct((n * (d // LANES), LANES), F32),
                   jax.ShapeDtypeStruct((n, LANES), I32), jax.ShapeDtypeStruct((n, LANES), F32),
                   jax.ShapeDtypeStruct((SUBLANES, LANES), F32)],
        compiler_params=_cparams(("arbitrary",)),
        name="postmix_router",
    )(o_sb, o_ca, gate, x2, mod, g_post.reshape(1, d), g_ffn.reshape(1, d), wbs_b, wbc_b, wout_b,
      wr_pad, br_pad)


def _rank_kernel(e_ref, l_ref, start_ref, dest_ref, run_ref):
    @pl.when(pl.program_id(0) == 0)
    def _():
        run_ref[...] = start_ref[...]

    e = e_ref[...]
    lane = lax.broadcasted_iota(I32, e.shape, 1)
    hots = [lane == e[:, k:k + 1] for k in range(TOP_K)]
    chosen = functools.reduce(jnp.logical_or, hots)
    before = _dot(l_ref[...], chosen.astype(BF16)) + run_ref[0:1, :]
    dest = jnp.zeros(e.shape, I32)
    for k in range(TOP_K):
        dk = jnp.sum(jnp.where(hots[k], before, 0.0), axis=-1, keepdims=True).astype(I32)
        dest = jnp.where(lane == k, dk, dest)
    dest_ref[...] = dest
    run_ref[0:1, :] += jnp.sum(chosen.astype(F32), axis=0, keepdims=True)


def _routing_ranks(top_e, group_start):
    n = top_e.shape[0]
    t = min(ROW_TILE, n)
    strict_lower = (jnp.arange(t)[:, None] > jnp.arange(t)[None, :]).astype(BF16)
    return pl.pallas_call(
        _rank_kernel,
        grid=(n // t,),
        in_specs=[pl.BlockSpec((t, LANES), lambda i: (i, 0)),
                  pl.BlockSpec((t, t), lambda i: (0, 0)),
                  pl.BlockSpec((SUBLANES, LANES), lambda i: (0, 0))],
        out_specs=pl.BlockSpec((t, LANES), lambda i: (i, 0)),
        out_shape=jax.ShapeDtypeStruct((n, LANES), I32),
        scratch_shapes=[pltpu.VMEM((SUBLANES, LANES), F32)],
        compiler_params=_cparams(("arbitrary",)),
        name="routing_ranks",
    )(top_e, strict_lower, group_start)


def _moe_kernel(be_ref, nvalid_ref, padlo_ref, padhi_ref, dest_ref, h2_hbm, w1_ref, b1_ref, w2_ref, b2_ref, y4_hbm,
                xbuf0, xbuf1, ybuf0, ybuf1, w1b, w2b, hid, sem_in, sem_out, rowflat_ref, *, n_tokens):
    i = pl.program_id(0)
    n_blocks = pl.num_programs(0)
    nv = nvalid_ref[0]
    n_sub = h2_hbm.shape[1]
    xbuf, ybuf = (xbuf0, xbuf1), (ybuf0, ybuf1)
    blk = xbuf0.shape[0] // n_sub
    d = n_sub * LANES
    di = w2b.shape[0]

    def tile(buf, r):
        return buf.at[pl.ds(r * n_sub, n_sub), :]

    def gather_copy(flat, r, slot):
        return pltpu.make_async_copy(h2_hbm.at[flat & (n_tokens - 1)], tile(xbuf[slot], r), sem_in.at[slot])

    def scatter_copy(flat, r, slot):
        return pltpu.make_async_copy(tile(ybuf[slot], r), y4_hbm.at[flat], sem_out.at[slot])

    def wait_gather(slot):
        pltpu.make_async_copy(xbuf[slot], xbuf[slot], sem_in.at[slot]).wait()

    def wait_scatter(slot):
        pltpu.make_async_copy(ybuf[slot], ybuf[slot], sem_out.at[slot]).wait()

    def step(slot):
        other = 1 - slot

        @pl.when(i == 0)
        def _():
            ybuf0[...] = jnp.zeros_like(ybuf0)
            ybuf1[...] = jnp.zeros_like(ybuf1)

            def clear(r, c):
                for s in range(2):
                    scatter_copy(n_tokens * TOP_K + s * blk + r, r, s).start()
                return c
            lax.fori_loop(0, blk, clear, 0)
            wait_scatter(0)
            wait_scatter(1)

            def first(r, c):
                gather_copy(rowflat_ref[r], r, slot).start()
                return c
            lax.fori_loop(0, blk, first, 0)

        wait_gather(slot)

        @pl.when(jnp.logical_or(i == 0, be_ref[i] != be_ref[jnp.maximum(i - 1, 0)]))
        def _():
            w1b[...] = w1_ref[0].astype(BF16)
            w2b[...] = w2_ref[0].astype(BF16)

        nxt = jnp.minimum(i + 1, n_blocks - 1) * blk
        prev = jnp.where(i == 0, n_blocks - 1, i - 1) * blk
        n_hid, n_out = di // MOE_COLS, d // MOE_COLS
        def issue(stage):
            if stage < n_hid:
                for r in range(stage * blk // n_hid, (stage + 1) * blk // n_hid):
                    gather_copy(rowflat_ref[nxt + r], r, other).start(priority=r % 2)
            else:
                st = stage - n_hid
                for r in range(st * blk // n_out, (st + 1) * blk // n_out):
                    scatter_copy(rowflat_ref[prev + r], r, other).start(priority=r % 2)

        xb = jnp.concatenate([xbuf[slot][pl.ds(s, blk, stride=n_sub), :] for s in range(n_sub)],
                             axis=1).astype(BF16)
        for c in range(n_hid):
            issue(c)
            c0 = c * MOE_COLS
            gate = _dot(xb, w1b[:, c0:c0 + MOE_COLS]) + b1_ref[0, :, c0:c0 + MOE_COLS]
            up = _dot(xb, w1b[:, di + c0:di + c0 + MOE_COLS]) + b1_ref[0, :, di + c0:di + c0 + MOE_COLS]
            gate = jnp.minimum(gate, SWIGLU_LIMIT)
            up = jnp.clip(up, -SWIGLU_LIMIT, SWIGLU_LIMIT)
            hid[:, c0:c0 + MOE_COLS] = (gate * jax.nn.sigmoid(SWIGLU_ALPHA * gate) * (up + 1.0)).astype(BF16)

        @pl.when(i >= 1)
        def _():
            wait_scatter(slot)

        hb = hid[...]
        for c in range(n_out):
            issue(n_hid + c)
            c0 = c * MOE_COLS
            y = _dot(hb, w2b[:, c0:c0 + MOE_COLS]) + b2_ref[0, :, c0:c0 + MOE_COLS]
            for s in range(MOE_COLS // LANES):
                ybuf[slot][pl.ds(c0 // LANES + s, blk, stride=n_sub), :] = y[:, s * LANES:(s + 1) * LANES]

        @pl.when(i == nv - 1)
        def _():
            def last(r, c):
                scatter_copy(rowflat_ref[i * blk + r], r, slot).start()
                return c
            lax.fori_loop(0, blk, last, 0)
            wait_scatter(other)
            wait_scatter(slot)
            wait_gather(other)

    @pl.when(i == 0)
    def _():
        def pad_range(g, c):
            def pad(r, c2):
                rowflat_ref[r] = n_tokens * TOP_K + (r & (2 * blk - 1))
                return c2
            return lax.fori_loop(padlo_ref[g], padhi_ref[g], pad, c)
        lax.fori_loop(0, padlo_ref.shape[0], pad_range, 0)

        def place(t, c):
            for k in range(TOP_K):
                rowflat_ref[dest_ref[t * TOP_K + k]] = k * n_tokens + t
            return c
        lax.fori_loop(0, n_tokens, place, 0, unroll=8)

    @pl.when(jnp.logical_and(i < nv, i % 2 == 0))
    def _():
        step(0)

    @pl.when(jnp.logical_and(i < nv, i % 2 == 1))
    def _():
        step(1)


def _expert_ffn(h2_tiles, dest, block_e, n_valid, pad_lo, pad_hi, w1, b1, w2, b2):
    e, d, two_i = w1.shape
    n_sub = d // LANES
    n = h2_tiles.shape[0] // n_sub
    h2 = h2_tiles.reshape(n, n_sub, LANES)
    n_blocks = block_e.shape[0]
    by_expert = lambda i, be, *_: (be[i], 0, 0)
    grid_spec = pltpu.PrefetchScalarGridSpec(
        num_scalar_prefetch=5,
        grid=(n_blocks,),
        in_specs=[pl.BlockSpec(memory_space=pl.ANY),
                  pl.BlockSpec((1, d, two_i), by_expert),
                  pl.BlockSpec((1, 1, two_i), by_expert),
                  pl.BlockSpec((1, two_i // 2, d), by_expert),
                  pl.BlockSpec((1, 1, d), by_expert)],
        out_specs=pl.BlockSpec(memory_space=pl.ANY),
        scratch_shapes=[pltpu.VMEM((MOE_BLOCK * n_sub, LANES), F32)] * 4 + [
                        pltpu.VMEM((d, two_i), BF16), pltpu.VMEM((two_i // 2, d), BF16),
                        pltpu.VMEM((MOE_BLOCK, two_i // 2), BF16),
                        pltpu.SemaphoreType.DMA((2,)), pltpu.SemaphoreType.DMA((2,)),
                        pltpu.SMEM((n_blocks * MOE_BLOCK,), I32)],
    )
    y4 = pl.pallas_call(
        functools.partial(_moe_kernel, n_tokens=n),
        grid_spec=grid_spec,
        out_shape=jax.ShapeDtypeStruct((n * TOP_K + 2 * MOE_BLOCK, n_sub, LANES), F32),
        compiler_params=_cparams(("arbitrary",)),
        name="expert_ffn",
    )(block_e, n_valid, pad_lo, pad_hi, dest, h2, w1, b1.reshape(e, 1, two_i), w2, b2.reshape(e, 1, d))
    return y4.reshape((n * TOP_K + 2 * MOE_BLOCK) * n_sub, LANES)


def _combine_kernel(*refs):
    y_refs, (w_ref, x1_ref, mod_ref, g_ref, o_ref) = refs[:TOP_K], refs[TOP_K:]
    tm, d = x1_ref.shape
    n_sub = d // LANES
    ff = jnp.concatenate(
        [functools.reduce(jnp.add, [y_refs[k][pl.ds(s, tm, stride=n_sub), :] * w_ref[:, k:k + 1]
                                    for k in range(TOP_K)])
         for s in range(n_sub)], axis=1)
    o_ref[...] = x1_ref[...] + mod_ref[0, 5:6, :] * _rms(ff, g_ref[...])


def _combine(y4, top_w, x1, mod, g_post, seq):
    n, d = x1.shape
    tm = min(ROW_TILE, seq)
    per_b = seq // tm
    n_sub = d // LANES
    y_specs = [pl.BlockSpec((tm * n_sub, LANES), functools.partial(lambda k, i: (k * (n // tm) + i, 0), k))
               for k in range(TOP_K)]
    return pl.pallas_call(
        _combine_kernel,
        grid=(n // tm,),
        in_specs=y_specs + [
                  pl.BlockSpec((tm, LANES), lambda i: (i, 0)),
                  pl.BlockSpec((tm, d), lambda i: (i, 0)),
                  pl.BlockSpec((1,) + mod.shape[1:], lambda i: (i // per_b, 0, 0)),
                  pl.BlockSpec((1, d), lambda i: (0, 0))],
        out_specs=pl.BlockSpec((tm, d), lambda i: (i, 0)),
        out_shape=jax.ShapeDtypeStruct((n, d), F32),
        compiler_params=_cparams(("arbitrary",)),
        name="combine",
    )(*([y4] * TOP_K), top_w, x1, mod, g_post.reshape(1, d))


def _layer(x, c, w_ada, b_ada, g_pre_mix, g_post_mix, w_in, rel_bias, w_branch_sb, w_branch_ca, w_gate,
           b_gate, w_out, g_pre_ffn, g_post_ffn, w_router, b_router, w_e1, b_e1, w_e2, b_e2):
    batch, seq, d = x.shape
    n = batch * seq
    sb_w = w_branch_sb.shape[0]
    ca_w = w_branch_ca.shape[0]
    n_exp = w_router.shape[1]
    x2 = x.reshape(n, d)

    mod = _modulation(c, w_ada, b_ada)

    scale = HEAD_DIM ** -0.5
    col_scale = np.ones((w_in.shape[1],), np.float32)
    col_scale[:sb_w] = scale
    col_scale[3 * sb_w:3 * sb_w + ca_w] = scale * LOG2_E
    w_in_b = (w_in * col_scale[None, :]).astype(BF16)
    proj, gate = _premix(x2, mod, g_pre_mix, w_in_b, w_gate.astype(BF16), b_gate, seq)

    sbl, cal = sb_w // LANES, ca_w // LANES
    o_sb = _sb_attention(proj, batch, seq, sb_w, 0, sbl, 2 * sbl)
    o_ca = _ca_attention(proj, _ca_tables(rel_bias), batch, seq, ca_w, 3 * sbl, 3 * sbl + cal, 3 * sbl + 2 * cal)

    wr_f32 = jnp.zeros((d, LANES), F32).at[:, :n_exp].set(w_router)
    wr_hi = wr_f32.astype(BF16)
    wr_lo = (wr_f32 - wr_hi.astype(F32)).astype(BF16)
    wr_pad = jnp.concatenate([wr_hi, wr_hi, wr_lo], axis=0)
    br_pad = jnp.full((1, LANES), NEG, F32).at[0, :n_exp].set(b_router)
    x1, h2, top_e, top_w, cnt = _postmix(o_sb, o_ca, gate, x2, mod, g_post_mix, g_pre_ffn,
                                         w_branch_sb.astype(BF16), w_branch_ca.astype(BF16),
                                         w_out.astype(BF16), wr_pad, br_pad, seq)

    nk = n * TOP_K
    counts = cnt[0, :n_exp].astype(I32)
    padded = (counts + MOE_BLOCK - 1) // MOE_BLOCK * MOE_BLOCK
    pad_end = jnp.cumsum(padded)
    pad_start = pad_end - padded
    group_start = jnp.zeros((SUBLANES, LANES), F32).at[0, :n_exp].set(pad_start.astype(F32))
    dest = _routing_ranks(top_e, group_start)[:, :TOP_K].reshape(nk)
    n_blocks = nk // MOE_BLOCK + n_exp
    assert n & (n - 1) == 0, "row ids are k * n + token with the token taken back by a bit mask"
    blk_start = jnp.arange(n_blocks, dtype=I32) * MOE_BLOCK
    block_e = jnp.minimum(jnp.sum((pad_end[None, :] <= blk_start[:, None]).astype(I32), axis=1), n_exp - 1)
    n_valid = (pad_end[-1] // MOE_BLOCK).astype(I32).reshape(1)

    rows_end = jnp.full((1,), n_blocks * MOE_BLOCK, I32)
    pad_lo = jnp.concatenate([pad_start + counts, pad_end[-1:], rows_end - MOE_BLOCK]).astype(I32)
    pad_hi = jnp.concatenate([pad_end, jnp.minimum(pad_end[-1:] + MOE_BLOCK, rows_end), rows_end]).astype(I32)
    y4 = _expert_ffn(h2, dest, block_e, n_valid, pad_lo, pad_hi, w_e1, b_e1, w_e2, b_e2)
    out = _combine(y4, top_w, x1, mod, g_post_ffn, seq)
    return out.reshape(batch, seq, d)


def kernel(x, c, w_ada, b_ada, g_pre_mix, g_post_mix, w_in, rel_bias, w_branch_sb, w_branch_ca, w_gate, b_gate,
           w_out, g_pre_ffn, g_post_ffn, w_router, b_router, w_e1, b_e1, w_e2, b_e2):
    for l in range(w_ada.shape[0]):
        x = _layer(x, c, w_ada[l], b_ada[l], g_pre_mix[l], g_post_mix[l], w_in[l], rel_bias[l],
                   w_branch_sb[l], w_branch_ca[l], w_gate[l], b_gate[l], w_out[l], g_pre_ffn[l],
                   g_post_ffn[l], w_router[l], b_router[l], w_e1[l], b_e1[l], w_e2[l], b_e2[l])
    return x
```

```python
import functools

import numpy as np
import jax
import jax.numpy as jnp
from jax import lax
from jax.experimental import pallas as pl
from jax.experimental.pallas import tpu as pltpu

F32 = jnp.float32
BF16 = jnp.bfloat16
I32 = jnp.int32

RMS_EPS = 1e-6
HEAD_DIM = 64
LANES = 128
SUBLANES = 8
CHUNK = 64
CA_LEFT_CHUNKS = 8
REL_CLIP = 256
TOP_K = 4
SWIGLU_ALPHA = 1.702
SWIGLU_LIMIT = 7.0
NEG = -1e30
LOG2_E = 1.4426950408889634

SB_TILE = 512
SB_KEY_TILE = 256
SB_ROWS = 128
CA_TILE = 4 * CHUNK
CA_KEY_TILES = CA_LEFT_CHUNKS * CHUNK // CA_TILE + 1
CA_WINDOW = CA_KEY_TILES * CA_TILE
ROW_TILE = 512
MOE_BLOCK = 512
MOE_COLS = 256
VMEM_LIMIT = 56 * 1024 * 1024


def _cparams(sem):
    return pltpu.CompilerParams(dimension_semantics=sem, vmem_limit_bytes=VMEM_LIMIT)


def _dot(a, b):
    return jnp.dot(a, b, preferred_element_type=F32)


def _dot_nt(a, b):
    return lax.dot_general(a, b, (((1,), (1,)), ((), ())), preferred_element_type=F32)


def _rms(x, g):
    return x * lax.rsqrt(jnp.mean(x * x, axis=-1, keepdims=True) + RMS_EPS) * g


def _mod_kernel(c_ref, w_ref, b_ref, o_ref):
    o_ref[...] = jnp.dot(c_ref[...], w_ref[...], preferred_element_type=F32,
                         precision=lax.Precision.HIGHEST) + b_ref[...]


def _modulation(c, w_ada, b_ada):
    b, d = c.shape
    n_out = w_ada.shape[1]
    c_pad = jnp.zeros((SUBLANES, d), F32).at[:b].set(c)
    out = pl.pallas_call(
        _mod_kernel,
        grid=(n_out // d,),
        in_specs=[pl.BlockSpec((SUBLANES, d), lambda j: (0, 0)),
                  pl.BlockSpec((d, d), lambda j: (0, j)),
                  pl.BlockSpec((1, d), lambda j: (0, j))],
        out_specs=pl.BlockSpec((SUBLANES, d), lambda j: (0, j)),
        out_shape=jax.ShapeDtypeStruct((SUBLANES, n_out), F32),
        compiler_params=_cparams(("arbitrary",)),
        name="adaln_mod",
    )(c_pad, w_ada, b_ada.reshape(1, n_out))
    return out[:b].reshape(b, n_out // d, d)


def _premix_kernel(x_ref, mod_ref, g_ref, win_ref, wg_ref, bg_ref, proj_ref, gate_ref, *, col):
    h = _rms(x_ref[...], g_ref[...]) * (1.0 + mod_ref[0, 1:2, :]) + mod_ref[0, 0:1, :]
    hb = h.astype(BF16)
    for c0 in range(0, proj_ref.shape[1], col):
        proj_ref[:, c0:c0 + col] = _dot(hb, win_ref[:, c0:c0 + col]).astype(BF16)
    for c0 in range(0, gate_ref.shape[1], col):
        g = jax.nn.sigmoid(_dot(hb, wg_ref[:, c0:c0 + col]) + bg_ref[:, c0:c0 + col])
        gate_ref[:, c0:c0 + col] = g.astype(BF16)


def _premix(x2, mod, g_pre, w_in_b, w_gate_b, b_gate, seq):
    n, d = x2.shape
    pw, gw = w_in_b.shape[1], w_gate_b.shape[1]
    tm = min(ROW_TILE, seq)
    per_b = seq // tm
    return pl.pallas_call(
        functools.partial(_premix_kernel, col=512),
        grid=(n // tm,),
        in_specs=[pl.BlockSpec((tm, d), lambda i: (i, 0)),
                  pl.BlockSpec((1,) + mod.shape[1:], lambda i: (i // per_b, 0, 0)),
                  pl.BlockSpec((1, d), lambda i: (0, 0)),
                  pl.BlockSpec((d, pw), lambda i: (0, 0)),
                  pl.BlockSpec((d, gw), lambda i: (0, 0)),
                  pl.BlockSpec((1, gw), lambda i: (0, 0))],
        out_specs=[pl.BlockSpec((tm, pw), lambda i: (i, 0)),
                   pl.BlockSpec((tm, gw), lambda i: (i, 0))],
        out_shape=[jax.ShapeDtypeStruct((n, pw), BF16), jax.ShapeDtypeStruct((n, gw), BF16)],
        compiler_params=_cparams(("arbitrary",)),
        name="premix",
    )(x2, mod, g_pre.reshape(1, d), w_in_b, w_gate_b, b_gate.reshape(1, gw))


def _sb_kernel(q_ref, k_ref, v_ref, u_ref, o_ref, z_scr, cum_scr, carry_scr, acc_scr):
    t = acc_scr.shape[0] // 2 * SB_ROWS

    def query_tile(i, carry):
        rows = pl.ds(pl.multiple_of(i * t, t), t)
        _sb_query_tile(i, q_ref.at[rows, :], k_ref, v_ref, u_ref, o_ref.at[rows, :],
                       z_scr, cum_scr, carry_scr, acc_scr)
        return carry

    lax.fori_loop(0, q_ref.shape[0] // t, query_tile, 0)


def _sb_query_tile(i, q_ref, k_ref, v_ref, u_ref, o_ref, z_scr, cum_scr, carry_scr, acc_scr):
    t = q_ref.shape[0]
    tk = u_ref.shape[1]
    per_q = t // tk
    rows = SB_ROWS
    q2 = q_ref[...]
    lane = lax.broadcasted_iota(I32, (t, LANES), 1)
    u2 = u_ref[...]
    chains = [(hh, r0) for hh in range(2) for r0 in range(0, t, rows)]
    qs = []
    for hh, r0 in chains:
        in_head = (lane < HEAD_DIM) if hh == 0 else (lane >= HEAD_DIM)
        qs.append(jnp.where(in_head, q2, jnp.zeros_like(q2))[r0:r0 + rows])

    def causal(j_local):
        return [lax.broadcasted_iota(I32, (rows, tk), 1) + j_local * tk
                < lax.broadcasted_iota(I32, (rows, tk), 0) + r0 for _, r0 in chains]

    n_ch = len(chains)

    every = tuple(range(n_ch))

    def seeing(j_local):
        return tuple(c for c, (_, r0) in enumerate(chains) if r0 + rows - 1 > j_local * tk)

    def logits(j, slot, active=every):
        start = pl.multiple_of(j * tk, tk)
        kt = k_ref[pl.ds(start, tk), :]
        for c in active:
            z_scr[slot, c] = _dot_nt(qs[c], kt)

    def sums(slot, mask, active=every):
        for c in active:
            z = z_scr[slot, c].astype(BF16)
            sp = jnp.maximum(z, 0) + jnp.log(1 + jnp.exp(-jnp.abs(z)))
            if mask is not None:
                sp = jnp.where(mask[c], sp, jnp.zeros_like(sp))
            cum_scr[slot, c] = _dot(sp, u2)

    def finish(j, slot, mask, active=every):
        start = pl.multiple_of(j * tk, tk)
        vt = v_ref[pl.ds(start, tk), :]
        for c in active:
            cum = cum_scr[slot, c]
            carry = carry_scr[c]
            w = jnp.exp(z_scr[slot, c] - cum - jnp.concatenate([carry] * (tk // LANES), axis=1))
            if mask is not None:
                w = jnp.where(mask[c], w, 0.0)
            acc_scr[c] += _dot(w.astype(BF16), vt)
            carry_scr[c] = carry + jnp.broadcast_to(cum[:, 0:1], carry.shape)

    def sweep_tile(j, slot, mask, next_mask, active=every, next_active=every):
        logits(jnp.maximum(j - 1, 0), 1 - slot, next_active)
        finish(j, slot, mask, active)
        sums(1 - slot, next_mask, next_active)

    carry_scr[...] = jnp.zeros_like(carry_scr)
    acc_scr[...] = jnp.zeros_like(acc_scr)
    first = i * per_q
    logits(first + per_q - 1, (per_q - 1) % 2, seeing(per_q - 1))
    sums((per_q - 1) % 2, causal(per_q - 1), seeing(per_q - 1))
    for jl in range(per_q - 1, -1, -1):
        sweep_tile(first + jl, jl % 2, causal(jl), causal(jl - 1) if jl > 0 else None,
                   seeing(jl), seeing(jl - 1) if jl > 0 else every)

    def body(n, carry):
        j = first - 1 - 2 * n
        sweep_tile(j, 1, None, None)
        sweep_tile(j - 1, 0, None, None)
        return carry

    lax.fori_loop(0, first // 2, body, 0)
    accs = [jnp.concatenate([acc_scr[c] for c, (h2, _) in enumerate(chains) if h2 == hh], axis=0)
            for hh in range(2)]
    o_ref[...] = jnp.where(lane < HEAD_DIM, accs[0], accs[1]).astype(o_ref.dtype)


def _sb_attention(proj, batch, seq, width, q_col, k_col, v_col):
    n = proj.shape[0]
    t = min(SB_TILE, seq)
    tk = min(SB_KEY_TILE, t)
    assert (t // tk) % 2 == 0, "the sweep alternates two staging slots per key tile"
    nq = seq // t
    pairs = width // LANES
    n_chains = 2 * (t // SB_ROWS)
    tri = jnp.asarray((np.arange(tk)[:, None] >= np.arange(tk)[None, :]).astype(np.float32)).astype(BF16)
    return pl.pallas_call(
        _sb_kernel,
        grid=(batch, pairs),
        in_specs=[pl.BlockSpec((seq, LANES), lambda b, p: (b, q_col + p)),
                  pl.BlockSpec((seq, LANES), lambda b, p: (b, k_col + p)),
                  pl.BlockSpec((seq, LANES), lambda b, p: (b, v_col + p)),
                  pl.BlockSpec((tk, tk), lambda b, p: (0, 0))],
        out_specs=pl.BlockSpec((seq, LANES), lambda b, p: (b, p)),
        out_shape=jax.ShapeDtypeStruct((n, width), BF16),
        scratch_shapes=[pltpu.VMEM((2, n_chains, SB_ROWS, tk), F32), pltpu.VMEM((2, n_chains, SB_ROWS, tk), F32),
                        pltpu.VMEM((n_chains, SB_ROWS, LANES), F32), pltpu.VMEM((n_chains, SB_ROWS, LANES), F32)],
        compiler_params=_cparams(("arbitrary", "arbitrary")),
        name="stickbreak_attn",
    )(proj, proj, proj, tri)


def _ca_kernel(q_ref, k_ref, v_ref, t_ref, o_ref):
    t = t_ref.shape[2]
    window = t_ref.shape[3]
    lane = lax.broadcasted_iota(I32, (t, LANES), 1)
    chains = [(hh, r0) for hh in range(2) for r0 in range(0, t, CHUNK)]

    def tile(r, carry):
        rows = pl.ds(pl.multiple_of(r * t, t), t)
        start = pl.multiple_of(jnp.maximum(r - (CA_KEY_TILES - 1), 0) * t, t)
        placement = jnp.minimum(r, CA_KEY_TILES - 1)
        kw = k_ref[pl.ds(start, window), :]
        vw = v_ref[pl.ds(start, window), :]
        q2 = q_ref[rows, :]
        zs = []
        for hh, r0 in chains:
            in_head = (lane < HEAD_DIM) if hh == 0 else (lane >= HEAD_DIM)
            qh = jnp.where(in_head, q2, jnp.zeros_like(q2))[r0:r0 + CHUNK]
            zs.append(_dot_nt(qh, kw) + t_ref[placement, hh, r0:r0 + CHUNK, :])
        ps = [jnp.exp2(z - jnp.max(z, axis=-1, keepdims=True)) for z in zs]
        outs = [_dot(p.astype(BF16), vw) / jnp.sum(p, axis=-1, keepdims=True) for p in ps]
        heads = [jnp.concatenate([o for (h2, _), o in zip(chains, outs) if h2 == hh], axis=0)
                 for hh in range(2)]
        o_ref[rows, :] = jnp.where(lane < HEAD_DIM, heads[0], heads[1]).astype(o_ref.dtype)
        return carry

    lax.fori_loop(0, q_ref.shape[0] // t, tile, 0)


def _ca_tables(rel_bias):
    h = rel_bias.shape[0]
    left = CA_LEFT_CHUNKS * CHUNK
    period = CA_WINDOW + CA_TILE
    u = np.arange(period)
    rel = np.where(u < CA_WINDOW, left - u, left + (period - u))
    v = rel_bias[:, np.clip(rel, -REL_CLIP, REL_CLIP) + REL_CLIP].astype(F32) * LOG2_E
    flat = jnp.tile(v, (1, CA_TILE))[:, :CA_TILE * (period - 1)]
    bias = flat.reshape(h, CA_TILE, period - 1)[:, :, :CA_WINDOW]
    i = np.arange(CA_TILE)[:, None]
    j = np.arange(CA_WINDOW)[None, :]
    first = (i // CHUNK) * CHUNK
    visible = (j >= first) & (j < first + (CA_LEFT_CHUNKS + 1) * CHUNK)
    full = jnp.where(visible[None], bias, NEG)
    tables = []
    for r in range(CA_KEY_TILES):
        shift = (CA_KEY_TILES - 1 - r) * CA_TILE
        tables.append(jnp.pad(full[:, :, shift:], ((0, 0), (0, 0), (0, shift)), constant_values=NEG))
    return jnp.stack(tables)


def _ca_attention(proj, tables, batch, seq, width, q_col, k_col, v_col):
    n = proj.shape[0]
    t = CA_TILE
    nq = seq // t
    pairs = width // LANES
    assert seq >= CA_WINDOW
    return pl.pallas_call(
        _ca_kernel,
        grid=(batch, pairs),
        in_specs=[pl.BlockSpec((seq, LANES), lambda b, p: (b, q_col + p)),
                  pl.BlockSpec((seq, LANES), lambda b, p: (b, k_col + p)),
                  pl.BlockSpec((seq, LANES), lambda b, p: (b, v_col + p)),
                  pl.BlockSpec((CA_KEY_TILES, 2, t, CA_WINDOW), lambda b, p: (0, p, 0, 0))],
        out_specs=pl.BlockSpec((seq, LANES), lambda b, p: (b, p)),
        out_shape=jax.ShapeDtypeStruct((n, width), BF16),
        compiler_params=_cparams(("arbitrary", "arbitrary")),
        name="chunkband_attn",
    )(proj, proj, proj, tables)


def _postmix_kernel(osb_ref, oca_ref, gate_ref, x_ref, mod_ref, gpost_ref, gffn_ref, wbs_ref, wbc_ref,
                    wout_ref, wr_ref, br_ref, x1_ref, h2_ref, tope_ref, topw_ref, cnt_ref):
    d = x_ref.shape[1]
    y_sb = _dot(osb_ref[...], wbs_ref[...])
    y_ca = _dot(oca_ref[...], wbc_ref[...])
    mixed_in = gate_ref[:, :d].astype(F32) * y_sb + gate_ref[:, d:].astype(F32) * y_ca
    mixed = _dot(mixed_in.astype(BF16), wout_ref[...])
    x1 = x_ref[...] + mod_ref[0, 2:3, :] * _rms(mixed, gpost_ref[...])
    x1_ref[...] = x1
    h2 = _rms(x1, gffn_ref[...]) * (1.0 + mod_ref[0, 4:5, :]) + mod_ref[0, 3:4, :]
    n_sub = d // LANES
    for s in range(n_sub):
        h2_ref[pl.ds(s, h2.shape[0], stride=n_sub), :] = h2[:, s * LANES:(s + 1) * LANES]
    h_hi = h2.astype(BF16)
    h_lo = (h2 - h_hi.astype(F32)).astype(BF16)
    logits = _dot(jnp.concatenate([h_hi, h_lo, h_hi], axis=1), wr_ref[...]) + br_ref[...]
    lane = lax.broadcasted_iota(I32, logits.shape, 1)
    lane_f = lane.astype(F32)
    vals, idxs = [], []
    for _ in range(TOP_K):
        m = jnp.max(logits, axis=-1, keepdims=True)
        idx_f = jnp.min(jnp.where(logits == m, lane_f, float(LANES)), axis=-1, keepdims=True)
        vals.append(m)
        idxs.append(idx_f.astype(I32))
        logits = jnp.where(lane_f == idx_f, 2 * NEG, logits)
    ex = [jnp.exp(v - vals[0]) for v in vals]
    denom = functools.reduce(jnp.add, ex)
    tope = jnp.zeros(logits.shape, I32)
    topw = jnp.zeros(logits.shape, F32)
    for k in range(TOP_K):
        tope = jnp.where(lane == k, idxs[k], tope)
        topw = jnp.where(lane == k, ex[k] / denom, topw)
    tope_ref[...] = tope
    topw_ref[...] = topw

    @pl.when(pl.program_id(0) == 0)
    def _():
        cnt_ref[...] = jnp.zeros_like(cnt_ref)

    chosen = functools.reduce(jnp.logical_or, [lane == idx for idx in idxs])
    cnt_ref[0:1, :] += jnp.sum(chosen.astype(F32), axis=0, keepdims=True)


def _postmix(o_sb, o_ca, gate, x2, mod, g_post, g_ffn, wbs_b, wbc_b, wout_b, wr_pad, br_pad, seq):
    n, d = x2.shape
    w = o_sb.shape[1]
    tm = min(ROW_TILE, seq)
    per_b = seq // tm
    row = lambda i: (i, 0)
    fixed = lambda i: (0, 0)
    return pl.pallas_call(
        _postmix_kernel,
        grid=(n // tm,),
        in_specs=[pl.BlockSpec((tm, w), row), pl.BlockSpec((tm, w), row),
                  pl.BlockSpec((tm, 2 * d), row), pl.BlockSpec((tm, d), row),
                  pl.BlockSpec((1,) + mod.shape[1:], lambda i: (i // per_b, 0, 0)),
                  pl.BlockSpec((1, d), fixed), pl.BlockSpec((1, d), fixed),
                  pl.BlockSpec((w, d), fixed), pl.BlockSpec((w, d), fixed),
                  pl.BlockSpec((d, d), fixed), pl.BlockSpec((3 * d, LANES), fixed),
                  pl.BlockSpec((1, LANES), fixed)],
        out_specs=[pl.BlockSpec((tm, d), row), pl.BlockSpec((tm * (d // LANES), LANES), row),
                   pl.BlockSpec((tm, LANES), row), pl.BlockSpec((tm, LANES), row),
                   pl.BlockSpec((SUBLANES, LANES), fixed)],
        out_shape=[jax.ShapeDtypeStruct((n, d), F32), jax.ShapeDtypeStruct((n * (d // LANES), LANES), F32),
                   jax.ShapeDtypeStruct((n, LANES), I32), jax.ShapeDtypeStruct((n, LANES), F32),
                   jax.ShapeDtypeStruct((SUBLANES, LANES), F32)],
        compiler_params=_cparams(("arbitrary",)),
        name="postmix_router",
    )(o_sb, o_ca, gate, x2, mod, g_post.reshape(1, d), g_ffn.reshape(1, d), wbs_b, wbc_b, wout_b,
      wr_pad, br_pad)


def _rank_kernel(e_ref, l_ref, start_ref, dest_ref, run_ref):
    @pl.when(pl.program_id(0) == 0)
    def _():
        run_ref[...] = start_ref[...]

    e = e_ref[...]
    lane = lax.broadcasted_iota(I32, e.shape, 1)
    hots = [lane == e[:, k:k + 1] for k in range(TOP_K)]
    chosen = functools.reduce(jnp.logical_or, hots)
    before = _dot(l_ref[...], chosen.astype(BF16)) + run_ref[0:1, :]
    dest = jnp.zeros(e.shape, I32)
    for k in range(TOP_K):
        dk = jnp.sum(jnp.where(hots[k], before, 0.0), axis=-1, keepdims=True).astype(I32)
        dest = jnp.where(lane == k, dk, dest)
    dest_ref[...] = dest
    run_ref[0:1, :] += jnp.sum(chosen.astype(F32), axis=0, keepdims=True)


def _routing_ranks(top_e, group_start):
    n = top_e.shape[0]
    t = min(ROW_TILE, n)
    strict_lower = (jnp.arange(t)[:, None] > jnp.arange(t)[None, :]).astype(BF16)
    return pl.pallas_call(
        _rank_kernel,
        grid=(n // t,),
        in_specs=[pl.BlockSpec((t, LANES), lambda i: (i, 0)),
                  pl.BlockSpec((t, t), lambda i: (0, 0)),
                  pl.BlockSpec((SUBLANES, LANES), lambda i: (0, 0))],
        out_specs=pl.BlockSpec((t, LANES), lambda i: (i, 0)),
        out_shape=jax.ShapeDtypeStruct((n, LANES), I32),
        scratch_shapes=[pltpu.VMEM((SUBLANES, LANES), F32)],
        compiler_params=_cparams(("arbitrary",)),
        name="routing_ranks",
    )(top_e, strict_lower, group_start)


def _moe_kernel(be_ref, nvalid_ref, padlo_ref, padhi_ref, dest_ref, h2_hbm, w1_ref, b1_ref, w2_ref, b2_ref, y4_hbm,
                xbuf0, xbuf1, ybuf0, ybuf1, w1b, w2b, hid, sem_in, sem_out, rowflat_ref, *, n_tokens):
    i = pl.program_id(0)
    n_blocks = pl.num_programs(0)
    nv = nvalid_ref[0]
    n_sub = h2_hbm.shape[1]
    xbuf, ybuf = (xbuf0, xbuf1), (ybuf0, ybuf1)
    blk = xbuf0.shape[0] // n_sub
    d = n_sub * LANES
    di = w2b.shape[0]

    def tile(buf, r):
        return buf.at[pl.ds(r * n_sub, n_sub), :]

    def gather_copy(flat, r, slot):
        return pltpu.make_async_copy(h2_hbm.at[flat & (n_tokens - 1)], tile(xbuf[slot], r), sem_in.at[slot])

    def scatter_copy(flat, r, slot):
        return pltpu.make_async_copy(tile(ybuf[slot], r), y4_hbm.at[flat], sem_out.at[slot])

    def wait_gather(slot):
        pltpu.make_async_copy(xbuf[slot], xbuf[slot], sem_in.at[slot]).wait()

    def wait_scatter(slot):
        pltpu.make_async_copy(ybuf[slot], ybuf[slot], sem_out.at[slot]).wait()

    def step(slot):
        other = 1 - slot

        @pl.when(i == 0)
        def _():
            ybuf0[...] = jnp.zeros_like(ybuf0)
            ybuf1[...] = jnp.zeros_like(ybuf1)

            def clear(r, c):
                for s in range(2):
                    scatter_copy(n_tokens * TOP_K + s * blk + r, r, s).start()
                return c
            lax.fori_loop(0, blk, clear, 0)
            wait_scatter(0)
            wait_scatter(1)

            def first(r, c):
                gather_copy(rowflat_ref[r], r, slot).start()
                return c
            lax.fori_loop(0, blk, first, 0)

        wait_gather(slot)

        @pl.when(jnp.logical_or(i == 0, be_ref[i] != be_ref[jnp.maximum(i - 1, 0)]))
        def _():
            w1b[...] = w1_ref[0].astype(BF16)
            w2b[...] = w2_ref[0].astype(BF16)

        nxt = jnp.minimum(i + 1, n_blocks - 1) * blk
        prev = jnp.where(i == 0, n_blocks - 1, i - 1) * blk
        n_hid, n_out = di // MOE_COLS, d // MOE_COLS
        def issue(stage):
            if stage < n_hid:
                for r in range(stage * blk // n_hid, (stage + 1) * blk // n_hid):
                    gather_copy(rowflat_ref[nxt + r], r, other).start(priority=r % 2)
            else:
                st = stage - n_hid
                for r in range(st * blk // n_out, (st + 1) * blk // n_out):
                    scatter_copy(rowflat_ref[prev + r], r, other).start(priority=r % 2)

        xb = jnp.concatenate([xbuf[slot][pl.ds(s, blk, stride=n_sub), :] for s in range(n_sub)],
                             axis=1).astype(BF16)
        for c in range(n_hid):
            issue(c)
            c0 = c * MOE_COLS
            gate = _dot(xb, w1b[:, c0:c0 + MOE_COLS]) + b1_ref[0, :, c0:c0 + MOE_COLS]
            up = _dot(xb, w1b[:, di + c0:di + c0 + MOE_COLS]) + b1_ref[0, :, di + c0:di + c0 + MOE_COLS]
            gate = jnp.minimum(gate, SWIGLU_LIMIT)
            up = jnp.clip(up, -SWIGLU_LIMIT, SWIGLU_LIMIT)
            hid[:, c0:c0 + MOE_COLS] = (gate * jax.nn.sigmoid(SWIGLU_ALPHA * gate) * (up + 1.0)).astype(BF16)

        @pl.when(i >= 1)
        def _():
            wait_scatter(slot)

        hb = hid[...]
        for c in range(n_out):
            issue(n_hid + c)
            c0 = c * MOE_COLS
            y = _dot(hb, w2b[:, c0:c0 + MOE_COLS]) + b2_ref[0, :, c0:c0 + MOE_COLS]
            for s in range(MOE_COLS // LANES):
                ybuf[slot][pl.ds(c0 // LANES + s, blk, stride=n_sub), :] = y[:, s * LANES:(s + 1) * LANES]

        @pl.when(i == nv - 1)
        def _():
            def last(r, c):
                scatter_copy(rowflat_ref[i * blk + r], r, slot).start()
                return c
            lax.fori_loop(0, blk, last, 0)
            wait_scatter(other)
            wait_scatter(slot)
            wait_gather(other)

    @pl.when(i == 0)
    def _():
        def pad_range(g, c):
            def pad(r, c2):
                rowflat_ref[r] = n_tokens * TOP_K + (r & (2 * blk - 1))
                return c2
            return lax.fori_loop(padlo_ref[g], padhi_ref[g], pad, c)
        lax.fori_loop(0, padlo_ref.shape[0], pad_range, 0)

        def place(t, c):
            for k in range(TOP_K):
                rowflat_ref[dest_ref[t * TOP_K + k]] = k * n_tokens + t
            return c
        lax.fori_loop(0, n_tokens, place, 0, unroll=8)

    @pl.when(jnp.logical_and(i < nv, i % 2 == 0))
    def _():
        step(0)

    @pl.when(jnp.logical_and(i < nv, i % 2 == 1))
    def _():
        step(1)


def _expert_ffn(h2_tiles, dest, block_e, n_valid, pad_lo, pad_hi, w1, b1, w2, b2):
    e, d, two_i = w1.shape
    n_sub = d // LANES
    n = h2_tiles.shape[0] // n_sub
    h2 = h2_tiles.reshape(n, n_sub, LANES)
    n_blocks = block_e.shape[0]
    by_expert = lambda i, be, *_: (be[i], 0, 0)
    grid_spec = pltpu.PrefetchScalarGridSpec(
        num_scalar_prefetch=5,
        grid=(n_blocks,),
        in_specs=[pl.BlockSpec(memory_space=pl.ANY),
                  pl.BlockSpec((1, d, two_i), by_expert),
                  pl.BlockSpec((1, 1, two_i), by_expert),
                  pl.BlockSpec((1, two_i // 2, d), by_expert),
                  pl.BlockSpec((1, 1, d), by_expert)],
        out_specs=pl.BlockSpec(memory_space=pl.ANY),
        scratch_shapes=[pltpu.VMEM((MOE_BLOCK * n_sub, LANES), F32)] * 4 + [
                        pltpu.VMEM((d, two_i), BF16), pltpu.VMEM((two_i // 2, d), BF16),
                        pltpu.VMEM((MOE_BLOCK, two_i // 2), BF16),
                        pltpu.SemaphoreType.DMA((2,)), pltpu.SemaphoreType.DMA((2,)),
                        pltpu.SMEM((n_blocks * MOE_BLOCK,), I32)],
    )
    y4 = pl.pallas_call(
        functools.partial(_moe_kernel, n_tokens=n),
        grid_spec=grid_spec,
        out_shape=jax.ShapeDtypeStruct((n * TOP_K + 2 * MOE_BLOCK, n_sub, LANES), F32),
        compiler_params=_cparams(("arbitrary",)),
        name="expert_ffn",
    )(block_e, n_valid, pad_lo, pad_hi, dest, h2, w1, b1.reshape(e, 1, two_i), w2, b2.reshape(e, 1, d))
    return y4.reshape((n * TOP_K + 2 * MOE_BLOCK) * n_sub, LANES)


def _combine_kernel(*refs):
    y_refs, (w_ref, x1_ref, mod_ref, g_ref, o_ref) = refs[:TOP_K], refs[TOP_K:]
    tm, d = x1_ref.shape
    n_sub = d // LANES
    ff = jnp.concatenate(
        [functools.reduce(jnp.add, [y_refs[k][pl.ds(s, tm, stride=n_sub), :] * w_ref[:, k:k + 1]
                                    for k in range(TOP_K)])
         for s in range(n_sub)], axis=1)
    o_ref[...] = x1_ref[...] + mod_ref[0, 5:6, :] * _rms(ff, g_ref[...])


def _combine(y4, top_w, x1, mod, g_post, seq):
    n, d = x1.shape
    tm = min(ROW_TILE, seq)
    per_b = seq // tm
    n_sub = d // LANES
    y_specs = [pl.BlockSpec((tm * n_sub, LANES), functools.partial(lambda k, i: (k * (n // tm) + i, 0), k))
               for k in range(TOP_K)]
    return pl.pallas_call(
        _combine_kernel,
        grid=(n // tm,),
        in_specs=y_specs + [
                  pl.BlockSpec((tm, LANES), lambda i: (i, 0)),
                  pl.BlockSpec((tm, d), lambda i: (i, 0)),
                  pl.BlockSpec((1,) + mod.shape[1:], lambda i: (i // per_b, 0, 0)),
                  pl.BlockSpec((1, d), lambda i: (0, 0))],
        out_specs=pl.BlockSpec((tm, d), lambda i: (i, 0)),
        out_shape=jax.ShapeDtypeStruct((n, d), F32),
        compiler_params=_cparams(("arbitrary",)),
        name="combine",
    )(*([y4] * TOP_K), top_w, x1, mod, g_post.reshape(1, d))


def _layer(x, c, w_ada, b_ada, g_pre_mix, g_post_mix, w_in, rel_bias, w_branch_sb, w_branch_ca, w_gate,
           b_gate, w_out, g_pre_ffn, g_post_ffn, w_router, b_router, w_e1, b_e1, w_e2, b_e2):
    batch, seq, d = x.shape
    n = batch * seq
    sb_w = w_branch_sb.shape[0]
    ca_w = w_branch_ca.shape[0]
    n_exp = w_router.shape[1]
    x2 = x.reshape(n, d)

    mod = _modulation(c, w_ada, b_ada)

    scale = HEAD_DIM ** -0.5
    col_scale = np.ones((w_in.shape[1],), np.float32)
    col_scale[:sb_w] = scale
    col_scale[3 * sb_w:3 * sb_w + ca_w] = scale * LOG2_E
    w_in_b = (w_in * col_scale[None, :]).astype(BF16)
    proj, gate = _premix(x2, mod, g_pre_mix, w_in_b, w_gate.astype(BF16), b_gate, seq)

    sbl, cal = sb_w // LANES, ca_w // LANES
    o_sb = _sb_attention(proj, batch, seq, sb_w, 0, sbl, 2 * sbl)
    o_ca = _ca_attention(proj, _ca_tables(rel_bias), batch, seq, ca_w, 3 * sbl, 3 * sbl + cal, 3 * sbl + 2 * cal)

    wr_f32 = jnp.zeros((d, LANES), F32).at[:, :n_exp].set(w_router)
    wr_hi = wr_f32.astype(BF16)
    wr_lo = (wr_f32 - wr_hi.astype(F32)).astype(BF16)
    wr_pad = jnp.concatenate([wr_hi, wr_hi, wr_lo], axis=0)
    br_pad = jnp.full((1, LANES), NEG, F32).at[0, :n_exp].set(b_router)
    x1, h2, top_e, top_w, cnt = _postmix(o_sb, o_ca, gate, x2, mod, g_post_mix, g_pre_ffn,
                                         w_branch_sb.astype(BF16), w_branch_ca.astype(BF16),
                                         w_out.astype(BF16), wr_pad, br_pad, seq)

    nk = n * TOP_K
    counts = cnt[0, :n_exp].astype(I32)
    padded = (counts + MOE_BLOCK - 1) // MOE_BLOCK * MOE_BLOCK
    pad_end = jnp.cumsum(padded)
    pad_start = pad_end - padded
    group_start = jnp.zeros((SUBLANES, LANES), F32).at[0, :n_exp].set(pad_start.astype(F32))
    dest = _routing_ranks(top_e, group_start)[:, :TOP_K].reshape(nk)
    n_blocks = nk // MOE_BLOCK + n_exp
    assert n & (n - 1) == 0, "row ids are k * n + token with the token taken back by a bit mask"
    blk_start = jnp.arange(n_blocks, dtype=I32) * MOE_BLOCK
    block_e = jnp.minimum(jnp.sum((pad_end[None, :] <= blk_start[:, None]).astype(I32), axis=1), n_exp - 1)
    n_valid = (pad_end[-1] // MOE_BLOCK).astype(I32).reshape(1)

    rows_end = jnp.full((1,), n_blocks * MOE_BLOCK, I32)
    pad_lo = jnp.concatenate([pad_start + counts, pad_end[-1:], rows_end - MOE_BLOCK]).astype(I32)
    pad_hi = jnp.concatenate([pad_end, jnp.minimum(pad_end[-1:] + MOE_BLOCK, rows_end), rows_end]).astype(I32)
    y4 = _expert_ffn(h2, dest, block_e, n_valid, pad_lo, pad_hi, w_e1, b_e1, w_e2, b_e2)
    out = _combine(y4, top_w, x1, mod, g_post_ffn, seq)
    return out.reshape(batch, seq, d)


def kernel(x, c, w_ada, b_ada, g_pre_mix, g_post_mix, w_in, rel_bias, w_branch_sb, w_branch_ca, w_gate, b_gate,
           w_out, g_pre_ffn, g_post_ffn, w_router, b_router, w_e1, b_e1, w_e2, b_e2):
    for l in range(w_ada.shape[0]):
        x = _layer(x, c, w_ada[l], b_ada[l], g_pre_mix[l], g_post_mix[l], w_in[l], rel_bias[l],
                   w_branch_sb[l], w_branch_ca[l], w_gate[l], b_gate[l], w_out[l], g_pre_ffn[l],
                   g_post_ffn[l], w_router[l], b_router[l], w_e1[l], b_e1[l], w_e2[l], b_e2[l])
    return x
```

```python
import functools

import numpy as np
import jax
import jax.numpy as jnp
from jax import lax
from jax.experimental import pallas as pl
from jax.experimental.pallas import tpu as pltpu

F32 = jnp.float32
BF16 = jnp.bfloat16
I32 = jnp.int32

RMS_EPS = 1e-6
HEAD_DIM = 64
LANES = 128
SUBLANES = 8
CHUNK = 64
CA_LEFT_CHUNKS = 8
REL_CLIP = 256
TOP_K = 4
SWIGLU_ALPHA = 1.702
SWIGLU_LIMIT = 7.0
NEG = -1e30
LOG2_E = 1.4426950408889634

SB_TILE = 512
SB_KEY_TILE = 256
SB_ROWS = 128
CA_TILE = 4 * CHUNK
CA_KEY_TILES = CA_LEFT_CHUNKS * CHUNK // CA_TILE + 1
CA_WINDOW = CA_KEY_TILES * CA_TILE
ROW_TILE = 512
MOE_BLOCK = 512
MOE_COLS = 256
PAD_RUN = 8
VMEM_LIMIT = 56 * 1024 * 1024


def _cparams(sem):
    return pltpu.CompilerParams(dimension_semantics=sem, vmem_limit_bytes=VMEM_LIMIT)


def _dot(a, b):
    return jnp.dot(a, b, preferred_element_type=F32)


def _dot_nt(a, b):
    return lax.dot_general(a, b, (((1,), (1,)), ((), ())), preferred_element_type=F32)


def _rms(x, g):
    return x * lax.rsqrt(jnp.mean(x * x, axis=-1, keepdims=True) + RMS_EPS) * g


def _mod_kernel(c_ref, w_ref, b_ref, o_ref):
    o_ref[...] = jnp.dot(c_ref[...], w_ref[...], preferred_element_type=F32,
                         precision=lax.Precision.HIGHEST) + b_ref[...]


def _modulation(c, w_ada, b_ada):
    b, d = c.shape
    n_out = w_ada.shape[1]
    c_pad = jnp.zeros((SUBLANES, d), F32).at[:b].set(c)
    out = pl.pallas_call(
        _mod_kernel,
        grid=(n_out // d,),
        in_specs=[pl.BlockSpec((SUBLANES, d), lambda j: (0, 0)),
                  pl.BlockSpec((d, d), lambda j: (0, j)),
                  pl.BlockSpec((1, d), lambda j: (0, j))],
        out_specs=pl.BlockSpec((SUBLANES, d), lambda j: (0, j)),
        out_shape=jax.ShapeDtypeStruct((SUBLANES, n_out), F32),
        compiler_params=_cparams(("arbitrary",)),
        name="adaln_mod",
    )(c_pad, w_ada, b_ada.reshape(1, n_out))
    return out[:b].reshape(b, n_out // d, d)


def _premix_kernel(x_ref, mod_ref, g_ref, win_ref, wg_ref, bg_ref, proj_ref, gate_ref, *, col):
    h = _rms(x_ref[...], g_ref[...]) * (1.0 + mod_ref[0, 1:2, :]) + mod_ref[0, 0:1, :]
    hb = h.astype(BF16)
    for c0 in range(0, proj_ref.shape[1], col):
        proj_ref[:, c0:c0 + col] = _dot(hb, win_ref[:, c0:c0 + col]).astype(BF16)
    for c0 in range(0, gate_ref.shape[1], col):
        g = jax.nn.sigmoid(_dot(hb, wg_ref[:, c0:c0 + col]) + bg_ref[:, c0:c0 + col])
        gate_ref[:, c0:c0 + col] = g.astype(BF16)


def _premix(x2, mod, g_pre, w_in_b, w_gate_b, b_gate, seq):
    n, d = x2.shape
    pw, gw = w_in_b.shape[1], w_gate_b.shape[1]
    tm = min(ROW_TILE, seq)
    per_b = seq // tm
    return pl.pallas_call(
        functools.partial(_premix_kernel, col=512),
        grid=(n // tm,),
        in_specs=[pl.BlockSpec((tm, d), lambda i: (i, 0)),
                  pl.BlockSpec((1,) + mod.shape[1:], lambda i: (i // per_b, 0, 0)),
                  pl.BlockSpec((1, d), lambda i: (0, 0)),
                  pl.BlockSpec((d, pw), lambda i: (0, 0)),
                  pl.BlockSpec((d, gw), lambda i: (0, 0)),
                  pl.BlockSpec((1, gw), lambda i: (0, 0))],
        out_specs=[pl.BlockSpec((tm, pw), lambda i: (i, 0)),
                   pl.BlockSpec((tm, gw), lambda i: (i, 0))],
        out_shape=[jax.ShapeDtypeStruct((n, pw), BF16), jax.ShapeDtypeStruct((n, gw), BF16)],
        compiler_params=_cparams(("arbitrary",)),
        name="premix",
    )(x2, mod, g_pre.reshape(1, d), w_in_b, w_gate_b, b_gate.reshape(1, gw))


def _sb_kernel(q_ref, k_ref, v_ref, u_ref, o_ref, z_scr, cum_scr, carry_scr, acc_scr):
    t = acc_scr.shape[0] // 2 * SB_ROWS

    def query_tile(i, carry):
        rows = pl.ds(pl.multiple_of(i * t, t), t)
        _sb_query_tile(i, q_ref.at[rows, :], k_ref, v_ref, u_ref, o_ref.at[rows, :],
                       z_scr, cum_scr, carry_scr, acc_scr)
        return carry

    lax.fori_loop(0, q_ref.shape[0] // t, query_tile, 0)


def _sb_query_tile(i, q_ref, k_ref, v_ref, u_ref, o_ref, z_scr, cum_scr, carry_scr, acc_scr):
    t = q_ref.shape[0]
    tk = u_ref.shape[1]
    per_q = t // tk
    rows = SB_ROWS
    q2 = q_ref[...]
    lane = lax.broadcasted_iota(I32, (t, LANES), 1)
    u2 = u_ref[...]
    chains = [(hh, r0) for hh in range(2) for r0 in range(0, t, rows)]
    qs = []
    for hh, r0 in chains:
        in_head = (lane < HEAD_DIM) if hh == 0 else (lane >= HEAD_DIM)
        qs.append(jnp.where(in_head, q2, jnp.zeros_like(q2))[r0:r0 + rows])

    def causal(j_local):
        return [lax.broadcasted_iota(I32, (rows, tk), 1) + j_local * tk
                < lax.broadcasted_iota(I32, (rows, tk), 0) + r0 for _, r0 in chains]

    n_ch = len(chains)

    every = tuple(range(n_ch))

    def seeing(j_local):
        return tuple(c for c, (_, r0) in enumerate(chains) if r0 + rows - 1 > j_local * tk)

    def logits(j, slot, active=every):
        start = pl.multiple_of(j * tk, tk)
        kt = k_ref[pl.ds(start, tk), :]
        for c in active:
            z_scr[slot, c] = _dot_nt(qs[c], kt)

    def sums(slot, mask, active=every):
        for c in active:
            z = z_scr[slot, c].astype(BF16)
            sp = jnp.maximum(z, 0) + jnp.log(1 + jnp.exp(-jnp.abs(z)))
            if mask is not None:
                sp = jnp.where(mask[c], sp, jnp.zeros_like(sp))
            cum_scr[slot, c] = _dot(sp, u2)

    def finish(j, slot, mask, active=every):
        start = pl.multiple_of(j * tk, tk)
        vt = v_ref[pl.ds(start, tk), :]
        for c in active:
            cum = cum_scr[slot, c]
            carry = carry_scr[c]
            w = jnp.exp(z_scr[slot, c] - cum - jnp.concatenate([carry] * (tk // LANES), axis=1))
            if mask is not None:
                w = jnp.where(mask[c], w, 0.0)
            acc_scr[c] += _dot(w.astype(BF16), vt)
            carry_scr[c] = carry + jnp.broadcast_to(cum[:, 0:1], carry.shape)

    def sweep_tile(j, slot, mask, next_mask, active=every, next_active=every):
        logits(jnp.maximum(j - 1, 0), 1 - slot, next_active)
        finish(j, slot, mask, active)
        sums(1 - slot, next_mask, next_active)

    carry_scr[...] = jnp.zeros_like(carry_scr)
    acc_scr[...] = jnp.zeros_like(acc_scr)
    first = i * per_q
    logits(first + per_q - 1, (per_q - 1) % 2, seeing(per_q - 1))
    sums((per_q - 1) % 2, causal(per_q - 1), seeing(per_q - 1))
    for jl in range(per_q - 1, -1, -1):
        sweep_tile(first + jl, jl % 2, causal(jl), causal(jl - 1) if jl > 0 else None,
                   seeing(jl), seeing(jl - 1) if jl > 0 else every)

    def body(n, carry):
        j = first - 1 - 2 * n
        sweep_tile(j, 1, None, None)
        sweep_tile(j - 1, 0, None, None)
        return carry

    lax.fori_loop(0, first // 2, body, 0)
    accs = [jnp.concatenate([acc_scr[c] for c, (h2, _) in enumerate(chains) if h2 == hh], axis=0)
            for hh in range(2)]
    o_ref[...] = jnp.where(lane < HEAD_DIM, accs[0], accs[1]).astype(o_ref.dtype)


def _sb_attention(proj, batch, seq, width, q_col, k_col, v_col):
    n = proj.shape[0]
    t = min(SB_TILE, seq)
    tk = min(SB_KEY_TILE, t)
    assert (t // tk) % 2 == 0, "the sweep alternates two staging slots per key tile"
    nq = seq // t
    pairs = width // LANES
    n_chains = 2 * (t // SB_ROWS)
    tri = jnp.asarray((np.arange(tk)[:, None] >= np.arange(tk)[None, :]).astype(np.float32)).astype(BF16)
    return pl.pallas_call(
        _sb_kernel,
        grid=(batch, pairs),
        in_specs=[pl.BlockSpec((seq, LANES), lambda b, p: (b, q_col + p)),
                  pl.BlockSpec((seq, LANES), lambda b, p: (b, k_col + p)),
                  pl.BlockSpec((seq, LANES), lambda b, p: (b, v_col + p)),
                  pl.BlockSpec((tk, tk), lambda b, p: (0, 0))],
        out_specs=pl.BlockSpec((seq, LANES), lambda b, p: (b, p)),
        out_shape=jax.ShapeDtypeStruct((n, width), BF16),
        scratch_shapes=[pltpu.VMEM((2, n_chains, SB_ROWS, tk), F32), pltpu.VMEM((2, n_chains, SB_ROWS, tk), F32),
                        pltpu.VMEM((n_chains, SB_ROWS, LANES), F32), pltpu.VMEM((n_chains, SB_ROWS, LANES), F32)],
        compiler_params=_cparams(("arbitrary", "arbitrary")),
        name="stickbreak_attn",
    )(proj, proj, proj, tri)


def _ca_kernel(q_ref, k_ref, v_ref, t_ref, o_ref):
    t = t_ref.shape[2]
    window = t_ref.shape[3]
    lane = lax.broadcasted_iota(I32, (t, LANES), 1)
    chains = [(hh, r0) for hh in range(2) for r0 in range(0, t, CHUNK)]

    def tile(r, carry):
        rows = pl.ds(pl.multiple_of(r * t, t), t)
        start = pl.multiple_of(jnp.maximum(r - (CA_KEY_TILES - 1), 0) * t, t)
        placement = jnp.minimum(r, CA_KEY_TILES - 1)
        kw = k_ref[pl.ds(start, window), :]
        vw = v_ref[pl.ds(start, window), :]
        q2 = q_ref[rows, :]
        zs = []
        for hh, r0 in chains:
            in_head = (lane < HEAD_DIM) if hh == 0 else (lane >= HEAD_DIM)
            qh = jnp.where(in_head, q2, jnp.zeros_like(q2))[r0:r0 + CHUNK]
            zs.append(_dot_nt(qh, kw) + t_ref[placement, hh, r0:r0 + CHUNK, :])
        ps = [jnp.exp2(z - jnp.max(z, axis=-1, keepdims=True)) for z in zs]
        outs = [_dot(p.astype(BF16), vw) / jnp.sum(p, axis=-1, keepdims=True) for p in ps]
        heads = [jnp.concatenate([o for (h2, _), o in zip(chains, outs) if h2 == hh], axis=0)
                 for hh in range(2)]
        o_ref[rows, :] = jnp.where(lane < HEAD_DIM, heads[0], heads[1]).astype(o_ref.dtype)
        return carry

    lax.fori_loop(0, q_ref.shape[0] // t, tile, 0)


def _ca_tables(rel_bias):
    h = rel_bias.shape[0]
    left = CA_LEFT_CHUNKS * CHUNK
    period = CA_WINDOW + CA_TILE
    u = np.arange(period)
    rel = np.where(u < CA_WINDOW, left - u, left + (period - u))
    v = rel_bias[:, np.clip(rel, -REL_CLIP, REL_CLIP) + REL_CLIP].astype(F32) * LOG2_E
    flat = jnp.tile(v, (1, CA_TILE))[:, :CA_TILE * (period - 1)]
    bias = flat.reshape(h, CA_TILE, period - 1)[:, :, :CA_WINDOW]
    i = np.arange(CA_TILE)[:, None]
    j = np.arange(CA_WINDOW)[None, :]
    first = (i // CHUNK) * CHUNK
    visible = (j >= first) & (j < first + (CA_LEFT_CHUNKS + 1) * CHUNK)
    full = jnp.where(visible[None], bias, NEG)
    tables = []
    for r in range(CA_KEY_TILES):
        shift = (CA_KEY_TILES - 1 - r) * CA_TILE
        tables.append(jnp.pad(full[:, :, shift:], ((0, 0), (0, 0), (0, shift)), constant_values=NEG))
    return jnp.stack(tables)


def _ca_attention(proj, tables, batch, seq, width, q_col, k_col, v_col):
    n = proj.shape[0]
    t = CA_TILE
    nq = seq // t
    pairs = width // LANES
    assert seq >= CA_WINDOW
    return pl.pallas_call(
        _ca_kernel,
        grid=(batch, pairs),
        in_specs=[pl.BlockSpec((seq, LANES), lambda b, p: (b, q_col + p)),
                  pl.BlockSpec((seq, LANES), lambda b, p: (b, k_col + p)),
                  pl.BlockSpec((seq, LANES), lambda b, p: (b, v_col + p)),
                  pl.BlockSpec((CA_KEY_TILES, 2, t, CA_WINDOW), lambda b, p: (0, p, 0, 0))],
        out_specs=pl.BlockSpec((seq, LANES), lambda b, p: (b, p)),
        out_shape=jax.ShapeDtypeStruct((n, width), BF16),
        compiler_params=_cparams(("arbitrary", "arbitrary")),
        name="chunkband_attn",
    )(proj, proj, proj, tables)


def _postmix_kernel(osb_ref, oca_ref, gate_ref, x_ref, mod_ref, gpost_ref, gffn_ref, wbs_ref, wbc_ref,
                    wout_ref, wr_ref, br_ref, x1_ref, h2_ref, tope_ref, topw_ref, cnt_ref):
    d = x_ref.shape[1]
    y_sb = _dot(osb_ref[...], wbs_ref[...])
    y_ca = _dot(oca_ref[...], wbc_ref[...])
    mixed_in = gate_ref[:, :d].astype(F32) * y_sb + gate_ref[:, d:].astype(F32) * y_ca
    mixed = _dot(mixed_in.astype(BF16), wout_ref[...])
    x1 = x_ref[...] + mod_ref[0, 2:3, :] * _rms(mixed, gpost_ref[...])
    x1_ref[...] = x1
    h2 = _rms(x1, gffn_ref[...]) * (1.0 + mod_ref[0, 4:5, :]) + mod_ref[0, 3:4, :]
    n_sub = d // LANES
    for s in range(n_sub):
        h2_ref[pl.ds(s, h2.shape[0], stride=n_sub), :] = h2[:, s * LANES:(s + 1) * LANES]
    h_hi = h2.astype(BF16)
    h_lo = (h2 - h_hi.astype(F32)).astype(BF16)
    logits = _dot(jnp.concatenate([h_hi, h_lo, h_hi], axis=1), wr_ref[...]) + br_ref[...]
    lane = lax.broadcasted_iota(I32, logits.shape, 1)
    lane_f = lane.astype(F32)
    vals, idxs = [], []
    for _ in range(TOP_K):
        m = jnp.max(logits, axis=-1, keepdims=True)
        idx_f = jnp.min(jnp.where(logits == m, lane_f, float(LANES)), axis=-1, keepdims=True)
        vals.append(m)
        idxs.append(idx_f.astype(I32))
        logits = jnp.where(lane_f == idx_f, 2 * NEG, logits)
    ex = [jnp.exp(v - vals[0]) for v in vals]
    denom = functools.reduce(jnp.add, ex)
    tope = jnp.zeros(logits.shape, I32)
    topw = jnp.zeros(logits.shape, F32)
    for k in range(TOP_K):
        tope = jnp.where(lane == k, idxs[k], tope)
        topw = jnp.where(lane == k, ex[k] / denom, topw)
    tope_ref[...] = tope
    topw_ref[...] = topw

    @pl.when(pl.program_id(0) == 0)
    def _():
        cnt_ref[...] = jnp.zeros_like(cnt_ref)

    chosen = functools.reduce(jnp.logical_or, [lane == idx for idx in idxs])
    cnt_ref[0:1, :] += jnp.sum(chosen.astype(F32), axis=0, keepdims=True)


def _postmix(o_sb, o_ca, gate, x2, mod, g_post, g_ffn, wbs_b, wbc_b, wout_b, wr_pad, br_pad, seq):
    n, d = x2.shape
    w = o_sb.shape[1]
    tm = min(ROW_TILE, seq)
    per_b = seq // tm
    row = lambda i: (i, 0)
    fixed = lambda i: (0, 0)
    return pl.pallas_call(
        _postmix_kernel,
        grid=(n // tm,),
        in_specs=[pl.BlockSpec((tm, w), row), pl.BlockSpec((tm, w), row),
                  pl.BlockSpec((tm, 2 * d), row), pl.BlockSpec((tm, d), row),
                  pl.BlockSpec((1,) + mod.shape[1:], lambda i: (i // per_b, 0, 0)),
                  pl.BlockSpec((1, d), fixed), pl.BlockSpec((1, d), fixed),
                  pl.BlockSpec((w, d), fixed), pl.BlockSpec((w, d), fixed),
                  pl.BlockSpec((d, d), fixed), pl.BlockSpec((3 * d, LANES), fixed),
                  pl.BlockSpec((1, LANES), fixed)],
        out_specs=[pl.BlockSpec((tm, d), row), pl.BlockSpec((tm * (d // LANES), LANES), row),
                   pl.BlockSpec((tm, LANES), row), pl.BlockSpec((tm, LANES), row),
                   pl.BlockSpec((SUBLANES, LANES), fixed)],
        out_shape=[jax.ShapeDtypeStruct((n, d), F32), jax.ShapeDtypeStruct((n * (d // LANES), LANES), F32),
                   jax.ShapeDtypeStruct((n, LANES), I32), jax.ShapeDtypeStruct((n, LANES), F32),
                   jax.ShapeDtypeStruct((SUBLANES, LANES), F32)],
        compiler_params=_cparams(("arbitrary",)),
        name="postmix_router",
    )(o_sb, o_ca, gate, x2, mod, g_post.reshape(1, d), g_ffn.reshape(1, d), wbs_b, wbc_b, wout_b,
      wr_pad, br_pad)


def _rank_kernel(e_ref, l_ref, start_ref, dest_ref, run_ref):
    @pl.when(pl.program_id(0) == 0)
    def _():
        run_ref[...] = start_ref[...]

    e = e_ref[...]
    lane = lax.broadcasted_iota(I32, e.shape, 1)
    hots = [lane == e[:, k:k + 1] for k in range(TOP_K)]
    chosen = functools.reduce(jnp.logical_or, hots)
    before = _dot(l_ref[...], chosen.astype(BF16)) + run_ref[0:1, :]
    dest = jnp.zeros(e.shape, I32)
    for k in range(TOP_K):
        dk = jnp.sum(jnp.where(hots[k], before, 0.0), axis=-1, keepdims=True).astype(I32)
        dest = jnp.where(lane == k, dk, dest)
    dest_ref[...] = dest
    run_ref[0:1, :] += jnp.sum(chosen.astype(F32), axis=0, keepdims=True)


def _routing_ranks(top_e, group_start):
    n = top_e.shape[0]
    t = min(ROW_TILE, n)
    strict_lower = (jnp.arange(t)[:, None] > jnp.arange(t)[None, :]).astype(BF16)
    return pl.pallas_call(
        _rank_kernel,
        grid=(n // t,),
        in_specs=[pl.BlockSpec((t, LANES), lambda i: (i, 0)),
                  pl.BlockSpec((t, t), lambda i: (0, 0)),
                  pl.BlockSpec((SUBLANES, LANES), lambda i: (0, 0))],
        out_specs=pl.BlockSpec((t, LANES), lambda i: (i, 0)),
        out_shape=jax.ShapeDtypeStruct((n, LANES), I32),
        scratch_shapes=[pltpu.VMEM((SUBLANES, LANES), F32)],
        compiler_params=_cparams(("arbitrary",)),
        name="routing_ranks",
    )(top_e, strict_lower, group_start)


def _moe_kernel(be_ref, nvalid_ref, padlo_ref, padhi_ref, dest_ref, h2_hbm, w1_ref, b1_ref, w2_ref, b2_ref, y4_hbm,
                xbuf0, xbuf1, ybuf0, ybuf1, w1b, w2b, hid, sem_in, sem_out, rowflat_ref, *, n_tokens):
    i = pl.program_id(0)
    n_blocks = pl.num_programs(0)
    nv = nvalid_ref[0]
    n_sub = h2_hbm.shape[1]
    xbuf, ybuf = (xbuf0, xbuf1), (ybuf0, ybuf1)
    blk = xbuf0.shape[0] // n_sub
    d = n_sub * LANES
    di = w2b.shape[0]

    def tile(buf, r):
        return buf.at[pl.ds(r * n_sub, n_sub), :]

    def gather_copy(flat, r, slot):
        return pltpu.make_async_copy(h2_hbm.at[flat & (n_tokens - 1)], tile(xbuf[slot], r), sem_in.at[slot])

    def scatter_copy(flat, r, slot):
        return pltpu.make_async_copy(tile(ybuf[slot], r), y4_hbm.at[flat], sem_out.at[slot])

    def wait_gather(slot):
        pltpu.make_async_copy(xbuf[slot], xbuf[slot], sem_in.at[slot]).wait()

    def wait_scatter(slot):
        pltpu.make_async_copy(ybuf[slot], ybuf[slot], sem_out.at[slot]).wait()

    def step(slot):
        other = 1 - slot

        @pl.when(i == 0)
        def _():
            ybuf0[...] = jnp.zeros_like(ybuf0)
            ybuf1[...] = jnp.zeros_like(ybuf1)

            def clear(r, c):
                for s in range(2):
                    scatter_copy(n_tokens * TOP_K + s * blk + r, r, s).start()
                return c
            lax.fori_loop(0, blk, clear, 0)
            wait_scatter(0)
            wait_scatter(1)

            def first(r, c):
                gather_copy(rowflat_ref[r], r, slot).start()
                return c
            lax.fori_loop(0, blk, first, 0)

        wait_gather(slot)

        @pl.when(jnp.logical_or(i == 0, be_ref[i] != be_ref[jnp.maximum(i - 1, 0)]))
        def _():
            w1b[...] = w1_ref[0].astype(BF16)
            w2b[...] = w2_ref[0].astype(BF16)

        nxt = jnp.minimum(i + 1, n_blocks - 1) * blk
        prev = jnp.where(i == 0, n_blocks - 1, i - 1) * blk
        n_hid, n_out = di // MOE_COLS, d // MOE_COLS
        def issue(stage):
            if stage < n_hid:
                for r in range(stage * blk // n_hid, (stage + 1) * blk // n_hid):
                    gather_copy(rowflat_ref[nxt + r], r, other).start(priority=r % 2)
            else:
                st = stage - n_hid
                for r in range(st * blk // n_out, (st + 1) * blk // n_out):
                    scatter_copy(rowflat_ref[prev + r], r, other).start(priority=r % 2)

        xb = jnp.concatenate([xbuf[slot][pl.ds(s, blk, stride=n_sub), :] for s in range(n_sub)],
                             axis=1).astype(BF16)
        for c in range(n_hid):
            issue(c)
            c0 = c * MOE_COLS
            gate = _dot(xb, w1b[:, c0:c0 + MOE_COLS]) + b1_ref[0, :, c0:c0 + MOE_COLS]
            up = _dot(xb, w1b[:, di + c0:di + c0 + MOE_COLS]) + b1_ref[0, :, di + c0:di + c0 + MOE_COLS]
            gate = jnp.minimum(gate, SWIGLU_LIMIT)
            up = jnp.clip(up, -SWIGLU_LIMIT, SWIGLU_LIMIT)
            hid[:, c0:c0 + MOE_COLS] = (gate * jax.nn.sigmoid(SWIGLU_ALPHA * gate) * (up + 1.0)).astype(BF16)

        @pl.when(i >= 1)
        def _():
            wait_scatter(slot)

        hb = hid[...]
        for c in range(n_out):
            issue(n_hid + c)
            c0 = c * MOE_COLS
            y = _dot(hb, w2b[:, c0:c0 + MOE_COLS]) + b2_ref[0, :, c0:c0 + MOE_COLS]
            for s in range(MOE_COLS // LANES):
                ybuf[slot][pl.ds(c0 // LANES + s, blk, stride=n_sub), :] = y[:, s * LANES:(s + 1) * LANES]

        @pl.when(i == nv - 1)
        def _():
            def last(r, c):
                scatter_copy(rowflat_ref[i * blk + r], r, slot).start()
                return c
            lax.fori_loop(0, blk, last, 0)
            wait_scatter(other)
            wait_scatter(slot)
            wait_gather(other)

    @pl.when(i == 0)
    def _():
        def pad_range(g, c):
            lo = padlo_ref[g]

            def pad(q, c2):
                for u in range(PAD_RUN):
                    r = lo + q * PAD_RUN + u
                    rowflat_ref[r] = n_tokens * TOP_K + (r & (2 * blk - 1))
                return c2
            return lax.fori_loop(0, (padhi_ref[g] - lo + PAD_RUN - 1) // PAD_RUN, pad, c)
        lax.fori_loop(0, padlo_ref.shape[0], pad_range, 0)

        def place(t, c):
            for k in range(TOP_K):
                rowflat_ref[dest_ref[t * TOP_K + k]] = k * n_tokens + t
            return c
        lax.fori_loop(0, n_tokens, place, 0, unroll=8)

    @pl.when(jnp.logical_and(i < nv, i % 2 == 0))
    def _():
        step(0)

    @pl.when(jnp.logical_and(i < nv, i % 2 == 1))
    def _():
        step(1)


def _expert_ffn(h2_tiles, dest, block_e, n_valid, pad_lo, pad_hi, w1, b1, w2, b2):
    e, d, two_i = w1.shape
    n_sub = d // LANES
    n = h2_tiles.shape[0] // n_sub
    h2 = h2_tiles.reshape(n, n_sub, LANES)
    n_blocks = block_e.shape[0]
    by_expert = lambda i, be, *_: (be[i], 0, 0)
    grid_spec = pltpu.PrefetchScalarGridSpec(
        num_scalar_prefetch=5,
        grid=(n_blocks,),
        in_specs=[pl.BlockSpec(memory_space=pl.ANY),
                  pl.BlockSpec((1, d, two_i), by_expert),
                  pl.BlockSpec((1, 1, two_i), by_expert),
                  pl.BlockSpec((1, two_i // 2, d), by_expert),
                  pl.BlockSpec((1, 1, d), by_expert)],
        out_specs=pl.BlockSpec(memory_space=pl.ANY),
        scratch_shapes=[pltpu.VMEM((MOE_BLOCK * n_sub, LANES), F32)] * 4 + [
                        pltpu.VMEM((d, two_i), BF16), pltpu.VMEM((two_i // 2, d), BF16),
                        pltpu.VMEM((MOE_BLOCK, two_i // 2), BF16),
                        pltpu.SemaphoreType.DMA((2,)), pltpu.SemaphoreType.DMA((2,)),
                        pltpu.SMEM((n_blocks * MOE_BLOCK + PAD_RUN,), I32)],
    )
    y4 = pl.pallas_call(
        functools.partial(_moe_kernel, n_tokens=n),
        grid_spec=grid_spec,
        out_shape=jax.ShapeDtypeStruct((n * TOP_K + 2 * MOE_BLOCK, n_sub, LANES), F32),
        compiler_params=_cparams(("arbitrary",)),
        name="expert_ffn",
    )(block_e, n_valid, pad_lo, pad_hi, dest, h2, w1, b1.reshape(e, 1, two_i), w2, b2.reshape(e, 1, d))
    return y4.reshape((n * TOP_K + 2 * MOE_BLOCK) * n_sub, LANES)


def _combine_kernel(*refs):
    y_refs, (w_ref, x1_ref, mod_ref, g_ref, o_ref) = refs[:TOP_K], refs[TOP_K:]
    tm, d = x1_ref.shape
    n_sub = d // LANES
    ff = jnp.concatenate(
        [functools.reduce(jnp.add, [y_refs[k][pl.ds(s, tm, stride=n_sub), :] * w_ref[:, k:k + 1]
                                    for k in range(TOP_K)])
         for s in range(n_sub)], axis=1)
    o_ref[...] = x1_ref[...] + mod_ref[0, 5:6, :] * _rms(ff, g_ref[...])


def _combine(y4, top_w, x1, mod, g_post, seq):
    n, d = x1.shape
    tm = min(ROW_TILE, seq)
    per_b = seq // tm
    n_sub = d // LANES
    y_specs = [pl.BlockSpec((tm * n_sub, LANES), functools.partial(lambda k, i: (k * (n // tm) + i, 0), k))
               for k in range(TOP_K)]
    return pl.pallas_call(
        _combine_kernel,
        grid=(n // tm,),
        in_specs=y_specs + [
                  pl.BlockSpec((tm, LANES), lambda i: (i, 0)),
                  pl.BlockSpec((tm, d), lambda i: (i, 0)),
                  pl.BlockSpec((1,) + mod.shape[1:], lambda i: (i // per_b, 0, 0)),
                  pl.BlockSpec((1, d), lambda i: (0, 0))],
        out_specs=pl.BlockSpec((tm, d), lambda i: (i, 0)),
        out_shape=jax.ShapeDtypeStruct((n, d), F32),
        compiler_params=_cparams(("arbitrary",)),
        name="combine",
    )(*([y4] * TOP_K), top_w, x1, mod, g_post.reshape(1, d))


def _layer(x, c, w_ada, b_ada, g_pre_mix, g_post_mix, w_in, rel_bias, w_branch_sb, w_branch_ca, w_gate,
           b_gate, w_out, g_pre_ffn, g_post_ffn, w_router, b_router, w_e1, b_e1, w_e2, b_e2):
    batch, seq, d = x.shape
    n = batch * seq
    sb_w = w_branch_sb.shape[0]
    ca_w = w_branch_ca.shape[0]
    n_exp = w_router.shape[1]
    x2 = x.reshape(n, d)

    mod = _modulation(c, w_ada, b_ada)

    scale = HEAD_DIM ** -0.5
    col_scale = np.ones((w_in.shape[1],), np.float32)
    col_scale[:sb_w] = scale
    col_scale[3 * sb_w:3 * sb_w + ca_w] = scale * LOG2_E
    w_in_b = (w_in * col_scale[None, :]).astype(BF16)
    proj, gate = _premix(x2, mod, g_pre_mix, w_in_b, w_gate.astype(BF16), b_gate, seq)

    sbl, cal = sb_w // LANES, ca_w // LANES
    o_sb = _sb_attention(proj, batch, seq, sb_w, 0, sbl, 2 * sbl)
    o_ca = _ca_attention(proj, _ca_tables(rel_bias), batch, seq, ca_w, 3 * sbl, 3 * sbl + cal, 3 * sbl + 2 * cal)

    wr_f32 = jnp.zeros((d, LANES), F32).at[:, :n_exp].set(w_router)
    wr_hi = wr_f32.astype(BF16)
    wr_lo = (wr_f32 - wr_hi.astype(F32)).astype(BF16)
    wr_pad = jnp.concatenate([wr_hi, wr_hi, wr_lo], axis=0)
    br_pad = jnp.full((1, LANES), NEG, F32).at[0, :n_exp].set(b_router)
    x1, h2, top_e, top_w, cnt = _postmix(o_sb, o_ca, gate, x2, mod, g_post_mix, g_pre_ffn,
                                         w_branch_sb.astype(BF16), w_branch_ca.astype(BF16),
                                         w_out.astype(BF16), wr_pad, br_pad, seq)

    nk = n * TOP_K
    counts = cnt[0, :n_exp].astype(I32)
    padded = (counts + MOE_BLOCK - 1) // MOE_BLOCK * MOE_BLOCK
    pad_end = jnp.cumsum(padded)
    pad_start = pad_end - padded
    group_start = jnp.zeros((SUBLANES, LANES), F32).at[0, :n_exp].set(pad_start.astype(F32))
    dest = _routing_ranks(top_e, group_start)[:, :TOP_K].reshape(nk)
    n_blocks = nk // MOE_BLOCK + n_exp
    assert n & (n - 1) == 0, "row ids are k * n + token with the token taken back by a bit mask"
    blk_start = jnp.arange(n_blocks, dtype=I32) * MOE_BLOCK
    block_e = jnp.minimum(jnp.sum((pad_end[None, :] <= blk_start[:, None]).astype(I32), axis=1), n_exp - 1)
    n_valid = (pad_end[-1] // MOE_BLOCK).astype(I32).reshape(1)

    rows_end = jnp.full((1,), n_blocks * MOE_BLOCK, I32)
    pad_lo = jnp.concatenate([pad_start + counts, pad_end[-1:], rows_end - MOE_BLOCK]).astype(I32)
    pad_hi = jnp.concatenate([pad_end, jnp.minimum(pad_end[-1:] + MOE_BLOCK, rows_end), rows_end]).astype(I32)
    y4 = _expert_ffn(h2, dest, block_e, n_valid, pad_lo, pad_hi, w_e1, b_e1, w_e2, b_e2)
    out = _combine(y4, top_w, x1, mod, g_post_ffn, seq)
    return out.reshape(batch, seq, d)


def kernel(x, c, w_ada, b_ada, g_pre_mix, g_post_mix, w_in, rel_bias, w_branch_sb, w_branch_ca, w_gate, b_gate,
           w_out, g_pre_ffn, g_post_ffn, w_router, b_router, w_e1, b_e1, w_e2, b_e2):
    for l in range(w_ada.shape[0]):
        x = _layer(x, c, w_ada[l], b_ada[l], g_pre_mix[l], g_post_mix[l], w_in[l], rel_bias[l],
                   w_branch_sb[l], w_branch_ca[l], w_gate[l], b_gate[l], w_out[l], g_pre_ffn[l],
                   g_post_ffn[l], w_router[l], b_router[l], w_e1[l], b_e1[l], w_e2[l], b_e2[l])
    return x
```
